```python
import jax, jax.numpy as jnp
from jax import lax
import numpy as np

D_MODEL = 2048
BATCH = 4
SEQ = 2048
DEPTH = 2
DEC_BATCH = 32
DEC_SEQ = 16
PAST_LEN = 1024

CHUNK = 64
Q_BLOCK = 128
HEAD_DIM = 128
FOX_HEADS = 8
SB_HEADS = 8
FOX_WIDTH = FOX_HEADS * HEAD_DIM
SB_WIDTH = SB_HEADS * HEAD_DIM
N_AB = 4 * FOX_WIDTH + 4 * SB_WIDTH + FOX_HEADS
MLP_EXPAND = 2
C_WIDTH = MLP_EXPAND * D_MODEL
C_GROUPS = 16
C_GROUP_DIM = C_WIDTH // C_GROUPS
C_CHUNK = 128
RMS_EPS = 1e-6
LN_EPS = 1e-5
FORGET_BIAS = 2.0
ATTN_SCALE = HEAD_DIM ** -0.5

kernel_name = 'hybrid_fox_stickbreak_gmlp_stream_step'


def rmsnorm(x, g):
    x32 = x.astype(jnp.float32)
    y = x32 * lax.rsqrt(jnp.mean(x32 * x32, axis=-1, keepdims=True) + RMS_EPS)
    return (y * g.astype(jnp.float32)).astype(x.dtype)


def layernorm(x, g, b):
    x32 = x.astype(jnp.float32)
    mu = jnp.mean(x32, axis=-1, keepdims=True)
    xc = x32 - mu
    y = xc * lax.rsqrt(jnp.mean(xc * xc, axis=-1, keepdims=True) + LN_EPS)
    return (y * g.astype(jnp.float32) + b.astype(jnp.float32)).astype(x.dtype)


def _heads(t, n):
    return t.reshape(t.shape[0], t.shape[1], n, HEAD_DIM)


def ab_project(h, w_in0, b_forget):
    p = jnp.einsum('bsd,dn->bsn', h, w_in0)
    o = 0
    parts = []
    for width in (FOX_WIDTH,) * 4 + (SB_WIDTH,) * 4:
        parts.append(p[..., o:o + width])
        o += width
    f_logit = p[..., o:o + FOX_HEADS].astype(jnp.float32) + b_forget.astype(jnp.float32)
    logf = jax.nn.log_sigmoid(f_logit)
    fq, fk, fv = (_heads(t, FOX_HEADS) for t in parts[0:3])
    sq, sk, sv = (_heads(t, SB_HEADS) for t in parts[4:7])
    return fq, fk, fv, parts[3], logf, sq, sk, sv, parts[7]


def fox_core(q, k, v, Fq, Fk, qpos, kpos):
    s = jnp.einsum('bqhd,bkhd->bhqk', q, k).astype(jnp.float32) * ATTN_SCALE
    s = s + (jnp.swapaxes(Fq, 1, 2)[..., :, None] - jnp.swapaxes(Fk, 1, 2)[..., None, :])
    causal = kpos[None, :] <= qpos[:, None]
    s = jnp.where(causal, s, -jnp.inf)
    p = jax.nn.softmax(s, axis=-1)
    return jnp.einsum('bhqk,bkhd->bqhd', p.astype(v.dtype), v)


def sb_core(q, k, v, qpos, kpos):
    z = jnp.einsum('bqhd,bkhd->bhqk', q, k).astype(jnp.float32) * ATTN_SCALE
    before = kpos[None, :] < qpos[:, None]
    log_stay = jnp.where(before, jax.nn.log_sigmoid(-z), 0.0)
    after = lax.cumsum(log_stay, axis=3, reverse=True) - log_stay
    a = jnp.where(before, jnp.exp(jax.nn.log_sigmoid(z) + after), 0.0)
    return jnp.einsum('bhqk,bkhd->bqhd', a.astype(v.dtype), v)


def fox_prompt(q, k, v, logf):
    B_, S_, H, Dh = q.shape
    nb = S_ // Q_BLOCK
    F = jnp.cumsum(logf, axis=1)
    kpos = jnp.arange(S_)
    qb = q.reshape(B_, nb, Q_BLOCK, H, Dh).swapaxes(0, 1)
    Fb = F.reshape(B_, nb, Q_BLOCK, H).swapaxes(0, 1)

    def block(args):
        i, qi, Fi = args
        return fox_core(qi, k, v, Fi, F, i * Q_BLOCK + jnp.arange(Q_BLOCK), kpos)

    o = lax.map(block, (jnp.arange(nb), qb, Fb))
    return o.swapaxes(0, 1).reshape(B_, S_, H * Dh)


def sb_prompt(q, k, v):
    B_, S_, H, Dh = q.shape
    nb = S_ // Q_BLOCK
    kpos = jnp.arange(S_)
    qb = q.reshape(B_, nb, Q_BLOCK, H, Dh).swapaxes(0, 1)

    def block(args):
        i, qi = args
        return sb_core(qi, k, v, i * Q_BLOCK + jnp.arange(Q_BLOCK), kpos)

    o = lax.map(block, (jnp.arange(nb), qb))
    return o.swapaxes(0, 1).reshape(B_, S_, H * Dh)


def c_project(h, w_in1, ln_g, ln_b):
    p = jnp.einsum('bsd,dn->bsn', h, w_in1)
    u = p[..., :C_WIDTH]
    v = layernorm(p[..., C_WIDTH:2 * C_WIDTH], ln_g, ln_b)
    z = p[..., 2 * C_WIDTH:]
    return u, v, z


def spatial_mix(v, w_sp, b_sp):
    B_, N, L, _ = v.shape
    vg = v.reshape(B_, N, L, C_GROUPS, C_GROUP_DIM)
    causal = jnp.tril(jnp.ones((L, L), dtype=bool))
    w = jnp.where(causal, w_sp[:, :L, :L], 0.0).astype(v.dtype)
    mixed = jnp.einsum('gts,bnsgc->bntgc', w, vg) + b_sp[:, :L].T[None, None, :, :, None].astype(v.dtype)
    return mixed.reshape(B_, N, L, C_WIDTH)


def setup_inputs(seed: int = 0) -> dict:
    key = jax.random.key(seed)
    ks = jax.random.split(key, 20)
    nrm = lambda k, shape, scale: jax.random.normal(k, shape, jnp.float32) * scale
    kv_shape = (DEC_BATCH, PAST_LEN, FOX_HEADS, HEAD_DIM)
    sb_shape = (DEC_BATCH, PAST_LEN, SB_HEADS, HEAD_DIM)
    return {
        'x_prompt': nrm(ks[0], (BATCH, SEQ, D_MODEL), 1.0),
        'x_sample': nrm(ks[1], (DEC_BATCH, DEC_SEQ, D_MODEL), 1.0),
        'cache_fox_k': nrm(ks[2], kv_shape, 1.0),
        'cache_fox_v': nrm(ks[3], kv_shape, 1.0),
        'cache_fox_logf': jax.nn.log_sigmoid(FORGET_BIAS + nrm(ks[4], (DEC_BATCH, PAST_LEN, FOX_HEADS), 1.0)),
        'cache_sb_k': nrm(ks[5], sb_shape, 1.0),
        'cache_sb_v': nrm(ks[6], sb_shape, 1.0),
        'norm0_g': 1.0 + nrm(ks[7], (D_MODEL,), 0.02),
        'w_in0': nrm(ks[8], (D_MODEL, N_AB), D_MODEL ** -0.5),
        'b_forget': FORGET_BIAS + nrm(ks[9], (FOX_HEADS,), 0.1),
        'w_out0': nrm(ks[10], (FOX_WIDTH + SB_WIDTH, D_MODEL), (FOX_WIDTH + SB_WIDTH) ** -0.5),
        'norm1_g': 1.0 + nrm(ks[11], (D_MODEL,), 0.02),
        'w_in1': nrm(ks[12], (D_MODEL, 3 * C_WIDTH), D_MODEL ** -0.5),
        'sgu_ln_g': 1.0 + nrm(ks[13], (C_WIDTH,), 0.02),
        'sgu_ln_b': nrm(ks[14], (C_WIDTH,), 0.02),
        'w_sp': nrm(ks[15], (C_GROUPS, C_CHUNK, C_CHUNK), C_CHUNK ** -0.5),
        'b_sp': 1.0 + nrm(ks[16], (C_GROUPS, C_CHUNK), 0.1),
        'w_out1': nrm(ks[17], (C_WIDTH, D_MODEL), C_WIDTH ** -0.5),
        'final_g': 1.0 + nrm(ks[18], (D_MODEL,), 0.02),
    }


def reference(x_prompt, x_sample, cache_fox_k, cache_fox_v, cache_fox_logf, cache_sb_k, cache_sb_v,
              norm0_g, w_in0, b_forget, w_out0, norm1_g, w_in1, sgu_ln_g, sgu_ln_b, w_sp, b_sp, w_out1, final_g):
    hp = x_prompt
    hs = x_sample
    S_ = x_prompt.shape[1]
    P = cache_fox_k.shape[1]
    T = x_sample.shape[1]
    for layer in range(DEPTH):
        if layer % 2 == 0:
            fq, fk, fv, fz, logf_p, sq, sk, sv, sz = ab_project(rmsnorm(hp, norm0_g), w_in0, b_forget)
            fo = fox_prompt(fq, fk, fv, logf_p)
            so = sb_prompt(sq, sk, sv)
            mix = jnp.concatenate([fo * jax.nn.silu(fz), so * jax.nn.silu(sz)], axis=-1)
            hp = hp + jnp.einsum('bsn,nd->bsd', mix, w_out0)
            fox_k_prompt, fox_v_prompt, fox_logf_prompt = fk, fv, logf_p
            sb_k_prompt, sb_v_prompt = sk, sv
            gq, gk, gv, gz, logf_s, tq, tk, tv, tz = ab_project(rmsnorm(hs, norm0_g), w_in0, b_forget)
            qpos = P + jnp.arange(T)
            kpos = jnp.arange(P + T)
            k_all = jnp.concatenate([cache_fox_k.astype(gk.dtype), gk], axis=1)
            v_all = jnp.concatenate([cache_fox_v.astype(gv.dtype), gv], axis=1)
            F_all = jnp.cumsum(jnp.concatenate([cache_fox_logf.astype(jnp.float32), logf_s], axis=1), axis=1)
            go = fox_core(gq, k_all, v_all, F_all[:, P:], F_all, qpos, kpos).reshape(T and hs.shape[0], T, FOX_WIDTH)
            sk_all = jnp.concatenate([cache_sb_k.astype(tk.dtype), tk], axis=1)
            sv_all = jnp.concatenate([cache_sb_v.astype(tv.dtype), tv], axis=1)
            to = sb_core(tq, sk_all, sv_all, qpos, kpos).reshape(hs.shape[0], T, SB_WIDTH)
            mix_s = jnp.concatenate([go * jax.nn.silu(gz), to * jax.nn.silu(tz)], axis=-1)
            hs = hs + jnp.einsum('bsn,nd->bsd', mix_s, w_out0)
            fox_k_sample, fox_v_sample, fox_logf_sample = gk, gv, logf_s
            sb_k_sample, sb_v_sample = tk, tv
        else:
            u, v, z = c_project(rmsnorm(hp, norm1_g), w_in1, sgu_ln_g, sgu_ln_b)
            Bp = hp.shape[0]
            mixed = spatial_mix(v.reshape(Bp, S_ // C_CHUNK, C_CHUNK, C_WIDTH), w_sp, b_sp).reshape(Bp, S_, C_WIDTH)
            hp = hp + jnp.einsum('bsn,nd->bsd', u * mixed * jax.nn.silu(z), w_out1)
            us, vs, zs = c_project(rmsnorm(hs, norm1_g), w_in1, sgu_ln_g, sgu_ln_b)
            mixed_s = spatial_mix(vs[:, None], w_sp, b_sp)[:, 0]
            hs = hs + jnp.einsum('bsn,nd->bsd', us * mixed_s * jax.nn.silu(zs), w_out1)
            sgu_v_sample = vs
    y_prompt = rmsnorm(hp, final_g)
    y_sample = rmsnorm(hs, final_g)
    return (y_prompt, y_sample,
            fox_k_prompt, fox_v_prompt, fox_logf_prompt,
            fox_k_sample, fox_v_sample, fox_logf_sample,
            sb_k_prompt, sb_v_prompt,
            sb_k_sample, sb_v_sample,
            sgu_v_sample)
```

```python
import functools

import jax
import jax.numpy as jnp
from jax import lax
from jax.experimental import pallas as pl
from jax.experimental.pallas import tpu as pltpu

F32 = jnp.float32
BF16 = jnp.bfloat16

HEAD_DIM = 128
N_HEADS = 8
MIX_WIDTH = N_HEADS * HEAD_DIM
C_GROUPS = 16
C_CHUNK = 128
RMS_EPS = 1e-6
LN_EPS = 1e-5
ATTN_SCALE = HEAD_DIM ** -0.5

LANES = 128
ATTN_BLOCK = 256
V7X_VMEM_BYTES = 64 * 1024 * 1024
VMEM_LIMIT_CAP = V7X_VMEM_BYTES - 8 * 1024 * 1024


def _params(n_grid_axes, vmem_bytes):
    limit = min(int(vmem_bytes * 1.25) + (4 << 20), VMEM_LIMIT_CAP)
    return pltpu.CompilerParams(
        dimension_semantics=("arbitrary",) * n_grid_axes,
        vmem_limit_bytes=limit,
    )


def _nbytes(shape, dtype):
    n = 1
    for s in shape:
        n *= s
    return n * jnp.dtype(dtype).itemsize


def _dot(a, b):
    return jnp.dot(a, b, preferred_element_type=F32)


def _dot_nt(a, b):
    return lax.dot_general(a, b, (((1,), (1,)), ((), ())), preferred_element_type=F32)


def _log_sigmoid(x):
    return jnp.minimum(x, 0.0) - jnp.log(1.0 + jnp.exp(-jnp.abs(x)))


def _silu(x):
    return x / (1.0 + jnp.exp(-x))


def _shift_div(x, n):
    assert n & (n - 1) == 0
    return lax.shift_right_logical(x, jnp.full_like(x, n.bit_length() - 1))


def _split3(x):
    hi = x.astype(BF16)
    r1 = x - hi.astype(F32)
    mid = r1.astype(BF16)
    lo = (r1 - mid.astype(F32)).astype(BF16)
    return hi, mid, lo


def _split2(x):
    hi = x.astype(BF16)
    lo = (x - hi.astype(F32)).astype(BF16)
    return hi, lo


def _rms_kernel(x_ref, g_ref, o_ref):
    x = x_ref[...]
    ms = jnp.mean(x * x, axis=-1, keepdims=True)
    o_ref[...] = (x * lax.rsqrt(ms + RMS_EPS) * g_ref[...]).astype(o_ref.dtype)


def rmsnorm(x, g, out_dtype, tm=512):
    m, d = x.shape
    tm = min(tm, m)
    vmem = 2 * _nbytes((tm, d), F32) + 2 * _nbytes((tm, d), out_dtype)
    return pl.pallas_call(
        _rms_kernel,
        grid=(m // tm,),
        in_specs=[pl.BlockSpec((tm, d), lambda i: (i, 0)),
                  pl.BlockSpec((1, d), lambda i: (0, 0))],
        out_specs=pl.BlockSpec((tm, d), lambda i: (i, 0)),
        out_shape=jax.ShapeDtypeStruct((m, d), out_dtype),
        compiler_params=_params(1, vmem),
        name="rmsnorm",
    )(x, g.reshape(1, d))


def _proj_kernel(*refs, k_sizes, has_res):
    n_x = len(k_sizes)
    x_refs = refs[:n_x]
    w_ref = refs[n_x]
    res_ref = refs[n_x + 1] if has_res else None
    o_ref = refs[n_x + 1 + int(has_res)]
    wb_ref = refs[n_x + 2 + int(has_res)]

    @pl.when(pl.program_id(1) == 0)
    def _():
        wb_ref[...] = w_ref[...].astype(BF16)

    k0 = 0
    acc = None
    for x_ref, k in zip(x_refs, k_sizes):
        part = _dot(x_ref[...], wb_ref[k0:k0 + k, :])
        acc = part if acc is None else acc + part
        k0 += k
    if has_res:
        acc = acc + res_ref[...]
    o_ref[...] = acc.astype(o_ref.dtype)


def project(xs, w, out_dtype, *, col0=0, n_cols=None, tn=1024, tm=512, residual=None, name="proj"):
    m = xs[0].shape[0]
    k_sizes = tuple(x.shape[1] for x in xs)
    k_total = sum(k_sizes)
    assert k_total == w.shape[0]
    n_cols = w.shape[1] - col0 if n_cols is None else n_cols
    tm = min(tm, m)
    assert m % tm == 0 and n_cols % tn == 0 and col0 % tn == 0
    cb0 = col0 // tn
    in_specs = [pl.BlockSpec((tm, k), lambda j, i: (i, 0)) for k in k_sizes]
    in_specs.append(pl.BlockSpec((k_total, tn), lambda j, i: (0, cb0 + j)))
    args = list(xs) + [w]
    vmem = sum(2 * _nbytes((tm, k), BF16) for k in k_sizes)
    vmem += 2 * _nbytes((k_total, tn), F32) + _nbytes((k_total, tn), BF16)
    vmem += 2 * _nbytes((tm, tn), out_dtype) + _nbytes((tm, tn), F32)
    if residual is not None:
        in_specs.append(pl.BlockSpec((tm, tn), lambda j, i: (i, j)))
        args.append(residual)
        vmem += 2 * _nbytes((tm, tn), F32)
    return pl.pallas_call(
        functools.partial(_proj_kernel, k_sizes=k_sizes, has_res=residual is not None),
        grid=(n_cols // tn, m // tm),
        in_specs=in_specs,
        out_specs=pl.BlockSpec((tm, tn), lambda j, i: (i, j)),
        out_shape=jax.ShapeDtypeStruct((m, n_cols), out_dtype),
        scratch_shapes=[pltpu.VMEM((k_total, tn), BF16)],
        compiler_params=_params(2, vmem),
        name=name,
    )(*args)


CUM_CHUNK = 256
HEAD_ROWS = 16


def _tri_incl(n):
    r = lax.broadcasted_iota(jnp.int32, (n, n), 0)
    c = lax.broadcasted_iota(jnp.int32, (n, n), 1)
    return jnp.where(r <= c, 1.0, 0.0).astype(BF16)


def _cumsum_lanes(x, chunk):
    n = x.shape[1]
    tri = _tri_incl(chunk)
    carry = jnp.zeros((x.shape[0], 1), F32)
    outs = []
    for c in range(n // chunk):
        hi, mid, lo = _split3(x[:, c * chunk:(c + 1) * chunk])
        loc = _dot(hi, tri) + _dot(mid, tri) + _dot(lo, tri)
        outs.append(loc + carry)
        carry = carry + loc[:, chunk - 1:chunk]
    return jnp.concatenate(outs, axis=1)


def _forget_kernel(x_ref, w_ref, b_ref, logf_ref, *cum_refs):
    logit = _dot(x_ref[...], w_ref[...]) + b_ref[...]
    logf_t = _log_sigmoid(logit).T[:HEAD_ROWS, :]
    logf_ref[0] = logf_t
    if cum_refs:
        cum_refs[0][0] = _cumsum_lanes(logf_t, CUM_CHUNK)


def forget_gate(xn, wf, bf, n_seq, with_cumsum):
    m, d = xn.shape
    t = m // n_seq
    out_shape = [jax.ShapeDtypeStruct((n_seq, HEAD_ROWS, t), F32)]
    out_specs = [pl.BlockSpec((1, HEAD_ROWS, t), lambda s: (s, 0, 0))]
    if with_cumsum:
        out_shape = out_shape * 2
        out_specs = out_specs * 2
    vmem = 2 * _nbytes((t, d), BF16) + 2 * _nbytes((d, LANES), BF16) + 8 * _nbytes((t, LANES), F32)
    return pl.pallas_call(
        _forget_kernel,
        grid=(n_seq,),
        in_specs=[pl.BlockSpec((t, d), lambda s: (s, 0)),
                  pl.BlockSpec((d, LANES), lambda s: (0, 0)),
                  pl.BlockSpec((1, LANES), lambda s: (0, 0))],
        out_specs=out_specs,
        out_shape=out_shape,
        compiler_params=_params(1, vmem),
        name="forget_gate",
    )(xn, wf, bf)


def _fox_prompt_kernel(q_ref, k_ref, v_ref, f_ref, z_ref, o_ref, kb_ref, vb_ref, *, seq, blk):
    h = pl.program_id(1)
    kb_ref[...] = k_ref[0].astype(BF16)
    vb_ref[...] = v_ref[0].astype(BF16)
    row = lax.broadcasted_iota(jnp.int32, (blk, blk), 0)
    col = lax.broadcasted_iota(jnp.int32, (blk, blk), 1)
    causal = col <= row

    def q_block(qi, _):
        q0 = pl.multiple_of(qi * blk, blk)
        q = q_ref[0, pl.ds(q0, blk), :]

        def kv_step(kj, carry, diagonal):
            m, l, acc = carry
            k0 = pl.multiple_of(kj * blk, blk)
            s = _dot_nt(q, kb_ref[pl.ds(k0, blk), :]) * ATTN_SCALE
            fk = f_ref[0, pl.ds(h, 1), pl.ds(kj, 1), :].reshape(1, blk)
            s = s - fk
            if diagonal:
                s = jnp.where(causal, s, -jnp.inf)
            m_new = jnp.maximum(m, jnp.max(s, axis=-1, keepdims=True))
            alpha = jnp.exp(m - m_new)
            p = jnp.exp(s - m_new)
            l = alpha * l + jnp.sum(p, axis=-1, keepdims=True)
            acc = alpha * acc + _dot(p.astype(BF16), vb_ref[pl.ds(k0, blk), :])
            return m_new, l, acc

        init = (jnp.full((blk, 1), -jnp.inf, F32), jnp.zeros((blk, 1), F32),
                jnp.zeros((blk, HEAD_DIM), F32))
        carry = lax.fori_loop(0, qi, lambda kj, c: kv_step(kj, c, False), init)
        _, l, acc = kv_step(qi, carry, True)
        gate = _silu(z_ref[0, pl.ds(q0, blk), :].astype(F32))
        o_ref[0, pl.ds(q0, blk), :] = (acc / l * gate).astype(o_ref.dtype)
        return 0

    lax.fori_loop(0, seq // blk, q_block, 0)


def _tri_after(n):
    r = lax.broadcasted_iota(jnp.int32, (n, n), 0)
    c = lax.broadcasted_iota(jnp.int32, (n, n), 1)
    return jnp.where(r > c, 1.0, 0.0).astype(BF16)


def _sb_block(z, carry, before, tri):
    soft = jnp.log(1.0 + jnp.exp(-jnp.abs(z)))
    log_stay = jnp.minimum(-z, 0.0) - soft
    log_take = jnp.minimum(z, 0.0) - soft
    if before is not None:
        log_stay = jnp.where(before, log_stay, 0.0)
    hi, lo = _split2(log_stay)
    after = _dot(hi, tri) + _dot(lo, tri) + carry
    a = jnp.exp(log_take + after)
    if before is not None:
        a = jnp.where(before, a, 0.0)
    return a, carry + jnp.sum(log_stay, axis=-1, keepdims=True)


def _sb_prompt_kernel(q_ref, k_ref, v_ref, z_ref, o_ref, kb_ref, vb_ref, *, seq, blk):
    kb_ref[...] = k_ref[0].astype(BF16)
    vb_ref[...] = v_ref[0].astype(BF16)
    row = lax.broadcasted_iota(jnp.int32, (blk, blk), 0)
    col = lax.broadcasted_iota(jnp.int32, (blk, blk), 1)
    before = col < row
    tri = _tri_after(blk)

    def q_block(qi, _):
        q0 = pl.multiple_of(qi * blk, blk)
        q = q_ref[0, pl.ds(q0, blk), :]

        def kv_step(kj, carry, acc, mask):
            k0 = pl.multiple_of(kj * blk, blk)
            z = _dot_nt(q, kb_ref[pl.ds(k0, blk), :]) * ATTN_SCALE
            a, carry = _sb_block(z, carry, mask, tri)
            return carry, acc + _dot(a.astype(BF16), vb_ref[pl.ds(k0, blk), :])

        carry, acc = kv_step(qi, jnp.zeros((blk, 1), F32), jnp.zeros((blk, HEAD_DIM), F32), before)
        _, acc = lax.fori_loop(0, qi, lambda t, c: kv_step(qi - 1 - t, c[0], c[1], None), (carry, acc))
        gate = _silu(z_ref[0, pl.ds(q0, blk), :].astype(F32))
        o_ref[0, pl.ds(q0, blk), :] = (acc * gate).astype(o_ref.dtype)
        return 0

    lax.fori_loop(0, seq // blk, q_block, 0)


def prompt_attention(kind, q, k, v, z, f4=None):
    b, s, w = q.shape
    blk = ATTN_BLOCK
    head_spec = pl.BlockSpec((1, s, HEAD_DIM), lambda bi, hi: (bi, 0, hi))
    in_specs = [head_spec, head_spec, head_spec]
    args = [q, k, v]
    if kind == "fox":
        in_specs.append(pl.BlockSpec((1, HEAD_ROWS, s // blk, blk), lambda bi, hi: (bi, 0, 0, 0)))
        args.append(f4)
        body = functools.partial(_fox_prompt_kernel, seq=s, blk=blk)
    else:
        body = functools.partial(_sb_prompt_kernel, seq=s, blk=blk)
    in_specs.append(head_spec)
    args.append(z)
    vmem = 2 * (2 * _nbytes((s, HEAD_DIM), BF16) + 2 * _nbytes((s, HEAD_DIM), F32))
    vmem += 2 * _nbytes((s, HEAD_DIM), BF16) + 2 * _nbytes((s, HEAD_DIM), BF16)
    vmem += 2 * _nbytes((HEAD_ROWS, s), F32) + 16 * _nbytes((blk, blk), F32)
    return pl.pallas_call(
        body,
        grid=(b, w // HEAD_DIM),
        in_specs=in_specs,
        out_specs=head_spec,
        out_shape=jax.ShapeDtypeStruct((b, s, w), BF16),
        scratch_shapes=[pltpu.VMEM((s, HEAD_DIM), BF16), pltpu.VMEM((s, HEAD_DIM), BF16)],
        compiler_params=_params(2, vmem),
        name=kind + "_prompt_attention",
    )(*args)


def _sample_kernel(*refs, kind, past, t_new, keys):
    if kind == "fox":
        q_ref, kn_ref, vn_ref, kc_ref, vc_ref, lf_ref, z_ref, o_ref, ka_ref, va_ref = refs
    else:
        q_ref, kn_ref, vn_ref, kc_ref, vc_ref, z_ref, o_ref, ka_ref, va_ref, tri_ref = refs
    rows = N_HEADS * t_new
    width = N_HEADS * HEAD_DIM
    pad = keys - past - t_new

    for src_c, src_n, dst in ((kc_ref, kn_ref, ka_ref), (vc_ref, vn_ref, va_ref)):
        dst[0:past, :] = src_c[0].astype(BF16)
        dst[past:past + t_new, :] = src_n[0].astype(BF16)
        dst[past + t_new:keys, :] = jnp.zeros((pad, width), BF16)

    r_q = lax.broadcasted_iota(jnp.int32, (rows, width), 0)
    c_q = lax.broadcasted_iota(jnp.int32, (rows, width), 1)
    q_tiled = jnp.concatenate([q_ref[0]] * N_HEADS, axis=0)
    same_head = _shift_div(r_q, t_new) == _shift_div(c_q, HEAD_DIM)
    q_bd = jnp.where(same_head, q_tiled, jnp.zeros_like(q_tiled))

    s = _dot_nt(q_bd, ka_ref[...]) * ATTN_SCALE
    r_s = lax.broadcasted_iota(jnp.int32, (rows, keys), 0)
    c_s = lax.broadcasted_iota(jnp.int32, (rows, keys), 1)
    q_pos = past + (r_s & (t_new - 1))

    if kind == "fox":
        f_all = _cumsum_lanes(lf_ref[0], LANES)
        f_rows = jnp.concatenate(
            [jnp.broadcast_to(f_all[hh:hh + 1, :], (t_new, keys)) for hh in range(N_HEADS)], axis=0)
        s = jnp.where(c_s <= q_pos, s - f_rows, -jnp.inf)
        m = jnp.max(s, axis=-1, keepdims=True)
        p = jnp.exp(s - m)
        wts = p
        norm = jnp.sum(p, axis=-1, keepdims=True)
    else:
        @pl.when(pl.program_id(0) == 0)
        def _():
            tri_ref[...] = _tri_after(keys)
        wts, _ = _sb_block(s, jnp.zeros((rows, 1), F32), c_s < q_pos, tri_ref[...])
        norm = None

    o_full = _dot(wts.astype(BF16), va_ref[...])
    if norm is not None:
        o_full = o_full / norm
    o = jnp.concatenate(
        [o_full[hh * t_new:(hh + 1) * t_new, hh * HEAD_DIM:(hh + 1) * HEAD_DIM] for hh in range(N_HEADS)],
        axis=1)
    o_ref[0] = (o * _silu(z_ref[0].astype(F32))).astype(o_ref.dtype)


def sample_attention(kind, q, k_new, v_new, cache_k, cache_v, z, logf_all=None):
    b, t_new, w = q.shape
    past = cache_k.shape[1]
    keys = -(-(past + t_new) // LANES) * LANES
    new_spec = pl.BlockSpec((1, t_new, w), lambda i: (i, 0, 0))
    cache_spec = pl.BlockSpec((1, past, w), lambda i: (i, 0, 0))
    in_specs = [new_spec, new_spec, new_spec, cache_spec, cache_spec]
    args = [q, k_new, v_new, cache_k, cache_v]
    scratch = [pltpu.VMEM((keys, w), BF16), pltpu.VMEM((keys, w), BF16)]
    vmem = 4 * _nbytes((past, w), F32) + 2 * _nbytes((keys, w), BF16)
    vmem += 24 * _nbytes((N_HEADS * t_new, keys), F32)
    if kind == "fox":
        in_specs.append(pl.BlockSpec((1, HEAD_ROWS, keys), lambda i: (i, 0, 0)))
        args.append(logf_all)
    else:
        scratch.append(pltpu.VMEM((keys, keys), BF16))
        vmem += _nbytes((keys, keys), BF16) + 2 * _nbytes((keys, keys), F32)
    in_specs.append(new_spec)
    args.append(z)
    return pl.pallas_call(
        functools.partial(_sample_kernel, kind=kind, past=past, t_new=t_new, keys=keys),
        grid=(b,),
        in_specs=in_specs,
        out_specs=new_spec,
        out_shape=jax.ShapeDtypeStruct((b, t_new, w), BF16),
        scratch_shapes=scratch,
        compiler_params=_params(1, vmem),
        name=kind + "_sample_attention",
    )(*args)


def _sgu_kernel(u_ref, v_ref, z_ref, g_ref, b_ref, w_ref, bias_ref, o_ref, *vn_refs, group_dim):
    rows = v_ref.shape[0]
    v = v_ref[...].astype(F32)
    mu = jnp.mean(v, axis=-1, keepdims=True)
    vc = v - mu
    var = jnp.mean(vc * vc, axis=-1, keepdims=True)
    vn = vc * lax.rsqrt(var + LN_EPS) * g_ref[...] + b_ref[...]
    if vn_refs:
        vn_refs[0][...] = vn
    vnb = vn.astype(BF16)
    r = lax.broadcasted_iota(jnp.int32, (rows, rows), 0)
    c = lax.broadcasted_iota(jnp.int32, (rows, rows), 1)
    causal = c <= r
    for g in range(w_ref.shape[0]):
        sl = slice(g * group_dim, (g + 1) * group_dim)
        wg = jnp.where(causal, w_ref[g], 0.0).astype(BF16)
        mixed = _dot(wg, vnb[:, sl]) + bias_ref[:, g:g + 1]
        gate = _silu(z_ref[:, sl].astype(F32))
        o_ref[:, sl] = (u_ref[:, sl].astype(F32) * mixed * gate).astype(o_ref.dtype)


def spatial_gate(uvz, ln_g, ln_b, w_mix, bias_t, emit_vn):
    m, c3 = uvz.shape
    c = c3 // 3
    n_groups, rows, _ = w_mix.shape
    blk = lambda j: pl.BlockSpec((rows, c), lambda i, j=j: (i, j))
    out_shape = [jax.ShapeDtypeStruct((m, c), BF16)]
    out_specs = [pl.BlockSpec((rows, c), lambda i: (i, 0))]
    if emit_vn:
        out_shape.append(jax.ShapeDtypeStruct((m, c), F32))
        out_specs.append(pl.BlockSpec((rows, c), lambda i: (i, 0)))
    vmem = 6 * _nbytes((rows, c), uvz.dtype) + 2 * _nbytes((rows, c), BF16) + 8 * _nbytes((rows, c), F32)
    vmem += 2 * _nbytes(w_mix.shape, F32)
    return pl.pallas_call(
        functools.partial(_sgu_kernel, group_dim=c // n_groups),
        grid=(m // rows,),
        in_specs=[blk(0), blk(1), blk(2),
                  pl.BlockSpec((1, c), lambda i: (0, 0)),
                  pl.BlockSpec((1, c), lambda i: (0, 0)),
                  pl.BlockSpec((n_groups, rows, rows), lambda i: (0, 0, 0)),
                  pl.BlockSpec((rows, n_groups), lambda i: (0, 0))],
        out_specs=out_specs,
        out_shape=out_shape,
        compiler_params=_params(1, vmem),
        name="spatial_gate",
    )(uvz, uvz, uvz, ln_g.reshape(1, c), ln_b.reshape(1, c), w_mix, bias_t)


def _layer0_inputs(h, norm_g, w_in0, wf, bf, n_seq, with_cumsum):
    xn = rmsnorm(h, norm_g, BF16)
    parts = []
    for p in range(8):
        is_kv = p in (1, 2, 5, 6)
        parts.append(project([xn], w_in0, F32 if is_kv else BF16, col0=p * MIX_WIDTH, n_cols=MIX_WIDTH,
                             name="in_proj0"))
    forget = forget_gate(xn, wf, bf, n_seq, with_cumsum)
    return parts, forget


def kernel(x_prompt, x_sample, cache_fox_k, cache_fox_v, cache_fox_logf, cache_sb_k, cache_sb_v,
           norm0_g, w_in0, b_forget, w_out0, norm1_g, w_in1, sgu_ln_g, sgu_ln_b, w_sp, b_sp, w_out1, final_g):
    bp, s, d = x_prompt.shape
    bs, t_new, _ = x_sample.shape
    past = cache_fox_k.shape[1]
    hp = x_prompt.reshape(bp * s, d)
    hs = x_sample.reshape(bs * t_new, d)

    n_main = 8 * MIX_WIDTH
    wf = jnp.pad(w_in0[:, n_main:], ((0, 0), (0, LANES - N_HEADS))).astype(BF16)
    bf = jnp.pad(b_forget, (0, LANES - N_HEADS)).reshape(1, LANES).astype(F32)

    (fq, fk, fv, fz, sq, sk, sv, sz), (logf_t, cum_t) = _layer0_inputs(hp, norm0_g, w_in0, wf, bf, bp, True)
    as_seq = lambda a: a.reshape(bp, s, MIX_WIDTH)
    f4 = cum_t.reshape(bp, HEAD_ROWS, s // ATTN_BLOCK, ATTN_BLOCK)
    mix_f = prompt_attention("fox", as_seq(fq), as_seq(fk), as_seq(fv), as_seq(fz), f4)
    mix_s = prompt_attention("sb", as_seq(sq), as_seq(sk), as_seq(sv), as_seq(sz))
    hp = project([mix_f.reshape(bp * s, MIX_WIDTH), mix_s.reshape(bp * s, MIX_WIDTH)], w_out0, F32,
                 residual=hp, name="out_proj0")
    fox_logf_prompt = jnp.swapaxes(logf_t[:, :N_HEADS, :], 1, 2)

    (gq, gk, gv, gz, tq, tk, tv, tz), (logf_s,) = _layer0_inputs(hs, norm0_g, w_in0, wf, bf, 1, False)
    as_new = lambda a: a.reshape(bs, t_new, MIX_WIDTH)
    logf_new_t = jnp.swapaxes(logf_s[0, :N_HEADS, :].reshape(N_HEADS, bs, t_new), 0, 1)
    keys = -(-(past + t_new) // LANES) * LANES
    logf_all = jnp.concatenate([jnp.swapaxes(cache_fox_logf.astype(F32), 1, 2), logf_new_t], axis=2)
    logf_all = jnp.pad(logf_all, ((0, 0), (0, HEAD_ROWS - N_HEADS), (0, keys - past - t_new)))
    as_cache = lambda a: a.reshape(bs, past, MIX_WIDTH)
    mix_g = sample_attention("fox", as_new(gq), as_new(gk), as_new(gv), as_cache(cache_fox_k),
                             as_cache(cache_fox_v), as_new(gz), logf_all)
    mix_t = sample_attention("sb", as_new(tq), as_new(tk), as_new(tv), as_cache(cache_sb_k),
                             as_cache(cache_sb_v), as_new(tz))
    hs = project([mix_g.reshape(bs * t_new, MIX_WIDTH), mix_t.reshape(bs * t_new, MIX_WIDTH)], w_out0, F32,
                 residual=hs, name="out_proj0")
    fox_logf_sample = jnp.swapaxes(logf_new_t, 1, 2)

    c_width = w_in1.shape[1] // 3
    bias_t = b_sp.T.astype(F32)
    uvz_p = project([rmsnorm(hp, norm1_g, BF16)], w_in1, BF16, name="in_proj1")
    (act_p,) = spatial_gate(uvz_p, sgu_ln_g, sgu_ln_b, w_sp, bias_t, False)
    hp = project([act_p], w_out1, F32, tn=512, residual=hp, name="out_proj1")

    per_blk = C_CHUNK // t_new
    eye = jnp.eye(per_blk, dtype=F32)
    w_corner = w_sp[:, :t_new, :t_new]
    w_bd = jnp.einsum("ab,gts->gatbs", eye, w_corner).reshape(C_GROUPS, C_CHUNK, C_CHUNK)
    bias_bd = jnp.tile(b_sp[:, :t_new].T.astype(F32), (per_blk, 1))
    uvz_s = project([rmsnorm(hs, norm1_g, BF16)], w_in1, F32, name="in_proj1")
    act_s, vn_s = spatial_gate(uvz_s, sgu_ln_g, sgu_ln_b, w_bd, bias_bd, True)
    hs = project([act_s], w_out1, F32, tn=512, residual=hs, name="out_proj1")

    y_prompt = rmsnorm(hp, final_g, F32).reshape(bp, s, d)
    y_sample = rmsnorm(hs, final_g, F32).reshape(bs, t_new, d)
    heads = lambda a, n: a.reshape(n, -1, N_HEADS, HEAD_DIM)
    return (y_prompt, y_sample,
            heads(fk, bp), heads(fv, bp), fox_logf_prompt,
            heads(gk, bs), heads(gv, bs), fox_logf_sample,
            heads(sk, bp), heads(sv, bp),
            heads(tk, bs), heads(tv, bs),
            vn_s.reshape(bs, t_new, c_width))
```

```python
import functools

import jax
import jax.numpy as jnp
from jax import lax
from jax.experimental import pallas as pl
from jax.experimental.pallas import tpu as pltpu

F32 = jnp.float32
BF16 = jnp.bfloat16

HEAD_DIM = 128
N_HEADS = 8
MIX_WIDTH = N_HEADS * HEAD_DIM
C_GROUPS = 16
C_CHUNK = 128
RMS_EPS = 1e-6
LN_EPS = 1e-5
ATTN_SCALE = HEAD_DIM ** -0.5

LANES = 128
ATTN_BLOCK = 256
ATTN_SUPER = 1024
V7X_VMEM_BYTES = 64 * 1024 * 1024
VMEM_LIMIT_CAP = V7X_VMEM_BYTES - 8 * 1024 * 1024


def _params(n_grid_axes, vmem_bytes):
    limit = min(int(vmem_bytes * 1.25) + (4 << 20), VMEM_LIMIT_CAP)
    return pltpu.CompilerParams(
        dimension_semantics=("arbitrary",) * n_grid_axes,
        vmem_limit_bytes=limit,
    )


def _nbytes(shape, dtype):
    n = 1
    for s in shape:
        n *= s
    return n * jnp.dtype(dtype).itemsize


def _dot(a, b):
    return jnp.dot(a, b, preferred_element_type=F32)


def _dot_nt(a, b):
    return lax.dot_general(a, b, (((1,), (1,)), ((), ())), preferred_element_type=F32)


def _log_sigmoid(x):
    return jnp.minimum(x, 0.0) - jnp.log(1.0 + jnp.exp(-jnp.abs(x)))


def _silu(x):
    return x / (1.0 + jnp.exp(-x))


def _widen(x, n):
    return x if n == LANES else jnp.concatenate([x] * (n // LANES), axis=1)


def _shift_div(x, n):
    assert n & (n - 1) == 0
    return lax.shift_right_logical(x, jnp.full_like(x, n.bit_length() - 1))


def _split3(x):
    hi = x.astype(BF16)
    r1 = x - hi.astype(F32)
    mid = r1.astype(BF16)
    lo = (r1 - mid.astype(F32)).astype(BF16)
    return hi, mid, lo


def _split2(x):
    hi = x.astype(BF16)
    lo = (x - hi.astype(F32)).astype(BF16)
    return hi, lo


def _rms_kernel(x_ref, g_ref, o_ref):
    x = x_ref[...]
    ms = jnp.mean(x * x, axis=-1, keepdims=True)
    o_ref[...] = (x * lax.rsqrt(ms + RMS_EPS) * g_ref[...]).astype(o_ref.dtype)


def rmsnorm(x, g, out_dtype, tm=512):
    m, d = x.shape
    tm = min(tm, m)
    vmem = 2 * _nbytes((tm, d), F32) + 2 * _nbytes((tm, d), out_dtype)
    return pl.pallas_call(
        _rms_kernel,
        grid=(m // tm,),
        in_specs=[pl.BlockSpec((tm, d), lambda i: (i, 0)),
                  pl.BlockSpec((1, d), lambda i: (0, 0))],
        out_specs=pl.BlockSpec((tm, d), lambda i: (i, 0)),
        out_shape=jax.ShapeDtypeStruct((m, d), out_dtype),
        compiler_params=_params(1, vmem),
        name="rmsnorm",
    )(x, g.reshape(1, d))


def _proj_kernel(*refs, k_sizes, has_res, cast_w, heads_out):
    n_x = len(k_sizes)
    x_refs = refs[:n_x]
    w_ref = refs[n_x]
    res_ref = refs[n_x + 1] if has_res else None
    n_in = n_x + 1 + int(has_res)
    o_ref = refs[n_in]
    o4_ref = refs[n_in + 1] if heads_out else None

    if cast_w:
        wb_ref = refs[n_in + 1 + int(heads_out)]

        @pl.when(pl.program_id(1) == 0)
        def _():
            wb_ref[...] = w_ref[...].astype(BF16)
    else:
        wb_ref = w_ref

    k0 = 0
    acc = None
    for x_ref, k in zip(x_refs, k_sizes):
        part = _dot(x_ref[...], wb_ref[k0:k0 + k, :])
        acc = part if acc is None else acc + part
        k0 += k
    if has_res:
        acc = acc + res_ref[...]
    o_ref[...] = acc.astype(o_ref.dtype)
    if heads_out:
        o4_ref[...] = pltpu.einshape("m(hd)->mhd", acc, h=N_HEADS, d=HEAD_DIM)


def project(xs, w, out_dtype, *, col0=0, n_cols=None, tn=1024, tm=512, residual=None, heads_out=False,
            name="proj"):
    m = xs[0].shape[0]
    k_sizes = tuple(x.shape[1] for x in xs)
    k_total = sum(k_sizes)
    assert k_total == w.shape[0]
    n_cols = w.shape[1] - col0 if n_cols is None else n_cols
    tm = min(tm, m)
    assert m % tm == 0 and n_cols % tn == 0 and col0 % tn == 0
    cast_w = w.dtype != BF16
    cb0 = col0 // tn
    in_specs = [pl.BlockSpec((tm, k), lambda j, i: (i, 0)) for k in k_sizes]
    in_specs.append(pl.BlockSpec((k_total, tn), lambda j, i: (0, cb0 + j)))
    args = list(xs) + [w]
    vmem = sum(2 * _nbytes((tm, k), BF16) for k in k_sizes)
    vmem += 2 * _nbytes((k_total, tn), w.dtype)
    vmem += 2 * _nbytes((tm, tn), out_dtype) + _nbytes((tm, tn), F32)
    if residual is not None:
        in_specs.append(pl.BlockSpec((tm, tn), lambda j, i: (i, j)))
        args.append(residual)
        vmem += 2 * _nbytes((tm, tn), F32)
    out_specs = [pl.BlockSpec((tm, tn), lambda j, i: (i, j))]
    out_shape = [jax.ShapeDtypeStruct((m, n_cols), out_dtype)]
    if heads_out:
        assert n_cols == tn == MIX_WIDTH
        out_specs.append(pl.BlockSpec((tm, N_HEADS, HEAD_DIM), lambda j, i: (i, 0, 0)))
        out_shape.append(jax.ShapeDtypeStruct((m, N_HEADS, HEAD_DIM), F32))
        vmem += 2 * _nbytes((tm, tn), F32)
    scratch = []
    if cast_w:
        scratch.append(pltpu.VMEM((k_total, tn), BF16))
        vmem += _nbytes((k_total, tn), BF16)
    outs = pl.pallas_call(
        functools.partial(_proj_kernel, k_sizes=k_sizes, has_res=residual is not None, cast_w=cast_w,
                          heads_out=heads_out),
        grid=(n_cols // tn, m // tm),
        in_specs=in_specs,
        out_specs=out_specs,
        out_shape=out_shape,
        scratch_shapes=scratch,
        compiler_params=_params(2, vmem),
        name=name,
    )(*args)
    return outs if heads_out else outs[0]


CUM_CHUNK = 256
HEAD_ROWS = 16


def _tri_incl(n):
    r = lax.broadcasted_iota(jnp.int32, (n, n), 0)
    c = lax.broadcasted_iota(jnp.int32, (n, n), 1)
    return jnp.where(r <= c, 1.0, 0.0).astype(BF16)


def _cumsum_lanes(x, chunk):
    n = x.shape[1]
    tri = _tri_incl(chunk)
    carry = jnp.zeros((x.shape[0], 1), F32)
    outs = []
    for c in range(n // chunk):
        hi, mid, lo = _split3(x[:, c * chunk:(c + 1) * chunk])
        loc = _dot(hi, tri) + _dot(mid, tri) + _dot(lo, tri)
        outs.append(loc + carry)
        carry = carry + loc[:, chunk - 1:chunk]
    return jnp.concatenate(outs, axis=1)


def _forget_kernel(x_ref, w_ref, b_ref, logf_ref, *cum_refs):
    logit = _dot(x_ref[...], w_ref[...]) + b_ref[...]
    logf_t = _log_sigmoid(logit).T[:HEAD_ROWS, :]
    logf_ref[0] = logf_t
    if cum_refs:
        cum_refs[0][0] = _cumsum_lanes(logf_t, CUM_CHUNK)


def forget_gate(xn, wf, bf, n_seq, with_cumsum):
    m, d = xn.shape
    t = m // n_seq
    out_shape = [jax.ShapeDtypeStruct((n_seq, HEAD_ROWS, t), F32)]
    out_specs = [pl.BlockSpec((1, HEAD_ROWS, t), lambda s: (s, 0, 0))]
    if with_cumsum:
        out_shape = out_shape * 2
        out_specs = out_specs * 2
    vmem = 2 * _nbytes((t, d), BF16) + 2 * _nbytes((d, LANES), BF16) + 8 * _nbytes((t, LANES), F32)
    return pl.pallas_call(
        _forget_kernel,
        grid=(n_seq,),
        in_specs=[pl.BlockSpec((t, d), lambda s: (s, 0)),
                  pl.BlockSpec((d, LANES), lambda s: (0, 0)),
                  pl.BlockSpec((1, LANES), lambda s: (0, 0))],
        out_specs=out_specs,
        out_shape=out_shape,
        compiler_params=_params(1, vmem),
        name="forget_gate",
    )(xn, wf, bf)


def _fox_prompt_kernel(q_ref, k_ref, v_ref, f_ref, z_ref, o_ref, m_ref, l_ref, acc_ref, *, seq, blk, sup):
    h = pl.program_id(1)
    n_sub = sup // blk

    def q_super(qs, _):
        q0 = pl.multiple_of(qs * sup, sup)
        m_ref[...] = jnp.full(m_ref.shape, -jnp.inf, F32)
        l_ref[...] = jnp.zeros(l_ref.shape, F32)
        acc_ref[...] = jnp.zeros(acc_ref.shape, F32)

        def kv_step(kj, row_lo, diagonal):
            rows = sup - row_lo
            sl = slice(row_lo, sup)
            k0 = pl.multiple_of(kj * blk, blk)
            q = q_ref[0, pl.ds(pl.multiple_of(q0 + row_lo, blk), rows), :]
            s = _dot_nt(q, k_ref[0, pl.ds(k0, blk), :]) * ATTN_SCALE
            s = s - f_ref[0, pl.ds(h, 1), pl.ds(kj, 1), :].reshape(1, blk)
            if diagonal:
                r = lax.broadcasted_iota(jnp.int32, (rows, blk), 0)
                c = lax.broadcasted_iota(jnp.int32, (rows, blk), 1)
                s = jnp.where(c <= r, s, -jnp.inf)
            m_old = m_ref[sl, :]
            m_new = jnp.maximum(m_old, jnp.max(s, axis=-1, keepdims=True))
            alpha = jnp.exp(m_old - m_new)
            p = jnp.exp(s - _widen(m_new, blk))
            m_ref[sl, :] = m_new
            l_ref[sl, :] = alpha * l_ref[sl, :] + jnp.sum(p, axis=-1, keepdims=True)
            acc_ref[sl, :] = alpha * acc_ref[sl, :] + _dot(p.astype(BF16), v_ref[0, pl.ds(k0, blk), :])

        def full_step(kj, carry):
            kv_step(kj, 0, False)
            return carry

        lax.fori_loop(0, qs * n_sub, full_step, 0)
        for c in range(n_sub):
            kv_step(qs * n_sub + c, c * blk, True)
        gate = _silu(z_ref[0, pl.ds(q0, sup), :].astype(F32))
        o_ref[0, pl.ds(q0, sup), :] = (acc_ref[...] / l_ref[...] * gate).astype(o_ref.dtype)
        return 0

    lax.fori_loop(0, seq // sup, q_super, 0)


def _tri_after(n):
    r = lax.broadcasted_iota(jnp.int32, (n, n), 0)
    c = lax.broadcasted_iota(jnp.int32, (n, n), 1)
    return jnp.where(r > c, 1.0, 0.0).astype(BF16)


def _sb_block(z, carry, before, tri):
    soft = jnp.log(1.0 + jnp.exp(-jnp.abs(z)))
    log_stay = jnp.minimum(-z, 0.0) - soft
    log_take = jnp.minimum(z, 0.0) - soft
    if before is not None:
        log_stay = jnp.where(before, log_stay, 0.0)
    hi, lo = _split2(log_stay)
    after = _dot(hi, tri) + _dot(lo, tri) + _widen(carry, z.shape[1])
    a = jnp.exp(log_take + after)
    if before is not None:
        a = jnp.where(before, a, 0.0)
    return a, carry + jnp.sum(log_stay, axis=-1, keepdims=True)


def _sb_prompt_kernel(q_ref, k_ref, v_ref, z_ref, o_ref, carry_ref, acc_ref, *, seq, blk, sup):
    n_sub = sup // blk
    tri = _tri_after(blk)

    def q_super(qs, _):
        q0 = pl.multiple_of(qs * sup, sup)
        carry_ref[...] = jnp.zeros(carry_ref.shape, F32)
        acc_ref[...] = jnp.zeros(acc_ref.shape, F32)

        def kv_step(kj, row_lo, diagonal):
            rows = sup - row_lo
            sl = slice(row_lo, sup)
            k0 = pl.multiple_of(kj * blk, blk)
            q = q_ref[0, pl.ds(pl.multiple_of(q0 + row_lo, blk), rows), :]
            z = _dot_nt(q, k_ref[0, pl.ds(k0, blk), :]) * ATTN_SCALE
            before = None
            if diagonal:
                r = lax.broadcasted_iota(jnp.int32, (rows, blk), 0)
                c = lax.broadcasted_iota(jnp.int32, (rows, blk), 1)
                before = c < r
            a, carry = _sb_block(z, carry_ref[sl, :], before, tri)
            carry_ref[sl, :] = carry
            acc_ref[sl, :] = acc_ref[sl, :] + _dot(a.astype(BF16), v_ref[0, pl.ds(k0, blk), :])

        for c in reversed(range(n_sub)):
            kv_step(qs * n_sub + c, c * blk, True)

        def full_step(t, carry):
            kv_step(qs * n_sub - 1 - t, 0, False)
            return carry

        lax.fori_loop(0, qs * n_sub, full_step, 0)
        gate = _silu(z_ref[0, pl.ds(q0, sup), :].astype(F32))
        o_ref[0, pl.ds(q0, sup), :] = (acc_ref[...] * gate).astype(o_ref.dtype)
        return 0

    lax.fori_loop(0, seq // sup, q_super, 0)


def prompt_attention(kind, q, k, v, z, f4=None):
    b, s, w = q.shape
    blk = ATTN_BLOCK
    sup = min(ATTN_SUPER, s)
    head_spec = pl.BlockSpec((1, s, HEAD_DIM), lambda bi, hi: (bi, 0, hi))
    in_specs = [head_spec, head_spec, head_spec]
    args = [q, k, v]
    col_state = pltpu.VMEM((sup, LANES), F32)
    acc_state = pltpu.VMEM((sup, HEAD_DIM), F32)
    if kind == "fox":
        in_specs.append(pl.BlockSpec((1, HEAD_ROWS, s // blk, blk), lambda bi, hi: (bi, 0, 0, 0)))
        args.append(f4)
        body = functools.partial(_fox_prompt_kernel, seq=s, blk=blk, sup=sup)
        scratch = [col_state, col_state, acc_state]
    else:
        body = functools.partial(_sb_prompt_kernel, seq=s, blk=blk, sup=sup)
        scratch = [col_state, acc_state]
    in_specs.append(head_spec)
    args.append(z)
    vmem = 2 * 5 * _nbytes((s, HEAD_DIM), BF16) + 2 * _nbytes((HEAD_ROWS, s), F32)
    vmem += 3 * _nbytes((sup, LANES), F32) + 12 * _nbytes((sup, blk), F32)
    return pl.pallas_call(
        body,
        grid=(b, w // HEAD_DIM),
        in_specs=in_specs,
        out_specs=head_spec,
        out_shape=jax.ShapeDtypeStruct((b, s, w), BF16),
        scratch_shapes=scratch,
        compiler_params=_params(2, vmem),
        name=kind + "_prompt_attention",
    )(*args)


def _sample_kernel(*refs, kind, past, t_new, keys):
    if kind == "fox":
        q_ref, kn_ref, vn_ref, kc_ref, vc_ref, lf_ref, z_ref, o_ref, ka_ref, va_ref = refs
    else:
        q_ref, kn_ref, vn_ref, kc_ref, vc_ref, z_ref, o_ref, ka_ref, va_ref, tri_ref = refs
    rows = N_HEADS * t_new
    width = N_HEADS * HEAD_DIM
    pad = keys - past - t_new

    for src_c, src_n, dst in ((kc_ref, kn_ref, ka_ref), (vc_ref, vn_ref, va_ref)):
        dst[0:past, :] = pltpu.einshape("phd->p(hd)", src_c[0]).astype(BF16)
        dst[past:past + t_new, :] = src_n[0]
        dst[past + t_new:keys, :] = jnp.zeros((pad, width), BF16)

    r_q = lax.broadcasted_iota(jnp.int32, (rows, width), 0)
    c_q = lax.broadcasted_iota(jnp.int32, (rows, width), 1)
    q_tiled = jnp.concatenate([q_ref[0]] * N_HEADS, axis=0)
    same_head = _shift_div(r_q, t_new) == _shift_div(c_q, HEAD_DIM)
    q_bd = jnp.where(same_head, q_tiled, jnp.zeros_like(q_tiled))

    s = _dot_nt(q_bd, ka_ref[...]) * ATTN_SCALE
    r_s = lax.broadcasted_iota(jnp.int32, (rows, keys), 0)
    c_s = lax.broadcasted_iota(jnp.int32, (rows, keys), 1)
    q_pos = past + (r_s & (t_new - 1))

    if kind == "fox":
        f_all = _cumsum_lanes(lf_ref[0], LANES)
        f_rows = jnp.concatenate(
            [jnp.broadcast_to(f_all[hh:hh + 1, :], (t_new, keys)) for hh in range(N_HEADS)], axis=0)
        s = jnp.where(c_s <= q_pos, s - f_rows, -jnp.inf)
        m = jnp.max(s, axis=-1, keepdims=True)
        p = jnp.exp(s - m)
        wts = p
        norm = jnp.sum(p, axis=-1, keepdims=True)
    else:
        @pl.when(pl.program_id(0) == 0)
        def _():
            tri_ref[...] = _tri_after(keys)
        wts, _ = _sb_block(s, jnp.zeros((rows, LANES), F32), c_s < q_pos, tri_ref[...])
        norm = None

    o_full = _dot(wts.astype(BF16), va_ref[...])
    if norm is not None:
        o_full = o_full / norm
    o = jnp.concatenate(
        [o_full[hh * t_new:(hh + 1) * t_new, hh * HEAD_DIM:(hh + 1) * HEAD_DIM] for hh in range(N_HEADS)],
        axis=1)
    o_ref[0] = (o * _silu(z_ref[0].astype(F32))).astype(o_ref.dtype)


def sample_attention(kind, q, k_new, v_new, cache_k, cache_v, z, logf_all=None):
    b, t_new, w = q.shape
    past = cache_k.shape[1]
    assert t_new & (t_new - 1) == 0
    keys = -(-(past + t_new) // LANES) * LANES
    new_spec = pl.BlockSpec((1, t_new, w), lambda i: (i, 0, 0))
    cache_spec = pl.BlockSpec((1, past, N_HEADS, HEAD_DIM), lambda i: (i, 0, 0, 0))
    in_specs = [new_spec, new_spec, new_spec, cache_spec, cache_spec]
    args = [q, k_new, v_new, cache_k, cache_v]
    scratch = [pltpu.VMEM((keys, w), BF16), pltpu.VMEM((keys, w), BF16)]
    vmem = 4 * _nbytes((past, w), F32) + 2 * _nbytes((keys, w), BF16)
    vmem += 24 * _nbytes((N_HEADS * t_new, keys), F32)
    if kind == "fox":
        in_specs.append(pl.BlockSpec((1, HEAD_ROWS, keys), lambda i: (i, 0, 0)))
        args.append(logf_all)
    else:
        scratch.append(pltpu.VMEM((keys, keys), BF16))
        vmem += _nbytes((keys, keys), BF16) + 2 * _nbytes((keys, keys), F32)
    in_specs.append(new_spec)
    args.append(z)
    return pl.pallas_call(
        functools.partial(_sample_kernel, kind=kind, past=past, t_new=t_new, keys=keys),
        grid=(b,),
        in_specs=in_specs,
        out_specs=new_spec,
        out_shape=jax.ShapeDtypeStruct((b, t_new, w), BF16),
        scratch_shapes=scratch,
        compiler_params=_params(1, vmem),
        name=kind + "_sample_attention",
    )(*args)


def _sgu_kernel(u_ref, v_ref, z_ref, g_ref, b_ref, w_ref, bias_ref, o_ref, *vn_refs, group_dim):
    rows = v_ref.shape[0]
    v = v_ref[...].astype(F32)
    mu = jnp.mean(v, axis=-1, keepdims=True)
    vc = v - mu
    var = jnp.mean(vc * vc, axis=-1, keepdims=True)
    vn = vc * lax.rsqrt(var + LN_EPS) * g_ref[...] + b_ref[...]
    if vn_refs:
        vn_refs[0][...] = vn
    vnb = vn.astype(BF16)
    r = lax.broadcasted_iota(jnp.int32, (rows, rows), 0)
    c = lax.broadcasted_iota(jnp.int32, (rows, rows), 1)
    causal = c <= r
    for g in range(w_ref.shape[0]):
        sl = slice(g * group_dim, (g + 1) * group_dim)
        wg = jnp.where(causal, w_ref[g], 0.0).astype(BF16)
        mixed = _dot(wg, vnb[:, sl]) + bias_ref[:, g:g + 1]
        gate = _silu(z_ref[:, sl].astype(F32))
        o_ref[:, sl] = (u_ref[:, sl].astype(F32) * mixed * gate).astype(o_ref.dtype)


def spatial_gate(uvz, ln_g, ln_b, w_mix, bias_t, emit_vn):
    m, c3 = uvz.shape
    c = c3 // 3
    n_groups, rows, _ = w_mix.shape
    blk = lambda j: pl.BlockSpec((rows, c), lambda i, j=j: (i, j))
    out_shape = [jax.ShapeDtypeStruct((m, c), BF16)]
    out_specs = [pl.BlockSpec((rows, c), lambda i: (i, 0))]
    if emit_vn:
        out_shape.append(jax.ShapeDtypeStruct((m, c), F32))
        out_specs.append(pl.BlockSpec((rows, c), lambda i: (i, 0)))
    vmem = 6 * _nbytes((rows, c), uvz.dtype) + 2 * _nbytes((rows, c), BF16) + 8 * _nbytes((rows, c), F32)
    vmem += 2 * _nbytes(w_mix.shape, F32)
    return pl.pallas_call(
        functools.partial(_sgu_kernel, group_dim=c // n_groups),
        grid=(m // rows,),
        in_specs=[blk(0), blk(1), blk(2),
                  pl.BlockSpec((1, c), lambda i: (0, 0)),
                  pl.BlockSpec((1, c), lambda i: (0, 0)),
                  pl.BlockSpec((n_groups, rows, rows), lambda i: (0, 0, 0)),
                  pl.BlockSpec((rows, n_groups), lambda i: (0, 0))],
        out_specs=out_specs,
        out_shape=out_shape,
        compiler_params=_params(1, vmem),
        name="spatial_gate",
    )(uvz, uvz, uvz, ln_g.reshape(1, c), ln_b.reshape(1, c), w_mix, bias_t)


def _layer0_inputs(h, norm_g, w_main, wf, bf, n_seq, with_cumsum):
    xn = rmsnorm(h, norm_g, BF16)
    parts, kv_out = [], []
    for p in range(8):
        is_kv = p in (1, 2, 5, 6)
        out = project([xn], w_main, BF16, col0=p * MIX_WIDTH, n_cols=MIX_WIDTH, heads_out=is_kv,
                      name="in_proj0")
        if is_kv:
            parts.append(out[0])
            kv_out.append(out[1])
        else:
            parts.append(out)
    forget = forget_gate(xn, wf, bf, n_seq, with_cumsum)
    return parts, kv_out, forget


def kernel(x_prompt, x_sample, cache_fox_k, cache_fox_v, cache_fox_logf, cache_sb_k, cache_sb_v,
           norm0_g, w_in0, b_forget, w_out0, norm1_g, w_in1, sgu_ln_g, sgu_ln_b, w_sp, b_sp, w_out1, final_g):
    bp, s, d = x_prompt.shape
    bs, t_new, _ = x_sample.shape
    past = cache_fox_k.shape[1]
    hp = x_prompt.reshape(bp * s, d)
    hs = x_sample.reshape(bs * t_new, d)

    n_main = 8 * MIX_WIDTH
    w_main = w_in0[:, :n_main].astype(BF16)
    wf = jnp.pad(w_in0[:, n_main:], ((0, 0), (0, LANES - N_HEADS))).astype(BF16)
    bf = jnp.pad(b_forget, (0, LANES - N_HEADS)).reshape(1, LANES).astype(F32)

    (fq, fk, fv, fz, sq, sk, sv, sz), kv_prompt, (logf_t, cum_t) = _layer0_inputs(
        hp, norm0_g, w_main, wf, bf, bp, True)
    as_seq = lambda a: a.reshape(bp, s, MIX_WIDTH)
    f4 = cum_t.reshape(bp, HEAD_ROWS, s // ATTN_BLOCK, ATTN_BLOCK)
    mix_f = prompt_attention("fox", as_seq(fq), as_seq(fk), as_seq(fv), as_seq(fz), f4)
    mix_s = prompt_attention("sb", as_seq(sq), as_seq(sk), as_seq(sv), as_seq(sz))
    hp = project([mix_f.reshape(bp * s, MIX_WIDTH), mix_s.reshape(bp * s, MIX_WIDTH)], w_out0, F32,
                 residual=hp, name="out_proj0")
    fox_logf_prompt = jnp.swapaxes(logf_t[:, :N_HEADS, :], 1, 2)

    (gq, gk, gv, gz, tq, tk, tv, tz), kv_sample, (logf_s,) = _layer0_inputs(
        hs, norm0_g, w_main, wf, bf, 1, False)
    as_new = lambda a: a.reshape(bs, t_new, MIX_WIDTH)
    logf_new_t = jnp.swapaxes(logf_s[0, :N_HEADS, :].reshape(N_HEADS, bs, t_new), 0, 1)
    keys = -(-(past + t_new) // LANES) * LANES
    logf_all = jnp.concatenate([jnp.swapaxes(cache_fox_logf.astype(F32), 1, 2), logf_new_t], axis=2)
    logf_all = jnp.pad(logf_all, ((0, 0), (0, HEAD_ROWS - N_HEADS), (0, keys - past - t_new)))
    mix_g = sample_attention("fox", as_new(gq), as_new(gk), as_new(gv), cache_fox_k, cache_fox_v,
                             as_new(gz), logf_all)
    mix_t = sample_attention("sb", as_new(tq), as_new(tk), as_new(tv), cache_sb_k, cache_sb_v, as_new(tz))
    hs = project([mix_g.reshape(bs * t_new, MIX_WIDTH), mix_t.reshape(bs * t_new, MIX_WIDTH)], w_out0, F32,
                 residual=hs, name="out_proj0")
    fox_logf_sample = jnp.swapaxes(logf_new_t, 1, 2)

    c_width = w_in1.shape[1] // 3
    bias_t = b_sp.T.astype(F32)
    uvz_p = project([rmsnorm(hp, norm1_g, BF16)], w_in1, BF16, name="in_proj1")
    (act_p,) = spatial_gate(uvz_p, sgu_ln_g, sgu_ln_b, w_sp, bias_t, False)
    hp = project([act_p], w_out1, F32, tn=512, residual=hp, name="out_proj1")

    per_blk = C_CHUNK // t_new
    eye = jnp.eye(per_blk, dtype=F32)
    w_corner = w_sp[:, :t_new, :t_new]
    w_bd = jnp.einsum("ab,gts->gatbs", eye, w_corner).reshape(C_GROUPS, C_CHUNK, C_CHUNK)
    bias_bd = jnp.tile(b_sp[:, :t_new].T.astype(F32), (per_blk, 1))
    uvz_s = project([rmsnorm(hs, norm1_g, BF16)], w_in1, F32, name="in_proj1")
    act_s, vn_s = spatial_gate(uvz_s, sgu_ln_g, sgu_ln_b, w_bd, bias_bd, True)
    hs = project([act_s], w_out1, F32, tn=512, residual=hs, name="out_proj1")

    y_prompt = rmsnorm(hp, final_g, F32).reshape(bp, s, d)
    y_sample = rmsnorm(hs, final_g, F32).reshape(bs, t_new, d)
    fk4, fv4, sk4, sv4 = (a.reshape(bp, s, N_HEADS, HEAD_DIM) for a in kv_prompt)
    gk4, gv4, tk4, tv4 = (a.reshape(bs, t_new, N_HEADS, HEAD_DIM) for a in kv_sample)
    return (y_prompt, y_sample,
            fk4, fv4, fox_logf_prompt,
            gk4, gv4, fox_logf_sample,
            sk4, sv4,
            tk4, tv4,
            vn_s.reshape(bs, t_new, c_width))
```

```python
import functools

import jax
import jax.numpy as jnp
from jax import lax
from jax.experimental import pallas as pl
from jax.experimental.pallas import tpu as pltpu

F32 = jnp.float32
BF16 = jnp.bfloat16

HEAD_DIM = 128
N_HEADS = 8
MIX_WIDTH = N_HEADS * HEAD_DIM
C_GROUPS = 16
C_CHUNK = 128
RMS_EPS = 1e-6
LN_EPS = 1e-5
ATTN_SCALE = HEAD_DIM ** -0.5
LOG2E = 1.4426950408889634

LANES = 128
ATTN_BLOCK = 256
ATTN_SUPER = 1024
V7X_VMEM_BYTES = 64 * 1024 * 1024
VMEM_LIMIT_CAP = V7X_VMEM_BYTES - 8 * 1024 * 1024


def _params(n_grid_axes, vmem_bytes):
    limit = min(int(vmem_bytes * 1.25) + (4 << 20), VMEM_LIMIT_CAP)
    return pltpu.CompilerParams(
        dimension_semantics=("arbitrary",) * n_grid_axes,
        vmem_limit_bytes=limit,
    )


def _nbytes(shape, dtype):
    n = 1
    for s in shape:
        n *= s
    return n * jnp.dtype(dtype).itemsize


def _dot(a, b):
    return jnp.dot(a, b, preferred_element_type=F32)


def _dot_nt(a, b):
    return lax.dot_general(a, b, (((1,), (1,)), ((), ())), preferred_element_type=F32)


def _log_sigmoid(x):
    return jnp.minimum(x, 0.0) - jnp.log(1.0 + jnp.exp(-jnp.abs(x)))


def _silu(x):
    return x / (1.0 + jnp.exp(-x))


def _widen(x, n):
    return x if n == LANES else jnp.concatenate([x] * (n // LANES), axis=1)


def _shift_div(x, n):
    assert n & (n - 1) == 0
    return lax.shift_right_logical(x, jnp.full_like(x, n.bit_length() - 1))


def _split3(x):
    hi = x.astype(BF16)
    r1 = x - hi.astype(F32)
    mid = r1.astype(BF16)
    lo = (r1 - mid.astype(F32)).astype(BF16)
    return hi, mid, lo


def _split2(x):
    hi = x.astype(BF16)
    lo = (x - hi.astype(F32)).astype(BF16)
    return hi, lo


def _rms_kernel(x_ref, g_ref, o_ref):
    x = x_ref[...]
    ms = jnp.mean(x * x, axis=-1, keepdims=True)
    o_ref[...] = (x * lax.rsqrt(ms + RMS_EPS) * g_ref[...]).astype(o_ref.dtype)


def rmsnorm(x, g, out_dtype, tm=512):
    m, d = x.shape
    tm = min(tm, m)
    vmem = 2 * _nbytes((tm, d), F32) + 2 * _nbytes((tm, d), out_dtype)
    return pl.pallas_call(
        _rms_kernel,
        grid=(m // tm,),
        in_specs=[pl.BlockSpec((tm, d), lambda i: (i, 0)),
                  pl.BlockSpec((1, d), lambda i: (0, 0))],
        out_specs=pl.BlockSpec((tm, d), lambda i: (i, 0)),
        out_shape=jax.ShapeDtypeStruct((m, d), out_dtype),
        compiler_params=_params(1, vmem),
        name="rmsnorm",
    )(x, g.reshape(1, d))


def _proj_kernel(*refs, k_sizes, has_res, w_transposed, heads_out):
    n_x = len(k_sizes)
    x_refs = refs[:n_x]
    w_ref = refs[n_x]
    res_ref = refs[n_x + 1] if has_res else None
    n_in = n_x + 1 + int(has_res)
    o_ref = refs[n_in]
    o4_ref = refs[n_in + 1] if heads_out else None
    wb_ref = refs[n_in + 1 + int(heads_out)]

    @pl.when(pl.program_id(1) == 0)
    def _():
        wb_ref[...] = w_ref[...].astype(BF16)

    k0 = 0
    acc = None
    for x_ref, k in zip(x_refs, k_sizes):
        if w_transposed:
            part = _dot_nt(x_ref[...], wb_ref[:, k0:k0 + k])
        else:
            part = _dot(x_ref[...], wb_ref[k0:k0 + k, :])
        acc = part if acc is None else acc + part
        k0 += k
    if has_res:
        acc = acc + res_ref[...]
    o_ref[...] = acc.astype(o_ref.dtype)
    if heads_out:
        o4_ref[...] = acc.reshape(acc.shape[0], N_HEADS, HEAD_DIM)


def project(xs, w, out_dtype, *, col0=0, n_cols=None, tn=1024, tm=1024, residual=None, heads_out=False,
            w_transposed=False, name="proj"):
    m = xs[0].shape[0]
    k_sizes = tuple(x.shape[1] for x in xs)
    k_total = sum(k_sizes)
    k_axis, n_axis = (1, 0) if w_transposed else (0, 1)
    assert k_total == w.shape[k_axis]
    n_cols = w.shape[n_axis] - col0 if n_cols is None else n_cols
    tm = min(tm, m)
    assert m % tm == 0 and n_cols % tn == 0 and col0 % tn == 0
    cb0 = col0 // tn
    in_specs = [pl.BlockSpec((tm, k), lambda j, i: (i, 0)) for k in k_sizes]
    if w_transposed:
        w_block = (tn, k_total)
        in_specs.append(pl.BlockSpec(w_block, lambda j, i: (cb0 + j, 0)))
    else:
        w_block = (k_total, tn)
        in_specs.append(pl.BlockSpec(w_block, lambda j, i: (0, cb0 + j)))
    args = list(xs) + [w]
    vmem = sum(2 * _nbytes((tm, k), BF16) for k in k_sizes)
    vmem += 2 * _nbytes(w_block, w.dtype) + _nbytes(w_block, BF16)
    vmem += 2 * _nbytes((tm, tn), out_dtype) + _nbytes((tm, tn), F32)
    if residual is not None:
        in_specs.append(pl.BlockSpec((tm, tn), lambda j, i: (i, j)))
        args.append(residual)
        vmem += 2 * _nbytes((tm, tn), F32)
    out_specs = [pl.BlockSpec((tm, tn), lambda j, i: (i, j))]
    out_shape = [jax.ShapeDtypeStruct((m, n_cols), out_dtype)]
    if heads_out:
        assert n_cols == tn == MIX_WIDTH
        out_specs.append(pl.BlockSpec((tm, N_HEADS, HEAD_DIM), lambda j, i: (i, 0, 0)))
        out_shape.append(jax.ShapeDtypeStruct((m, N_HEADS, HEAD_DIM), F32))
        vmem += 2 * _nbytes((tm, tn), F32)
    outs = pl.pallas_call(
        functools.partial(_proj_kernel, k_sizes=k_sizes, has_res=residual is not None,
                          w_transposed=w_transposed, heads_out=heads_out),
        grid=(n_cols // tn, m // tm),
        in_specs=in_specs,
        out_specs=out_specs,
        out_shape=out_shape,
        scratch_shapes=[pltpu.VMEM(w_block, BF16)],
        compiler_params=_params(2, vmem),
        name=name,
    )(*args)
    return outs if heads_out else outs[0]


CUM_CHUNK = 256
HEAD_ROWS = 16


def _tri_incl(n):
    r = lax.broadcasted_iota(jnp.int32, (n, n), 0)
    c = lax.broadcasted_iota(jnp.int32, (n, n), 1)
    return jnp.where(r <= c, 1.0, 0.0).astype(BF16)


def _cumsum_lanes(x, chunk):
    n = x.shape[1]
    tri = _tri_incl(chunk)
    carry = jnp.zeros((x.shape[0], 1), F32)
    outs = []
    for c in range(n // chunk):
        hi, mid, lo = _split3(x[:, c * chunk:(c + 1) * chunk])
        loc = _dot(hi, tri) + _dot(mid, tri) + _dot(lo, tri)
        outs.append(loc + carry)
        carry = carry + loc[:, chunk - 1:chunk]
    return jnp.concatenate(outs, axis=1)


def _forget_kernel(x_ref, w_ref, b_ref, logf_ref, *cum_refs):
    logit = _dot(x_ref[...], w_ref[...]) + b_ref[...]
    logf_t = _log_sigmoid(logit).T[:HEAD_ROWS, :]
    logf_ref[0] = logf_t
    if cum_refs:
        cum_refs[0][0] = _cumsum_lanes(logf_t, CUM_CHUNK)


def forget_gate(xn, wf, bf, n_seq, with_cumsum):
    m, d = xn.shape
    t = m // n_seq
    out_shape = [jax.ShapeDtypeStruct((n_seq, HEAD_ROWS, t), F32)]
    out_specs = [pl.BlockSpec((1, HEAD_ROWS, t), lambda s: (s, 0, 0))]
    if with_cumsum:
        out_shape = out_shape * 2
        out_specs = out_specs * 2
    vmem = 2 * _nbytes((t, d), BF16) + 2 * _nbytes((d, LANES), BF16) + 8 * _nbytes((t, LANES), F32)
    return pl.pallas_call(
        _forget_kernel,
        grid=(n_seq,),
        in_specs=[pl.BlockSpec((t, d), lambda s: (s, 0)),
                  pl.BlockSpec((d, LANES), lambda s: (0, 0)),
                  pl.BlockSpec((1, LANES), lambda s: (0, 0))],
        out_specs=out_specs,
        out_shape=out_shape,
        compiler_params=_params(1, vmem),
        name="forget_gate",
    )(xn, wf, bf)


def _pipelined(steps, stages):
    vals = [None] * len(steps)
    for t in range(len(steps) + len(stages) - 1):
        for j, stage in enumerate(stages):
            i = t - j
            if 0 <= i < len(steps):
                vals[i] = stage(steps[i]) if j == 0 else stage(steps[i], vals[i])


def _fox_prompt_kernel(q_ref, k_ref, v_ref, f_ref, z_ref, o_ref, m_ref, l_ref, acc_ref, *, seq, blk, sup):
    h = pl.program_id(1)
    n_sub = sup // blk

    for qs in range(seq // sup):
        q0 = qs * sup
        m_ref[...] = jnp.full(m_ref.shape, -jnp.inf, F32)
        l_ref[...] = jnp.zeros(l_ref.shape, F32)
        acc_ref[...] = jnp.zeros(acc_ref.shape, F32)

        def logits(step):
            kj, row_lo, diagonal = step
            q = q_ref[0, q0 + row_lo:q0 + sup, :]
            s = _dot_nt(q, k_ref[0, kj * blk:(kj + 1) * blk, :]) * (ATTN_SCALE * LOG2E)
            s = s - f_ref[0, pl.ds(h, 1), kj:kj + 1, :].reshape(1, blk) * LOG2E
            if diagonal:
                r = lax.broadcasted_iota(jnp.int32, s.shape, 0)
                c = lax.broadcasted_iota(jnp.int32, s.shape, 1)
                s = jnp.where(c <= r, s, -jnp.inf)
            return s

        def weights(step, s):
            sl = slice(step[1], sup)
            m_old = m_ref[sl, :]
            m_new = jnp.maximum(m_old, jnp.max(s, axis=-1, keepdims=True))
            alpha = jnp.exp2(m_old - m_new)
            p = jnp.exp2(s - _widen(m_new, blk))
            m_ref[sl, :] = m_new
            l_ref[sl, :] = alpha * l_ref[sl, :] + jnp.sum(p, axis=-1, keepdims=True)
            return alpha, p.astype(BF16)

        def accumulate(step, alpha_p):
            kj, row_lo, _ = step
            sl = slice(row_lo, sup)
            alpha, p = alpha_p
            acc_ref[sl, :] = alpha * acc_ref[sl, :] + _dot(p, v_ref[0, kj * blk:(kj + 1) * blk, :])

        steps = [(kj, 0, False) for kj in range(qs * n_sub)]
        steps += [(qs * n_sub + c, c * blk, True) for c in range(n_sub)]
        for step in steps:
            accumulate(step, weights(step, logits(step)))
        gate = _silu(z_ref[0, q0:q0 + sup, :].astype(F32))
        o_ref[0, q0:q0 + sup, :] = (acc_ref[...] / l_ref[...] * gate).astype(o_ref.dtype)


def _tri_after(n):
    r = lax.broadcasted_iota(jnp.int32, (n, n), 0)
    c = lax.broadcasted_iota(jnp.int32, (n, n), 1)
    return jnp.where(r > c, 1.0, 0.0).astype(BF16)


def _sb_block(w, carry, before, tri):
    log_stay, log_take = _sb_logs(w, before)
    a = _sb_weights(log_take, _sb_suffix(log_stay, tri), carry, before)
    return a, carry + jnp.sum(log_stay, axis=-1, keepdims=True)


def _sb_logs(w, before):
    nw = -w
    soft = jnp.log(1.0 + jnp.exp2(jnp.minimum(w, nw))) * LOG2E
    log_stay = jnp.minimum(nw, 0.0) - soft
    log_take = log_stay + w
    if before is not None:
        log_stay = jnp.where(before, log_stay, 0.0)
    return log_stay, log_take


def _sb_suffix(log_stay, tri):
    hi, lo = _split2(log_stay)
    return _dot(hi, tri) + _dot(lo, tri)


def _sb_weights(log_take, suffix, carry, before):
    a = jnp.exp2(log_take + (suffix + _widen(carry, suffix.shape[1])))
    if before is not None:
        a = jnp.where(before, a, 0.0)
    return a


def _sb_prompt_kernel(q_ref, k_ref, v_ref, z_ref, o_ref, carry_ref, acc_ref, *, seq, blk, sup):
    n_sub = sup // blk
    tri = _tri_after(blk)

    for qs in range(seq // sup):
        q0 = qs * sup
        carry_ref[...] = jnp.zeros(carry_ref.shape, F32)
        acc_ref[...] = jnp.zeros(acc_ref.shape, F32)

        def mask(step):
            if not step[2]:
                return None
            r = lax.broadcasted_iota(jnp.int32, (sup - step[1], blk), 0)
            c = lax.broadcasted_iota(jnp.int32, (sup - step[1], blk), 1)
            return c < r

        def logits(step):
            kj, row_lo, _ = step
            q = q_ref[0, q0 + row_lo:q0 + sup, :]
            return _dot_nt(q, k_ref[0, kj * blk:(kj + 1) * blk, :]) * (ATTN_SCALE * LOG2E)

        def suffix_sums(step, w):
            sl = slice(step[1], sup)
            log_stay, log_take = _sb_logs(w, mask(step))
            carry = carry_ref[sl, :]
            carry_ref[sl, :] = carry + jnp.sum(log_stay, axis=-1, keepdims=True)
            return log_take, _sb_suffix(log_stay, tri), carry

        def accumulate(step, parts):
            kj, row_lo, _ = step
            sl = slice(row_lo, sup)
            a = _sb_weights(*parts, mask(step)).astype(BF16)
            acc_ref[sl, :] = acc_ref[sl, :] + _dot(a, v_ref[0, kj * blk:(kj + 1) * blk, :])

        steps = [(qs * n_sub + c, c * blk, True) for c in reversed(range(n_sub))]
        steps += [(kj, 0, False) for kj in reversed(range(qs * n_sub))]
        _pipelined(steps, [logits, suffix_sums, accumulate])
        gate = _silu(z_ref[0, q0:q0 + sup, :].astype(F32))
        o_ref[0, q0:q0 + sup, :] = (acc_ref[...] * gate).astype(o_ref.dtype)


def prompt_attention(kind, q, k, v, z, f4=None):
    b, s, w = q.shape
    blk = ATTN_BLOCK
    sup = min(ATTN_SUPER, s)
    head_spec = pl.BlockSpec((1, s, HEAD_DIM), lambda bi, hi: (bi, 0, hi))
    in_specs = [head_spec, head_spec, head_spec]
    args = [q, k, v]
    col_state = pltpu.VMEM((sup, LANES), F32)
    acc_state = pltpu.VMEM((sup, HEAD_DIM), F32)
    if kind == "fox":
        in_specs.append(pl.BlockSpec((1, HEAD_ROWS, s // blk, blk), lambda bi, hi: (bi, 0, 0, 0)))
        args.append(f4)
        body = functools.partial(_fox_prompt_kernel, seq=s, blk=blk, sup=sup)
        scratch = [col_state, col_state, acc_state]
    else:
        body = functools.partial(_sb_prompt_kernel, seq=s, blk=blk, sup=sup)
        scratch = [col_state, acc_state]
    in_specs.append(head_spec)
    args.append(z)
    vmem = 2 * 5 * _nbytes((s, HEAD_DIM), BF16) + 2 * _nbytes((HEAD_ROWS, s), F32)
    vmem += 3 * _nbytes((sup, LANES), F32) + 12 * _nbytes((sup, blk), F32)
    return pl.pallas_call(
        body,
        grid=(b, w // HEAD_DIM),
        in_specs=in_specs,
        out_specs=head_spec,
        out_shape=jax.ShapeDtypeStruct((b, s, w), BF16),
        scratch_shapes=scratch,
        compiler_params=_params(2, vmem),
        name=kind + "_prompt_attention",
    )(*args)


def _sample_kernel(*refs, kind, past, t_new, keys):
    if kind == "fox":
        q_ref, kn_ref, vn_ref, kc_ref, vc_ref, lf_ref, z_ref, o_ref, ka_ref, va_ref = refs
    else:
        q_ref, kn_ref, vn_ref, kc_ref, vc_ref, z_ref, o_ref, ka_ref, va_ref = refs
    rows = N_HEADS * t_new
    width = N_HEADS * HEAD_DIM
    pad = keys - past - t_new

    for src_c, src_n, dst in ((kc_ref, kn_ref, ka_ref), (vc_ref, vn_ref, va_ref)):
        dst[0:past, :] = src_c[0].reshape(past, width).astype(BF16)
        dst[past:past + t_new, :] = src_n[0]
        dst[past + t_new:keys, :] = jnp.zeros((pad, width), BF16)

    r_q = lax.broadcasted_iota(jnp.int32, (rows, width), 0)
    c_q = lax.broadcasted_iota(jnp.int32, (rows, width), 1)
    q_tiled = jnp.concatenate([q_ref[0]] * N_HEADS, axis=0)
    same_head = _shift_div(r_q, t_new) == _shift_div(c_q, HEAD_DIM)
    q_bd = jnp.where(same_head, q_tiled, jnp.zeros_like(q_tiled))

    qk = _dot_nt(q_bd, ka_ref[...])
    r_s = lax.broadcasted_iota(jnp.int32, (rows, keys), 0)
    c_s = lax.broadcasted_iota(jnp.int32, (rows, keys), 1)
    q_pos = past + (r_s & (t_new - 1))

    if kind == "fox":
        f_all = _cumsum_lanes(lf_ref[0], LANES)
        f_rows = jnp.concatenate(
            [jnp.broadcast_to(f_all[hh:hh + 1, :], (t_new, keys)) for hh in range(N_HEADS)], axis=0)
        s = jnp.where(c_s <= q_pos, qk * ATTN_SCALE - f_rows, -jnp.inf)
        m = jnp.max(s, axis=-1, keepdims=True)
        p = jnp.exp(s - m)
        wts = p
        norm = jnp.sum(p, axis=-1, keepdims=True)
    else:
        w = qk * (ATTN_SCALE * LOG2E)
        before = c_s < q_pos
        tri = _tri_after(LANES)
        carry = jnp.zeros((rows, LANES), F32)
        chunks = [None] * (keys // LANES)
        for c in reversed(range(keys // LANES)):
            sl = slice(c * LANES, (c + 1) * LANES)
            chunks[c], carry = _sb_block(w[:, sl], carry, before[:, sl], tri)
        wts = jnp.concatenate(chunks, axis=1)
        norm = None

    o_full = _dot(wts.astype(BF16), va_ref[...])
    if norm is not None:
        o_full = o_full / norm
    o = jnp.concatenate(
        [o_full[hh * t_new:(hh + 1) * t_new, hh * HEAD_DIM:(hh + 1) * HEAD_DIM] for hh in range(N_HEADS)],
        axis=1)
    o_ref[0] = (o * _silu(z_ref[0].astype(F32))).astype(o_ref.dtype)


def sample_attention(kind, q, k_new, v_new, cache_k, cache_v, z, logf_all=None):
    b, t_new, w = q.shape
    past = cache_k.shape[1]
    assert t_new & (t_new - 1) == 0
    keys = -(-(past + t_new) // LANES) * LANES
    new_spec = pl.BlockSpec((1, t_new, w), lambda i: (i, 0, 0))
    cache_spec = pl.BlockSpec((1, past, N_HEADS, HEAD_DIM), lambda i: (i, 0, 0, 0))
    in_specs = [new_spec, new_spec, new_spec, cache_spec, cache_spec]
    args = [q, k_new, v_new, cache_k, cache_v]
    scratch = [pltpu.VMEM((keys, w), BF16), pltpu.VMEM((keys, w), BF16)]
    vmem = 4 * _nbytes((past, w), F32) + 2 * _nbytes((keys, w), BF16)
    vmem += 24 * _nbytes((N_HEADS * t_new, keys), F32)
    if kind == "fox":
        in_specs.append(pl.BlockSpec((1, HEAD_ROWS, keys), lambda i: (i, 0, 0)))
        args.append(logf_all)
    in_specs.append(new_spec)
    args.append(z)
    return pl.pallas_call(
        functools.partial(_sample_kernel, kind=kind, past=past, t_new=t_new, keys=keys),
        grid=(b,),
        in_specs=in_specs,
        out_specs=new_spec,
        out_shape=jax.ShapeDtypeStruct((b, t_new, w), BF16),
        scratch_shapes=scratch,
        compiler_params=_params(1, vmem),
        name=kind + "_sample_attention",
    )(*args)


def _sgu_kernel(u_ref, v_ref, z_ref, g_ref, b_ref, w_ref, bias_ref, o_ref, *vn_refs, group_dim):
    n_groups, chunk, _ = w_ref.shape
    r = lax.broadcasted_iota(jnp.int32, (chunk, chunk), 0)
    c = lax.broadcasted_iota(jnp.int32, (chunk, chunk), 1)
    causal = c <= r
    w_causal = [jnp.where(causal, w_ref[g], 0.0).astype(BF16) for g in range(n_groups)]
    for s in range(v_ref.shape[0] // chunk):
        rs = slice(s * chunk, (s + 1) * chunk)
        v = v_ref[rs, :].astype(F32)
        mu = jnp.mean(v, axis=-1, keepdims=True)
        vc = v - mu
        var = jnp.mean(vc * vc, axis=-1, keepdims=True)
        vn = vc * lax.rsqrt(var + LN_EPS) * g_ref[...] + b_ref[...]
        if vn_refs:
            vn_refs[0][rs, :] = vn
        vnb = vn.astype(BF16)
        for g in range(n_groups):
            sl = slice(g * group_dim, (g + 1) * group_dim)
            mixed = _dot(w_causal[g], vnb[:, sl]) + bias_ref[:, g:g + 1]
            gate = _silu(z_ref[rs, sl].astype(F32))
            o_ref[rs, sl] = (u_ref[rs, sl].astype(F32) * mixed * gate).astype(o_ref.dtype)


SGU_CHUNKS_PER_STEP = 2


def spatial_gate(uvz, ln_g, ln_b, w_mix, bias_t, emit_vn):
    m, c3 = uvz.shape
    c = c3 // 3
    n_groups, chunk, _ = w_mix.shape
    rows = SGU_CHUNKS_PER_STEP * chunk
    assert m % rows == 0
    blk = lambda j: pl.BlockSpec((rows, c), lambda i, j=j: (i, j))
    out_shape = [jax.ShapeDtypeStruct((m, c), BF16)]
    out_specs = [pl.BlockSpec((rows, c), lambda i: (i, 0))]
    if emit_vn:
        out_shape.append(jax.ShapeDtypeStruct((m, c), F32))
        out_specs.append(pl.BlockSpec((rows, c), lambda i: (i, 0)))
    vmem = 6 * _nbytes((rows, c), uvz.dtype) + 2 * _nbytes((rows, c), BF16) + 8 * _nbytes((rows, c), F32)
    vmem += 2 * _nbytes(w_mix.shape, F32)
    return pl.pallas_call(
        functools.partial(_sgu_kernel, group_dim=c // n_groups),
        grid=(m // rows,),
        in_specs=[blk(0), blk(1), blk(2),
                  pl.BlockSpec((1, c), lambda i: (0, 0)),
                  pl.BlockSpec((1, c), lambda i: (0, 0)),
                  pl.BlockSpec((n_groups, chunk, chunk), lambda i: (0, 0, 0)),
                  pl.BlockSpec((chunk, n_groups), lambda i: (0, 0))],
        out_specs=out_specs,
        out_shape=out_shape,
        compiler_params=_params(1, vmem),
        name="spatial_gate",
    )(uvz, uvz, uvz, ln_g.reshape(1, c), ln_b.reshape(1, c), w_mix, bias_t)


def _layer0_inputs(h, norm_g, w_t, wf, bf, n_seq, with_cumsum):
    xn = rmsnorm(h, norm_g, BF16)
    parts, kv_out = [], []
    for p in range(8):
        is_kv = p in (1, 2, 5, 6)
        out = project([xn], w_t, BF16, col0=p * MIX_WIDTH, n_cols=MIX_WIDTH, heads_out=is_kv,
                      w_transposed=True, name="in_proj0")
        if is_kv:
            parts.append(out[0])
            kv_out.append(out[1])
        else:
            parts.append(out)
    forget = forget_gate(xn, wf, bf, n_seq, with_cumsum)
    return parts, kv_out, forget


def kernel(x_prompt, x_sample, cache_fox_k, cache_fox_v, cache_fox_logf, cache_sb_k, cache_sb_v,
           norm0_g, w_in0, b_forget, w_out0, norm1_g, w_in1, sgu_ln_g, sgu_ln_b, w_sp, b_sp, w_out1, final_g):
    bp, s, d = x_prompt.shape
    bs, t_new, _ = x_sample.shape
    past = cache_fox_k.shape[1]
    hp = x_prompt.reshape(bp * s, d)
    hs = x_sample.reshape(bs * t_new, d)

    n_main = 8 * MIX_WIDTH
    w_t = w_in0.T
    wf =jnp.pad(w_in0[:, n_main:], ((0, 0), (0, LANES - N_HEADS))).astype(BF16)
    bf = jnp.pad(b_forget, (0, LANES - N_HEADS)).reshape(1, LANES).astype(F32)

    (fq, fk, fv, fz, sq, sk, sv, sz), kv_prompt, (logf_t, cum_t) = _layer0_inputs(
        hp, norm0_g, w_t, wf, bf, bp, True)
    as_seq = lambda a: a.reshape(bp, s, MIX_WIDTH)
    f4 = cum_t.reshape(bp, HEAD_ROWS, s // ATTN_BLOCK, ATTN_BLOCK)
    mix_f = prompt_attention("fox", as_seq(fq), as_seq(fk), as_seq(fv), as_seq(fz), f4)
    mix_s = prompt_attention("sb", as_seq(sq), as_seq(sk), as_seq(sv), as_seq(sz))
    hp = project([mix_f.reshape(bp * s, MIX_WIDTH), mix_s.reshape(bp * s, MIX_WIDTH)], w_out0, F32,
                 residual=hp, name="out_proj0")
    fox_logf_prompt = jnp.swapaxes(logf_t[:, :N_HEADS, :], 1, 2)

    (gq, gk, gv, gz, tq, tk, tv, tz), kv_sample, (logf_s,) = _layer0_inputs(
        hs, norm0_g, w_t, wf, bf, 1, False)
    as_new = lambda a: a.reshape(bs, t_new, MIX_WIDTH)
    logf_new_t = jnp.swapaxes(logf_s[0, :N_HEADS, :].reshape(N_HEADS, bs, t_new), 0, 1)
    keys = -(-(past + t_new) // LANES) * LANES
    logf_all = jnp.concatenate([jnp.swapaxes(cache_fox_logf.astype(F32), 1, 2), logf_new_t], axis=2)
    logf_all = jnp.pad(logf_all, ((0, 0), (0, HEAD_ROWS - N_HEADS), (0, keys - past - t_new)))
    mix_g = sample_attention("fox", as_new(gq), as_new(gk), as_new(gv), cache_fox_k, cache_fox_v,
                             as_new(gz), logf_all)
    mix_t = sample_attention("sb", as_new(tq), as_new(tk), as_new(tv), cache_sb_k, cache_sb_v, as_new(tz))
    hs = project([mix_g.reshape(bs * t_new, MIX_WIDTH), mix_t.reshape(bs * t_new, MIX_WIDTH)], w_out0, F32,
                 residual=hs, name="out_proj0")
    fox_logf_sample = jnp.swapaxes(logf_new_t, 1, 2)

    c_width = w_in1.shape[1] // 3
    bias_t = b_sp.T.astype(F32)
    uvz_p = project([rmsnorm(hp, norm1_g, BF16)], w_in1, BF16, name="in_proj1")
    (act_p,) = spatial_gate(uvz_p, sgu_ln_g, sgu_ln_b, w_sp, bias_t, False)
    hp = project([act_p], w_out1, F32, tn=512, residual=hp, name="out_proj1")

    per_blk = C_CHUNK // t_new
    eye = jnp.eye(per_blk, dtype=F32)
    w_corner = w_sp[:, :t_new, :t_new]
    w_bd = jnp.einsum("ab,gts->gatbs", eye, w_corner).reshape(C_GROUPS, C_CHUNK, C_CHUNK)
    bias_bd = jnp.tile(b_sp[:, :t_new].T.astype(F32), (per_blk, 1))
    uvz_s = project([rmsnorm(hs, norm1_g, BF16)], w_in1, F32, name="in_proj1")
    act_s, vn_s = spatial_gate(uvz_s, sgu_ln_g, sgu_ln_b, w_bd, bias_bd, True)
    hs = project([act_s], w_out1, F32, tn=512, residual=hs, name="out_proj1")

    y_prompt = rmsnorm(hp, final_g, F32).reshape(bp, s, d)
    y_sample = rmsnorm(hs, final_g, F32).reshape(bs, t_new, d)
    fk4, fv4, sk4, sv4 = (a.reshape(bp, s, N_HEADS, HEAD_DIM) for a in kv_prompt)
    gk4, gv4, tk4, tv4 = (a.reshape(bs, t_new, N_HEADS, HEAD_DIM) for a in kv_sample)
    return (y_prompt, y_sample,
            fk4, fv4, fox_logf_prompt,
            gk4, gv4, fox_logf_sample,
            sk4, sv4,
            tk4, tv4,
            vn_s.reshape(bs, t_new, c_width))
```

```python
import functools

import jax
import jax.numpy as jnp
from jax import lax
from jax.experimental import pallas as pl
from jax.experimental.pallas import tpu as pltpu

F32 = jnp.float32
BF16 = jnp.bfloat16

HEAD_DIM = 128
N_HEADS = 8
MIX_WIDTH = N_HEADS * HEAD_DIM
C_GROUPS = 16
C_CHUNK = 128
RMS_EPS = 1e-6
LN_EPS = 1e-5
ATTN_SCALE = HEAD_DIM ** -0.5
LOG2E = 1.4426950408889634

LANES = 128
ATTN_BLOCK = 256
ATTN_SUPER = 1024
V7X_VMEM_BYTES = 64 * 1024 * 1024
VMEM_LIMIT_CAP = V7X_VMEM_BYTES - 8 * 1024 * 1024


def _params(n_grid_axes, vmem_bytes):
    limit = min(int(vmem_bytes * 1.25) + (4 << 20), VMEM_LIMIT_CAP)
    return pltpu.CompilerParams(
        dimension_semantics=("arbitrary",) * n_grid_axes,
        vmem_limit_bytes=limit,
    )


def _nbytes(shape, dtype):
    n = 1
    for s in shape:
        n *= s
    return n * jnp.dtype(dtype).itemsize


def _dot(a, b):
    return jnp.dot(a, b, preferred_element_type=F32)


def _dot_nt(a, b):
    return lax.dot_general(a, b, (((1,), (1,)), ((), ())), preferred_element_type=F32)


def _log_sigmoid(x):
    return jnp.minimum(x, 0.0) - jnp.log(1.0 + jnp.exp(-jnp.abs(x)))


def _silu(x):
    return x / (1.0 + jnp.exp2(x * -LOG2E))


def _widen(x, n):
    return x if n == LANES else jnp.concatenate([x] * (n // LANES), axis=1)


def _shift_div(x, n):
    assert n & (n - 1) == 0
    return lax.shift_right_logical(x, jnp.full_like(x, n.bit_length() - 1))


def _split3(x):
    hi = x.astype(BF16)
    r1 = x - hi.astype(F32)
    mid = r1.astype(BF16)
    lo = (r1 - mid.astype(F32)).astype(BF16)
    return hi, mid, lo


def _split2(x):
    hi = x.astype(BF16)
    lo = (x - hi.astype(F32)).astype(BF16)
    return hi, lo


def _rms_kernel(x_ref, g_ref, o_ref):
    x = x_ref[...]
    ms = jnp.mean(x * x, axis=-1, keepdims=True)
    o_ref[...] = (x * lax.rsqrt(ms + RMS_EPS) * g_ref[...]).astype(o_ref.dtype)


def rmsnorm(x, g, out_dtype, tm=512):
    m, d = x.shape
    tm = min(tm, m)
    vmem = 2 * _nbytes((tm, d), F32) + 2 * _nbytes((tm, d), out_dtype)
    return pl.pallas_call(
        _rms_kernel,
        grid=(m // tm,),
        in_specs=[pl.BlockSpec((tm, d), lambda i: (i, 0)),
                  pl.BlockSpec((1, d), lambda i: (0, 0))],
        out_specs=pl.BlockSpec((tm, d), lambda i: (i, 0)),
        out_shape=jax.ShapeDtypeStruct((m, d), out_dtype),
        compiler_params=_params(1, vmem),
        name="rmsnorm",
    )(x, g.reshape(1, d))


def _proj_kernel(*refs, k_sizes, has_res, w_transposed, heads_out, has_extra):
    it = iter(refs)
    n_sets = 2 if has_extra else 1
    x_sets = [[next(it) for _ in k_sizes] for _ in range(n_sets)]
    w_ref = next(it)
    res_refs = [next(it) if has_res else None for _ in range(n_sets)]
    out_sets = [(next(it), next(it) if heads_out else None) for _ in range(n_sets)]
    wb_ref = next(it)

    @pl.when(pl.program_id(1) == 0)
    def _():
        wb_ref[...] = w_ref[...].astype(BF16)

    def row_set(x_refs, res_ref, o_ref, o4_ref):
        k0 = 0
        acc = None
        for x_ref, k in zip(x_refs, k_sizes):
            if w_transposed:
                part = _dot_nt(x_ref[...], wb_ref[:, k0:k0 + k])
            else:
                part = _dot(x_ref[...], wb_ref[k0:k0 + k, :])
            acc = part if acc is None else acc + part
            k0 += k
        if has_res:
            acc = acc + res_ref[...]
        o_ref[...] = acc.astype(o_ref.dtype)
        if heads_out:
            o4_ref[...] = acc.reshape(acc.shape[0], N_HEADS, HEAD_DIM)

    row_set(x_sets[0], res_refs[0], *out_sets[0])
    if has_extra:
        @pl.when(pl.program_id(1) == pl.num_programs(1) - 1)
        def _():
            row_set(x_sets[1], res_refs[1], *out_sets[1])


def project(row_sets, w, *, col0=0, n_cols=None, tn=1024, tm=1024, heads_out=False, w_transposed=False,
            name="proj"):
    assert 1 <= len(row_sets) <= 2
    xs0, _, res0 = row_sets[0]
    m = xs0[0].shape[0]
    k_sizes = tuple(x.shape[1] for x in xs0)
    k_total = sum(k_sizes)
    k_axis, n_axis = (1, 0) if w_transposed else (0, 1)
    assert k_total == w.shape[k_axis]
    n_cols = w.shape[n_axis] - col0 if n_cols is None else n_cols
    tm = min(tm, m)
    assert m % tm == 0 and n_cols % tn == 0 and col0 % tn == 0
    has_res = res0 is not None
    n_j = n_cols // tn
    cb0 = col0 // tn
    if w_transposed:
        w_block, w_index = (tn, k_total), lambda j, i: (cb0 + j, 0)
    else:
        w_block, w_index = (k_total, tn), lambda j, i: (0, cb0 + j)
    fixed = pl.Buffered(1)
    w_spec = pl.BlockSpec(w_block, w_index, pipeline_mode=fixed) if n_j == 1 else pl.BlockSpec(w_block, w_index)
    vmem = (1 if n_j == 1 else 2) * _nbytes(w_block, w.dtype) + _nbytes(w_block, BF16)

    x_specs, x_args, res_specs, res_args, out_specs, out_shape = [], [], [], [], [], []
    for s, (xs, out_dtype, res) in enumerate(row_sets):
        rows = xs[0].shape[0]
        assert tuple(x.shape[1] for x in xs) == k_sizes and (res is not None) == has_res
        if s == 0:
            tile, row_index, bufs = tm, (lambda j, i: i), 2
            x_specs += [pl.BlockSpec((tile, k), lambda j, i: (i, 0)) for k in k_sizes]
        else:
            tile, row_index, bufs = rows, (lambda j, i: 0), 1
            x_specs += [pl.BlockSpec((tile, k), lambda j, i: (0, 0), pipeline_mode=fixed) for k in k_sizes]
        x_args += list(xs)
        vmem += sum(bufs * _nbytes((tile, k), BF16) for k in k_sizes)
        if has_res:
            res_specs.append(pl.BlockSpec((tile, tn), lambda j, i, r=row_index: (r(j, i), j)))
            res_args.append(res)
            vmem += 2 * _nbytes((tile, tn), F32)
        out_specs.append(pl.BlockSpec((tile, tn), lambda j, i, r=row_index: (r(j, i), j)))
        out_shape.append(jax.ShapeDtypeStruct((rows, n_cols), out_dtype))
        vmem += 2 * _nbytes((tile, tn), out_dtype) + _nbytes((tile, tn), F32)
        if heads_out:
            assert n_cols == tn == MIX_WIDTH
            out_specs.append(pl.BlockSpec((tile, N_HEADS, HEAD_DIM), lambda j, i, r=row_index: (r(j, i), 0, 0)))
            out_shape.append(jax.ShapeDtypeStruct((rows, N_HEADS, HEAD_DIM), F32))
            vmem += 2 * _nbytes((tile, tn), F32)
    outs = pl.pallas_call(
        functools.partial(_proj_kernel, k_sizes=k_sizes, has_res=has_res, w_transposed=w_transposed,
                          heads_out=heads_out, has_extra=len(row_sets) == 2),
        grid=(n_j, m // tm),
        in_specs=x_specs + [w_spec] + res_specs,
        out_specs=out_specs,
        out_shape=out_shape,
        scratch_shapes=[pltpu.VMEM(w_block, BF16)],
        compiler_params=_params(2, vmem),
        name=name,
    )(*x_args, w, *res_args)
    per_set = 2 if heads_out else 1
    return [tuple(outs[s * per_set:(s + 1) * per_set]) if heads_out else outs[s] for s in range(len(row_sets))]


CUM_CHUNK = 256
HEAD_ROWS = 16


def _tri_incl(n):
    r = lax.broadcasted_iota(jnp.int32, (n, n), 0)
    c = lax.broadcasted_iota(jnp.int32, (n, n), 1)
    return jnp.where(r <= c, 1.0, 0.0).astype(BF16)


def _cumsum_lanes(x, chunk):
    n = x.shape[1]
    tri = _tri_incl(chunk)
    carry = jnp.zeros((x.shape[0], 1), F32)
    outs = []
    for c in range(n // chunk):
        hi, mid, lo = _split3(x[:, c * chunk:(c + 1) * chunk])
        loc = _dot(hi, tri) + _dot(mid, tri) + _dot(lo, tri)
        outs.append(loc + carry)
        carry = carry + loc[:, chunk - 1:chunk]
    return jnp.concatenate(outs, axis=1)


def _forget_kernel(x_ref, w_ref, b_ref, logf_ref, *cum_refs):
    logit = _dot(x_ref[...], w_ref[...]) + b_ref[...]
    logf_t = _log_sigmoid(logit).T[:HEAD_ROWS, :]
    logf_ref[0] = logf_t
    if cum_refs:
        cum_refs[0][0] = _cumsum_lanes(logf_t, CUM_CHUNK)


def forget_gate(xn, wf, bf, n_seq, with_cumsum):
    m, d = xn.shape
    t = m // n_seq
    out_shape = [jax.ShapeDtypeStruct((n_seq, HEAD_ROWS, t), F32)]
    out_specs = [pl.BlockSpec((1, HEAD_ROWS, t), lambda s: (s, 0, 0))]
    if with_cumsum:
        out_shape = out_shape * 2
        out_specs = out_specs * 2
    vmem = 2 * _nbytes((t, d), BF16) + 2 * _nbytes((d, LANES), BF16) + 8 * _nbytes((t, LANES), F32)
    return pl.pallas_call(
        _forget_kernel,
        grid=(n_seq,),
        in_specs=[pl.BlockSpec((t, d), lambda s: (s, 0)),
                  pl.BlockSpec((d, LANES), lambda s: (0, 0)),
                  pl.BlockSpec((1, LANES), lambda s: (0, 0))],
        out_specs=out_specs,
        out_shape=out_shape,
        compiler_params=_params(1, vmem),
        name="forget_gate",
    )(xn, wf, bf)


def _pipelined(steps, stages):
    vals = [None] * len(steps)
    for t in range(len(steps) + len(stages) - 1):
        for j, stage in enumerate(stages):
            i = t - j
            if 0 <= i < len(steps):
                vals[i] = stage(steps[i]) if j == 0 else stage(steps[i], vals[i])


def _fox_prompt_kernel(q_ref, k_ref, v_ref, f_ref, z_ref, o_ref, m_ref, l_ref, acc_ref, *, seq, blk, sup):
    h = pl.program_id(1)
    n_sub = sup // blk

    for qs in range(seq // sup):
        q0 = qs * sup
        m_ref[...] = jnp.full(m_ref.shape, -jnp.inf, F32)
        l_ref[...] = jnp.zeros(l_ref.shape, F32)
        acc_ref[...] = jnp.zeros(acc_ref.shape, F32)

        def logits(step):
            kj, row_lo, diagonal = step
            q = q_ref[0, q0 + row_lo:q0 + sup, :]
            s = _dot_nt(q, k_ref[0, kj * blk:(kj + 1) * blk, :]) * (ATTN_SCALE * LOG2E)
            s = s - f_ref[0, pl.ds(h, 1), kj:kj + 1, :].reshape(1, blk) * LOG2E
            if diagonal:
                r = lax.broadcasted_iota(jnp.int32, s.shape, 0)
                c = lax.broadcasted_iota(jnp.int32, s.shape, 1)
                s = jnp.where(c <= r, s, -jnp.inf)
            return s

        def weights(step, s):
            sl = slice(step[1], sup)
            m_old = m_ref[sl, :]
            m_new = jnp.maximum(m_old, jnp.max(s, axis=-1, keepdims=True))
            alpha = jnp.exp2(m_old - m_new)
            p = jnp.exp2(s - _widen(m_new, blk))
            m_ref[sl, :] = m_new
            l_ref[sl, :] = alpha * l_ref[sl, :] + jnp.sum(p, axis=-1, keepdims=True)
            return alpha, p.astype(BF16)

        def accumulate(step, alpha_p):
            kj, row_lo, _ = step
            sl = slice(row_lo, sup)
            alpha, p = alpha_p
            acc_ref[sl, :] = alpha * acc_ref[sl, :] + _dot(p, v_ref[0, kj * blk:(kj + 1) * blk, :])

        steps = [(kj, 0, False) for kj in range(qs * n_sub)]
        steps += [(qs * n_sub + c, c * blk, True) for c in range(n_sub)]
        for step in steps:
            accumulate(step, weights(step, logits(step)))
        gate = _silu(z_ref[0, q0:q0 + sup, :].astype(F32))
        o_ref[0, q0:q0 + sup, :] = (acc_ref[...] / l_ref[...] * gate).astype(o_ref.dtype)


def _tri_after(n):
    r = lax.broadcasted_iota(jnp.int32, (n, n), 0)
    c = lax.broadcasted_iota(jnp.int32, (n, n), 1)
    return jnp.where(r > c, 1.0, 0.0).astype(BF16)


def _sb_block(w, carry, before, tri):
    log_stay, log_take = _sb_logs(w, before)
    a = _sb_weights(log_take, _sb_suffix(log_stay, tri), carry, before)
    return a, carry + jnp.sum(log_stay, axis=-1, keepdims=True)


def _sb_logs(w, before):
    nw = -w
    soft = jnp.log(1.0 + jnp.exp2(jnp.minimum(w, nw))) * LOG2E
    log_stay = jnp.minimum(nw, 0.0) - soft
    log_take = log_stay + w
    if before is not None:
        log_stay = jnp.where(before, log_stay, 0.0)
    return log_stay, log_take


def _sb_suffix(log_stay, tri):
    hi, lo = _split2(log_stay)
    return _dot(hi, tri) + _dot(lo, tri)


def _sb_weights(log_take, suffix, carry, before):
    a = jnp.exp2(log_take + (suffix + _widen(carry, suffix.shape[1])))
    if before is not None:
        a = jnp.where(before, a, 0.0)
    return a


def _sb_prompt_kernel(q_ref, k_ref, v_ref, z_ref, o_ref, carry_ref, acc_ref, *, seq, blk, sup):
    n_sub = sup // blk
    tri = _tri_after(blk)

    for qs in range(seq // sup):
        q0 = qs * sup
        carry_ref[...] = jnp.zeros(carry_ref.shape, F32)
        acc_ref[...] = jnp.zeros(acc_ref.shape, F32)

        def mask(step):
            if not step[2]:
                return None
            r = lax.broadcasted_iota(jnp.int32, (sup - step[1], blk), 0)
            c = lax.broadcasted_iota(jnp.int32, (sup - step[1], blk), 1)
            return c < r

        def logits(step):
            kj, row_lo, _ = step
            q = q_ref[0, q0 + row_lo:q0 + sup, :]
            return _dot_nt(q, k_ref[0, kj * blk:(kj + 1) * blk, :]) * (ATTN_SCALE * LOG2E)

        def suffix_sums(step, w):
            sl = slice(step[1], sup)
            log_stay, log_take = _sb_logs(w, mask(step))
            carry = carry_ref[sl, :]
            carry_ref[sl, :] = carry + jnp.sum(log_stay, axis=-1, keepdims=True)
            return log_take, _sb_suffix(log_stay, tri), carry

        def accumulate(step, parts):
            kj, row_lo, _ = step
            sl = slice(row_lo, sup)
            a = _sb_weights(*parts, mask(step)).astype(BF16)
            acc_ref[sl, :] = acc_ref[sl, :] + _dot(a, v_ref[0, kj * blk:(kj + 1) * blk, :])

        steps = [(qs * n_sub + c, c * blk, True) for c in reversed(range(n_sub))]
        steps += [(kj, 0, False) for kj in reversed(range(qs * n_sub))]
        _pipelined(steps, [logits, suffix_sums, accumulate])
        gate = _silu(z_ref[0, q0:q0 + sup, :].astype(F32))
        o_ref[0, q0:q0 + sup, :] = (acc_ref[...] * gate).astype(o_ref.dtype)


def prompt_attention(kind, q, k, v, z, f4=None):
    b, s, w = q.shape
    blk = ATTN_BLOCK
    sup = min(ATTN_SUPER, s)
    head_spec = pl.BlockSpec((1, s, HEAD_DIM), lambda bi, hi: (bi, 0, hi))
    in_specs = [head_spec, head_spec, head_spec]
    args = [q, k, v]
    col_state = pltpu.VMEM((sup, LANES), F32)
    acc_state = pltpu.VMEM((sup, HEAD_DIM), F32)
    if kind == "fox":
        in_specs.append(pl.BlockSpec((1, HEAD_ROWS, s // blk, blk), lambda bi, hi: (bi, 0, 0, 0)))
        args.append(f4)
        body = functools.partial(_fox_prompt_kernel, seq=s, blk=blk, sup=sup)
        scratch = [col_state, col_state, acc_state]
    else:
        body = functools.partial(_sb_prompt_kernel, seq=s, blk=blk, sup=sup)
        scratch = [col_state, acc_state]
    in_specs.append(head_spec)
    args.append(z)
    vmem = 2 * 5 * _nbytes((s, HEAD_DIM), BF16) + 2 * _nbytes((HEAD_ROWS, s), F32)
    vmem += 3 * _nbytes((sup, LANES), F32) + 12 * _nbytes((sup, blk), F32)
    return pl.pallas_call(
        body,
        grid=(b, w // HEAD_DIM),
        in_specs=in_specs,
        out_specs=head_spec,
        out_shape=jax.ShapeDtypeStruct((b, s, w), BF16),
        scratch_shapes=scratch,
        compiler_params=_params(2, vmem),
        name=kind + "_prompt_attention",
    )(*args)


def _sample_kernel(*refs, kind, past, t_new, keys):
    if kind == "fox":
        q_ref, kn_ref, vn_ref, kc_ref, vc_ref, lf_ref, z_ref, o_ref, ka_ref, va_ref = refs
    else:
        q_ref, kn_ref, vn_ref, kc_ref, vc_ref, z_ref, o_ref, ka_ref, va_ref = refs
    rows = N_HEADS * t_new
    width = N_HEADS * HEAD_DIM
    pad = keys - past - t_new

    for src_c, src_n, dst in ((kc_ref, kn_ref, ka_ref), (vc_ref, vn_ref, va_ref)):
        dst[0:past, :] = src_c[0].reshape(past, width).astype(BF16)
        dst[past:past + t_new, :] = src_n[0]
        dst[past + t_new:keys, :] = jnp.zeros((pad, width), BF16)

    r_q = lax.broadcasted_iota(jnp.int32, (rows, width), 0)
    c_q = lax.broadcasted_iota(jnp.int32, (rows, width), 1)
    q_tiled = jnp.concatenate([q_ref[0]] * N_HEADS, axis=0)
    same_head = _shift_div(r_q, t_new) == _shift_div(c_q, HEAD_DIM)
    q_bd = jnp.where(same_head, q_tiled, jnp.zeros_like(q_tiled))

    qk = _dot_nt(q_bd, ka_ref[...])
    r_s = lax.broadcasted_iota(jnp.int32, (rows, keys), 0)
    c_s = lax.broadcasted_iota(jnp.int32, (rows, keys), 1)
    q_pos = past + (r_s & (t_new - 1))

    if kind == "fox":
        f_all = _cumsum_lanes(lf_ref[0], LANES)
        f_rows = jnp.concatenate(
            [jnp.broadcast_to(f_all[hh:hh + 1, :], (t_new, keys)) for hh in range(N_HEADS)], axis=0)
        s = jnp.where(c_s <= q_pos, qk * ATTN_SCALE - f_rows, -jnp.inf)
        m = jnp.max(s, axis=-1, keepdims=True)
        p = jnp.exp(s - m)
        wts = p
        norm = jnp.sum(p, axis=-1, keepdims=True)
    else:
        w = qk * (ATTN_SCALE * LOG2E)
        before = c_s < q_pos
        tri = _tri_after(LANES)
        carry = jnp.zeros((rows, LANES), F32)
        chunks = [None] * (keys // LANES)
        for c in reversed(range(keys // LANES)):
            sl = slice(c * LANES, (c + 1) * LANES)
            chunks[c], carry = _sb_block(w[:, sl], carry, before[:, sl], tri)
        wts = jnp.concatenate(chunks, axis=1)
        norm = None

    o_full = _dot(wts.astype(BF16), va_ref[...])
    if norm is not None:
        o_full = o_full / norm
    o = jnp.concatenate(
        [o_full[hh * t_new:(hh + 1) * t_new, hh * HEAD_DIM:(hh + 1) * HEAD_DIM] for hh in range(N_HEADS)],
        axis=1)
    o_ref[0] = (o * _silu(z_ref[0].astype(F32))).astype(o_ref.dtype)


def sample_attention(kind, q, k_new, v_new, cache_k, cache_v, z, logf_all=None):
    b, t_new, w = q.shape
    past = cache_k.shape[1]
    assert t_new & (t_new - 1) == 0
    keys = -(-(past + t_new) // LANES) * LANES
    new_spec = pl.BlockSpec((1, t_new, w), lambda i: (i, 0, 0))
    cache_spec = pl.BlockSpec((1, past, N_HEADS, HEAD_DIM), lambda i: (i, 0, 0, 0))
    in_specs = [new_spec, new_spec, new_spec, cache_spec, cache_spec]
    args = [q, k_new, v_new, cache_k, cache_v]
    scratch = [pltpu.VMEM((keys, w), BF16), pltpu.VMEM((keys, w), BF16)]
    vmem = 4 * _nbytes((past, w), F32) + 2 * _nbytes((keys, w), BF16)
    vmem += 24 * _nbytes((N_HEADS * t_new, keys), F32)
    if kind == "fox":
        in_specs.append(pl.BlockSpec((1, HEAD_ROWS, keys), lambda i: (i, 0, 0)))
        args.append(logf_all)
    in_specs.append(new_spec)
    args.append(z)
    return pl.pallas_call(
        functools.partial(_sample_kernel, kind=kind, past=past, t_new=t_new, keys=keys),
        grid=(b,),
        in_specs=in_specs,
        out_specs=new_spec,
        out_shape=jax.ShapeDtypeStruct((b, t_new, w), BF16),
        scratch_shapes=scratch,
        compiler_params=_params(1, vmem),
        name=kind + "_sample_attention",
    )(*args)


def _sgu_kernel(u_ref, v_ref, z_ref, g_ref, b_ref, w_ref, bias_ref, o_ref, *vn_refs, group_dim):
    n_groups, chunk, _ = w_ref.shape
    r = lax.broadcasted_iota(jnp.int32, (chunk, chunk), 0)
    c = lax.broadcasted_iota(jnp.int32, (chunk, chunk), 1)
    causal = c <= r
    w_causal = [jnp.where(causal, w_ref[g], 0.0).astype(BF16) for g in range(n_groups)]
    for s in range(v_ref.shape[0] // chunk):
        rs = slice(s * chunk, (s + 1) * chunk)
        v = v_ref[rs, :].astype(F32)
        mu = jnp.mean(v, axis=-1, keepdims=True)
        vc = v - mu
        var = jnp.mean(vc * vc, axis=-1, keepdims=True)
        vn = vc * lax.rsqrt(var + LN_EPS) * g_ref[...] + b_ref[...]
        if vn_refs:
            vn_refs[0][rs, :] = vn
        vnb = vn.astype(BF16)
        for g in range(n_groups):
            sl = slice(g * group_dim, (g + 1) * group_dim)
            mixed = _dot(w_causal[g], vnb[:, sl]) + bias_ref[:, g:g + 1]
            gate = _silu(z_ref[rs, sl].astype(F32))
            o_ref[rs, sl] = (u_ref[rs, sl].astype(F32) * mixed * gate).astype(o_ref.dtype)


SGU_CHUNKS_PER_STEP = 2


def spatial_gate(uvz, ln_g, ln_b, w_mix, bias_t, emit_vn):
    m, c3 = uvz.shape
    c = c3 // 3
    n_groups, chunk, _ = w_mix.shape
    rows = SGU_CHUNKS_PER_STEP * chunk
    assert m % rows == 0
    blk = lambda j: pl.BlockSpec((rows, c), lambda i, j=j: (i, j))
    out_shape = [jax.ShapeDtypeStruct((m, c), BF16)]
    out_specs = [pl.BlockSpec((rows, c), lambda i: (i, 0))]
    if emit_vn:
        out_shape.append(jax.ShapeDtypeStruct((m, c), F32))
        out_specs.append(pl.BlockSpec((rows, c), lambda i: (i, 0)))
    vmem = 6 * _nbytes((rows, c), uvz.dtype) + 2 * _nbytes((rows, c), BF16) + 8 * _nbytes((rows, c), F32)
    vmem += 2 * _nbytes(w_mix.shape, F32)
    return pl.pallas_call(
        functools.partial(_sgu_kernel, group_dim=c // n_groups),
        grid=(m // rows,),
        in_specs=[blk(0), blk(1), blk(2),
                  pl.BlockSpec((1, c), lambda i: (0, 0)),
                  pl.BlockSpec((1, c), lambda i: (0, 0)),
                  pl.BlockSpec((n_groups, chunk, chunk), lambda i: (0, 0, 0)),
                  pl.BlockSpec((chunk, n_groups), lambda i: (0, 0))],
        out_specs=out_specs,
        out_shape=out_shape,
        compiler_params=_params(1, vmem),
        name="spatial_gate",
    )(uvz, uvz, uvz, ln_g.reshape(1, c), ln_b.reshape(1, c), w_mix, bias_t)


def _layer0_inputs(xn_prompt, xn_sample, w_t):
    parts = ([], [])
    kv_out = ([], [])
    for p in range(8):
        is_kv = p in (1, 2, 5, 6)
        outs = project([([xn_prompt], BF16, None), ([xn_sample], BF16, None)], w_t, col0=p * MIX_WIDTH,
                       n_cols=MIX_WIDTH, heads_out=is_kv, w_transposed=True, name="in_proj0")
        for g, out in enumerate(outs):
            if is_kv:
                parts[g].append(out[0])
                kv_out[g].append(out[1])
            else:
                parts[g].append(out)
    return parts, kv_out


def kernel(x_prompt, x_sample, cache_fox_k, cache_fox_v, cache_fox_logf, cache_sb_k, cache_sb_v,
           norm0_g, w_in0, b_forget, w_out0, norm1_g, w_in1, sgu_ln_g, sgu_ln_b, w_sp, b_sp, w_out1, final_g):
    bp, s, d = x_prompt.shape
    bs, t_new, _ = x_sample.shape
    past = cache_fox_k.shape[1]
    hp = x_prompt.reshape(bp * s, d)
    hs = x_sample.reshape(bs * t_new, d)

    n_main = 8 * MIX_WIDTH
    w_t = w_in0.T
    wf = jnp.pad(w_in0[:, n_main:], ((0, 0), (0, LANES - N_HEADS))).astype(BF16)
    bf = jnp.pad(b_forget, (0, LANES - N_HEADS)).reshape(1, LANES).astype(F32)

    xn_p = rmsnorm(hp, norm0_g, BF16)
    xn_s = rmsnorm(hs, norm0_g, BF16)
    (parts_p, parts_s), (kv_prompt, kv_sample) = _layer0_inputs(xn_p, xn_s, w_t)
    fq, fk, fv, fz, sq, sk, sv, sz = parts_p
    gq, gk, gv, gz, tq, tk, tv, tz = parts_s
    logf_t, cum_t = forget_gate(xn_p, wf, bf, bp, True)
    (logf_s,) = forget_gate(xn_s, wf, bf, 1, False)

    as_seq = lambda a: a.reshape(bp, s, MIX_WIDTH)
    f4 = cum_t.reshape(bp, HEAD_ROWS, s // ATTN_BLOCK, ATTN_BLOCK)
    mix_f = prompt_attention("fox", as_seq(fq), as_seq(fk), as_seq(fv), as_seq(fz), f4)
    mix_s = prompt_attention("sb", as_seq(sq), as_seq(sk), as_seq(sv), as_seq(sz))
    fox_logf_prompt = jnp.swapaxes(logf_t[:, :N_HEADS, :], 1, 2)

    as_new = lambda a: a.reshape(bs, t_new, MIX_WIDTH)
    logf_new_t = jnp.swapaxes(logf_s[0, :N_HEADS, :].reshape(N_HEADS, bs, t_new), 0, 1)
    keys = -(-(past + t_new) // LANES) * LANES
    logf_all = jnp.concatenate([jnp.swapaxes(cache_fox_logf.astype(F32), 1, 2), logf_new_t], axis=2)
    logf_all = jnp.pad(logf_all, ((0, 0), (0, HEAD_ROWS - N_HEADS), (0, keys - past - t_new)))
    mix_g = sample_attention("fox", as_new(gq), as_new(gk), as_new(gv), cache_fox_k, cache_fox_v,
                             as_new(gz), logf_all)
    mix_t = sample_attention("sb", as_new(tq), as_new(tk), as_new(tv), cache_sb_k, cache_sb_v, as_new(tz))
    fox_logf_sample = jnp.swapaxes(logf_new_t, 1, 2)
    flat = lambda a: a.reshape(-1, MIX_WIDTH)
    hp, hs = project([([flat(mix_f), flat(mix_s)], F32, hp), ([flat(mix_g), flat(mix_t)], F32, hs)],
                     w_out0, tn=512, name="out_proj0")

    c_width = w_in1.shape[1] // 3
    bias_t = b_sp.T.astype(F32)
    uvz_p, uvz_s = project([([rmsnorm(hp, norm1_g, BF16)], BF16, None), ([rmsnorm(hs, norm1_g, BF16)], F32, None)],
                           w_in1, name="in_proj1")
    (act_p,) = spatial_gate(uvz_p, sgu_ln_g, sgu_ln_b, w_sp, bias_t, False)

    per_blk = C_CHUNK // t_new
    eye = jnp.eye(per_blk, dtype=F32)
    w_corner = w_sp[:, :t_new, :t_new]
    w_bd = jnp.einsum("ab,gts->gatbs", eye, w_corner).reshape(C_GROUPS, C_CHUNK, C_CHUNK)
    bias_bd = jnp.tile(b_sp[:, :t_new].T.astype(F32), (per_blk, 1))
    act_s, vn_s = spatial_gate(uvz_s, sgu_ln_g, sgu_ln_b, w_bd, bias_bd, True)
    hp, hs = project([([act_p], F32, hp), ([act_s], F32, hs)], w_out1, tn=512, name="out_proj1")

    y_prompt = rmsnorm(hp, final_g, F32).reshape(bp, s, d)
    y_sample = rmsnorm(hs, final_g, F32).reshape(bs, t_new, d)
    fk4, fv4, sk4, sv4 = (a.reshape(bp, s, N_HEADS, HEAD_DIM) for a in kv_prompt)
    gk4, gv4, tk4, tv4 = (a.reshape(bs, t_new, N_HEADS, HEAD_DIM) for a in kv_sample)
    return (y_prompt, y_sample,
            fk4, fv4, fox_logf_prompt,
            gk4, gv4, fox_logf_sample,
            sk4, sv4,
            tk4, tv4,
            vn_s.reshape(bs, t_new, c_width))
```

```python
import functools

import jax
import jax.numpy as jnp
from jax import lax
from jax.experimental import pallas as pl
from jax.experimental.pallas import tpu as pltpu

F32 = jnp.float32
BF16 = jnp.bfloat16

HEAD_DIM = 128
N_HEADS = 8
MIX_WIDTH = N_HEADS * HEAD_DIM
C_GROUPS = 16
C_CHUNK = 128
RMS_EPS = 1e-6
LN_EPS = 1e-5
ATTN_SCALE = HEAD_DIM ** -0.5
LOG2E = 1.4426950408889634
QK_SCALE = ATTN_SCALE * LOG2E

LANES = 128
ATTN_BLOCK = 256
ATTN_SUPER = 1024
V7X_VMEM_BYTES = 64 * 1024 * 1024
VMEM_LIMIT_CAP = V7X_VMEM_BYTES - 8 * 1024 * 1024


def _params(n_grid_axes, vmem_bytes, flags=None):
    limit = min(int(vmem_bytes * 1.25) + (4 << 20), VMEM_LIMIT_CAP)
    return pltpu.CompilerParams(
        dimension_semantics=("arbitrary",) * n_grid_axes,
        vmem_limit_bytes=limit,
        flags=flags,
    )


def _nbytes(shape, dtype):
    n = 1
    for s in shape:
        n *= s
    return n * jnp.dtype(dtype).itemsize


def _dot(a, b):
    return jnp.dot(a, b, preferred_element_type=F32)


def _dot_nt(a, b):
    return lax.dot_general(a, b, (((1,), (1,)), ((), ())), preferred_element_type=F32)


def _log_sigmoid(x):
    return jnp.minimum(x, 0.0) - jnp.log(1.0 + jnp.exp(-jnp.abs(x)))


def _silu(x):
    return x / (1.0 + jnp.exp2(x * -LOG2E))


def _widen(x, n):
    return x if n == LANES else jnp.concatenate([x] * (n // LANES), axis=1)


def _shift_div(x, n):
    assert n & (n - 1) == 0
    return lax.shift_right_logical(x, jnp.full_like(x, n.bit_length() - 1))


def _split3(x):
    hi = x.astype(BF16)
    r1 = x - hi.astype(F32)
    mid = r1.astype(BF16)
    lo = (r1 - mid.astype(F32)).astype(BF16)
    return hi, mid, lo


def _rms_kernel(x_ref, g_ref, o_ref):
    x = x_ref[...]
    ms = jnp.mean(x * x, axis=-1, keepdims=True)
    o_ref[...] = (x * lax.rsqrt(ms + RMS_EPS) * g_ref[...]).astype(o_ref.dtype)


def rmsnorm(x, g, out_dtype, tm=512):
    m, d = x.shape
    tm = min(tm, m)
    vmem = 2 * _nbytes((tm, d), F32) + 2 * _nbytes((tm, d), out_dtype)
    return pl.pallas_call(
        _rms_kernel,
        grid=(m // tm,),
        in_specs=[pl.BlockSpec((tm, d), lambda i: (i, 0)),
                  pl.BlockSpec((1, d), lambda i: (0, 0))],
        out_specs=pl.BlockSpec((tm, d), lambda i: (i, 0)),
        out_shape=jax.ShapeDtypeStruct((m, d), out_dtype),
        compiler_params=_params(1, vmem),
        name="rmsnorm",
    )(x, g.reshape(1, d))


def _proj_kernel(*refs, k_sizes, has_res, w_transposed, heads_out, has_extra, out_scale):
    it = iter(refs)
    n_sets = 2 if has_extra else 1
    x_sets = [[next(it) for _ in k_sizes] for _ in range(n_sets)]
    w_ref = next(it)
    res_refs = [next(it) if has_res else None for _ in range(n_sets)]
    out_sets = [(next(it), next(it) if heads_out else None) for _ in range(n_sets)]
    if w_ref.dtype == BF16:
        wb_ref = w_ref
    else:
        wb_ref = next(it)

        @pl.when(pl.program_id(1) == 0)
        def _():
            wb_ref[...] = w_ref[...].astype(BF16)

    def row_set(x_refs, res_ref, o_ref, o4_ref):
        k0 = 0
        acc = None
        for x_ref, k in zip(x_refs, k_sizes):
            if w_transposed:
                part = _dot_nt(x_ref[...], wb_ref[:, k0:k0 + k])
            else:
                part = _dot(x_ref[...], wb_ref[k0:k0 + k, :])
            acc = part if acc is None else acc + part
            k0 += k
        if has_res:
            acc = acc + res_ref[...]
        o_ref[...] = (acc if out_scale is None else acc * out_scale).astype(o_ref.dtype)
        if heads_out:
            o4_ref[...] = acc.reshape(acc.shape[0], N_HEADS, HEAD_DIM)

    row_set(x_sets[0], res_refs[0], *out_sets[0])
    if has_extra:
        @pl.when(pl.program_id(1) == pl.num_programs(1) - 1)
        def _():
            row_set(x_sets[1], res_refs[1], *out_sets[1])


def project(row_sets, w, *, col0=0, n_cols=None, tn=1024, tm=1024, heads_out=False, w_transposed=False,
            out_scale=None, name="proj"):
    assert 1 <= len(row_sets) <= 2
    xs0, _, res0 = row_sets[0]
    m = xs0[0].shape[0]
    k_sizes = tuple(x.shape[1] for x in xs0)
    k_total = sum(k_sizes)
    k_axis, n_axis = (1, 0) if w_transposed else (0, 1)
    assert k_total == w.shape[k_axis]
    n_cols = w.shape[n_axis] - col0 if n_cols is None else n_cols
    tm = min(tm, m)
    assert m % tm == 0 and n_cols % tn == 0 and col0 % tn == 0
    has_res = res0 is not None
    n_j = n_cols // tn
    cb0 = col0 // tn
    if w_transposed:
        w_block, w_index = (tn, k_total), lambda j, i: (cb0 + j, 0)
    else:
        w_block, w_index = (k_total, tn), lambda j, i: (0, cb0 + j)
    fixed = pl.Buffered(1)
    w_spec = pl.BlockSpec(w_block, w_index, pipeline_mode=fixed) if n_j == 1 else pl.BlockSpec(w_block, w_index)
    cast_w = w.dtype != BF16
    vmem = (1 if n_j == 1 else 2) * _nbytes(w_block, w.dtype) + (_nbytes(w_block, BF16) if cast_w else 0)

    x_specs, x_args, res_specs, res_args, out_specs, out_shape = [], [], [], [], [], []
    for s, (xs, out_dtype, res) in enumerate(row_sets):
        rows = xs[0].shape[0]
        assert tuple(x.shape[1] for x in xs) == k_sizes and (res is not None) == has_res
        if s == 0:
            tile, row_index, bufs = tm, (lambda j, i: i), 2
            x_specs += [pl.BlockSpec((tile, k), lambda j, i: (i, 0)) for k in k_sizes]
        else:
            tile, row_index, bufs = rows, (lambda j, i: 0), 1
            x_specs += [pl.BlockSpec((tile, k), lambda j, i: (0, 0), pipeline_mode=fixed) for k in k_sizes]
        x_args += list(xs)
        vmem += sum(bufs * _nbytes((tile, k), BF16) for k in k_sizes)
        if has_res:
            res_specs.append(pl.BlockSpec((tile, tn), lambda j, i, r=row_index: (r(j, i), j)))
            res_args.append(res)
            vmem += 2 * _nbytes((tile, tn), F32)
        out_specs.append(pl.BlockSpec((tile, tn), lambda j, i, r=row_index: (r(j, i), j)))
        out_shape.append(jax.ShapeDtypeStruct((rows, n_cols), out_dtype))
        vmem += 2 * _nbytes((tile, tn), out_dtype) + _nbytes((tile, tn), F32)
        if heads_out:
            assert n_cols == tn == MIX_WIDTH
            out_specs.append(pl.BlockSpec((tile, N_HEADS, HEAD_DIM), lambda j, i, r=row_index: (r(j, i), 0, 0)))
            out_shape.append(jax.ShapeDtypeStruct((rows, N_HEADS, HEAD_DIM), F32))
            vmem += 2 * _nbytes((tile, tn), F32)
    outs = pl.pallas_call(
        functools.partial(_proj_kernel, k_sizes=k_sizes, has_res=has_res, w_transposed=w_transposed,
                          heads_out=heads_out, has_extra=len(row_sets) == 2, out_scale=out_scale),
        grid=(n_j, m // tm),
        in_specs=x_specs + [w_spec] + res_specs,
        out_specs=out_specs,
        out_shape=out_shape,
        scratch_shapes=[pltpu.VMEM(w_block, BF16)] if cast_w else [],
        compiler_params=_params(2, vmem),
        name=name,
    )(*x_args, w, *res_args)
    per_set = 2 if heads_out else 1
    return [tuple(outs[s * per_set:(s + 1) * per_set]) if heads_out else outs[s] for s in range(len(row_sets))]


CUM_CHUNK = 256
HEAD_ROWS = 16


def _tri_incl(n):
    r = lax.broadcasted_iota(jnp.int32, (n, n), 0)
    c = lax.broadcasted_iota(jnp.int32, (n, n), 1)
    return jnp.where(r <= c, 1.0, 0.0).astype(BF16)


def _cumsum_lanes(x, chunk):
    n = x.shape[1]
    tri = _tri_incl(chunk)
    carry = jnp.zeros((x.shape[0], 1), F32)
    outs = []
    for c in range(n // chunk):
        hi, mid, lo = _split3(x[:, c * chunk:(c + 1) * chunk])
        loc = _dot(hi, tri) + _dot(mid, tri) + _dot(lo, tri)
        outs.append(loc + carry)
        carry = carry + loc[:, chunk - 1:chunk]
    return jnp.concatenate(outs, axis=1)


def _forget_kernel(x_ref, w_ref, b_ref, logf_ref, *cum_refs):
    logit = _dot(x_ref[...], w_ref[...]) + b_ref[...]
    logf_t = _log_sigmoid(logit).T[:HEAD_ROWS, :]
    logf_ref[0] = logf_t
    if cum_refs:
        cum_refs[0][0] = _cumsum_lanes(logf_t, CUM_CHUNK)


def forget_gate(xn, wf, bf, n_seq, with_cumsum):
    m, d = xn.shape
    t = m // n_seq
    out_shape = [jax.ShapeDtypeStruct((n_seq, HEAD_ROWS, t), F32)]
    out_specs = [pl.BlockSpec((1, HEAD_ROWS, t), lambda s: (s, 0, 0))]
    if with_cumsum:
        out_shape = out_shape * 2
        out_specs = out_specs * 2
    vmem = 2 * _nbytes((t, d), BF16) + 2 * _nbytes((d, LANES), BF16) + 8 * _nbytes((t, LANES), F32)
    return pl.pallas_call(
        _forget_kernel,
        grid=(n_seq,),
        in_specs=[pl.BlockSpec((t, d), lambda s: (s, 0)),
                  pl.BlockSpec((d, LANES), lambda s: (0, 0)),
                  pl.BlockSpec((1, LANES), lambda s: (0, 0))],
        out_specs=out_specs,
        out_shape=out_shape,
        compiler_params=_params(1, vmem),
        name="forget_gate",
    )(xn, wf, bf)


def _pipelined(steps, stages):
    vals = [None] * len(steps)
    for t in range(len(steps) + len(stages) - 1):
        for j, stage in enumerate(stages):
            i = t - j
            if 0 <= i < len(steps):
                vals[i] = stage(steps[i]) if j == 0 else stage(steps[i], vals[i])


def _fox_prompt_kernel(q_ref, k_ref, v_ref, f_ref, z_ref, o_ref, m_ref, l_ref, acc_ref, *, seq, blk, sup):
    h = pl.program_id(1)
    n_sub = sup // blk

    for qs in range(seq // sup):
        q0 = qs * sup
        m_ref[...] = jnp.full(m_ref.shape, -jnp.inf, F32)
        l_ref[...] = jnp.zeros(l_ref.shape, F32)
        acc_ref[...] = jnp.zeros(acc_ref.shape, F32)

        def logits(step):
            kj, row_lo, diagonal = step
            q = q_ref[0, q0 + row_lo:q0 + sup, :]
            s = _dot_nt(q, k_ref[0, kj * blk:(kj + 1) * blk, :])
            s = s - f_ref[0, pl.ds(h, 1), kj:kj + 1, :].reshape(1, blk) * LOG2E
            if diagonal:
                r = lax.broadcasted_iota(jnp.int32, s.shape, 0)
                c = lax.broadcasted_iota(jnp.int32, s.shape, 1)
                s = jnp.where(c <= r, s, -jnp.inf)
            return s

        def weights(step, s):
            sl = slice(step[1], sup)
            m_old = m_ref[sl, :]
            m_new = jnp.maximum(m_old, jnp.max(s, axis=-1, keepdims=True))
            alpha = jnp.exp2(m_old - m_new)
            p = jnp.exp2(s - _widen(m_new, blk))
            m_ref[sl, :] = m_new
            l_ref[sl, :] = alpha * l_ref[sl, :] + jnp.sum(p, axis=-1, keepdims=True)
            return alpha, p.astype(BF16)

        def accumulate(step, alpha_p):
            kj, row_lo, _ = step
            sl = slice(row_lo, sup)
            alpha, p = alpha_p
            acc_ref[sl, :] = alpha * acc_ref[sl, :] + _dot(p, v_ref[0, kj * blk:(kj + 1) * blk, :])

        steps = [(kj, 0, False) for kj in range(qs * n_sub)]
        steps += [(qs * n_sub + c, c * blk, True) for c in range(n_sub)]
        for step in steps:
            accumulate(step, weights(step, logits(step)))
        gate = _silu(z_ref[0, q0:q0 + sup, :].astype(F32))
        o_ref[0, q0:q0 + sup, :] = (acc_ref[...] / l_ref[...] * gate).astype(o_ref.dtype)


def _tri_after(n):
    r = lax.broadcasted_iota(jnp.int32, (n, n), 0)
    c = lax.broadcasted_iota(jnp.int32, (n, n), 1)
    return jnp.where(r > c, 1.0, 0.0).astype(BF16)


def _sb_block(w, carry, before, tri):
    log_stay, log_take = _sb_logs(w, before)
    a = _sb_weights(log_take, _sb_suffix(log_stay, tri), carry, before)
    return a, carry + jnp.sum(log_stay, axis=-1, keepdims=True)


def _sb_logs(w, before):
    nw = -w
    soft = jnp.log(1.0 + jnp.exp2(jnp.minimum(w, nw))) * LOG2E
    log_stay = jnp.minimum(nw, 0.0) - soft
    log_take = log_stay + w
    if before is not None:
        log_stay = jnp.where(before, log_stay, 0.0)
    return log_stay, log_take


def _sb_suffix(log_stay, tri):
    return _dot(log_stay.astype(BF16), tri)


def _sb_weights(log_take, suffix, carry, before):
    a = jnp.exp2(log_take + (suffix + _widen(carry, suffix.shape[1])))
    if before is not None:
        a = jnp.where(before, a, 0.0)
    return a


def _sb_prompt_kernel(q_ref, k_ref, v_ref, z_ref, o_ref, carry_ref, acc_ref, *, seq, blk, sup):
    n_sub = sup // blk
    tri = _tri_after(blk)

    for qs in range(seq // sup):
        q0 = qs * sup
        carry_ref[...] = jnp.zeros(carry_ref.shape, F32)
        acc_ref[...] = jnp.zeros(acc_ref.shape, F32)

        def mask(step):
            _, row_lo, row_hi, diagonal = step
            if not diagonal:
                return None
            r = lax.broadcasted_iota(jnp.int32, (row_hi - row_lo, blk), 0)
            c = lax.broadcasted_iota(jnp.int32, (row_hi - row_lo, blk), 1)
            return c < r

        def logits(step):
            kj, row_lo, row_hi, _ = step
            q = q_ref[0, q0 + row_lo:q0 + row_hi, :]
            return _dot_nt(q, k_ref[0, kj * blk:(kj + 1) * blk, :])

        def suffix_sums(step, w):
            sl = slice(step[1], step[2])
            log_stay, log_take = _sb_logs(w, mask(step))
            carry = carry_ref[sl, :]
            carry_ref[sl, :] = carry + jnp.sum(log_stay, axis=-1, keepdims=True)
            return log_take, _sb_suffix(log_stay, tri), carry

        def accumulate(step, parts):
            kj, row_lo, row_hi, _ = step
            sl = slice(row_lo, row_hi)
            a = _sb_weights(*parts, mask(step)).astype(BF16)
            acc_ref[sl, :] = acc_ref[sl, :] + _dot(a, v_ref[0, kj * blk:(kj + 1) * blk, :])

        steps = []
        for c in reversed(range(n_sub)):
            steps.append((qs * n_sub + c, c * blk, (c + 1) * blk, True))
            if c + 1 < n_sub:
                steps.append((qs * n_sub + c, (c + 1) * blk, sup, False))
        steps += [(kj, 0, sup, False) for kj in reversed(range(qs * n_sub))]
        _pipelined(steps, [logits, suffix_sums, accumulate])
        gate = _silu(z_ref[0, q0:q0 + sup, :].astype(F32))
        o_ref[0, q0:q0 + sup, :] = (acc_ref[...] * gate).astype(o_ref.dtype)


def prompt_attention(kind, q, k, v, z, f4=None):
    b, s, w = q.shape
    blk = ATTN_BLOCK
    sup = min(ATTN_SUPER, s)
    head_spec = pl.BlockSpec((1, s, HEAD_DIM), lambda bi, hi: (bi, 0, hi))
    in_specs = [head_spec, head_spec, head_spec]
    args = [q, k, v]
    col_state = pltpu.VMEM((sup, LANES), F32)
    acc_state = pltpu.VMEM((sup, HEAD_DIM), F32)
    if kind == "fox":
        in_specs.append(pl.BlockSpec((1, HEAD_ROWS, s // blk, blk), lambda bi, hi: (bi, 0, 0, 0)))
        args.append(f4)
        body = functools.partial(_fox_prompt_kernel, seq=s, blk=blk, sup=sup)
        scratch = [col_state, col_state, acc_state]
    else:
        body = functools.partial(_sb_prompt_kernel, seq=s, blk=blk, sup=sup)
        scratch = [col_state, acc_state]
    in_specs.append(head_spec)
    args.append(z)
    vmem = 2 * 5 * _nbytes((s, HEAD_DIM), BF16) + 2 * _nbytes((HEAD_ROWS, s), F32)
    vmem += 3 * _nbytes((sup, LANES), F32) + 12 * _nbytes((sup, blk), F32)
    return pl.pallas_call(
        body,
        grid=(b, w // HEAD_DIM),
        in_specs=in_specs,
        out_specs=head_spec,
        out_shape=jax.ShapeDtypeStruct((b, s, w), BF16),
        scratch_shapes=scratch,
        compiler_params=_params(2, vmem),
        name=kind + "_prompt_attention",
    )(*args)


def _sample_kernel(*refs, kind, past, t_new, keys):
    if kind == "fox":
        q_ref, kn_ref, vn_ref, kc_ref, vc_ref, lf_ref, z_ref, o_ref, ka_ref, va_ref = refs
    else:
        q_ref, kn_ref, vn_ref, kc_ref, vc_ref, z_ref, o_ref, ka_ref, va_ref = refs
    rows = N_HEADS * t_new
    width = N_HEADS * HEAD_DIM
    pad = keys - past - t_new

    for src_c, src_n, dst in ((kc_ref, kn_ref, ka_ref), (vc_ref, vn_ref, va_ref)):
        dst[0:past, :] = src_c[0].reshape(past, width).astype(BF16)
        dst[past:past + t_new, :] = src_n[0]
        dst[past + t_new:keys, :] = jnp.zeros((pad, width), BF16)

    r_q = lax.broadcasted_iota(jnp.int32, (rows, width), 0)
    c_q = lax.broadcasted_iota(jnp.int32, (rows, width), 1)
    q_tiled = jnp.concatenate([q_ref[0]] * N_HEADS, axis=0)
    same_head = _shift_div(r_q, t_new) == _shift_div(c_q, HEAD_DIM)
    q_bd = jnp.where(same_head, q_tiled, jnp.zeros_like(q_tiled))

    qk = _dot_nt(q_bd, ka_ref[...])
    r_s = lax.broadcasted_iota(jnp.int32, (rows, keys), 0)
    c_s = lax.broadcasted_iota(jnp.int32, (rows, keys), 1)
    q_pos = past + (r_s & (t_new - 1))

    if kind == "fox":
        f_all = _cumsum_lanes(lf_ref[0], LANES)
        f_rows = jnp.concatenate(
            [jnp.broadcast_to(f_all[hh:hh + 1, :], (t_new, keys)) for hh in range(N_HEADS)], axis=0)
        s = jnp.where(c_s <= q_pos, qk - f_rows * LOG2E, -jnp.inf)
        m = jnp.max(s, axis=-1, keepdims=True)
        p = jnp.exp2(s - m)
        wts = p
        norm = jnp.sum(p, axis=-1, keepdims=True)
    else:
        w = qk
        before = c_s < q_pos
        tri = _tri_after(LANES)
        carry = jnp.zeros((rows, LANES), F32)
        chunks = [None] * (keys // LANES)
        for c in reversed(range(keys // LANES)):
            sl = slice(c * LANES, (c + 1) * LANES)
            chunks[c], carry = _sb_block(w[:, sl], carry, before[:, sl], tri)
        wts = jnp.concatenate(chunks, axis=1)
        norm = None

    o_full = _dot(wts.astype(BF16), va_ref[...])
    if norm is not None:
        o_full = o_full / norm
    o = jnp.concatenate(
        [o_full[hh * t_new:(hh + 1) * t_new, hh * HEAD_DIM:(hh + 1) * HEAD_DIM] for hh in range(N_HEADS)],
        axis=1)
    o_ref[0] = (o * _silu(z_ref[0].astype(F32))).astype(o_ref.dtype)


def sample_attention(kind, q, k_new, v_new, cache_k, cache_v, z, logf_all=None):
    b, t_new, w = q.shape
    past = cache_k.shape[1]
    assert t_new & (t_new - 1) == 0
    keys = -(-(past + t_new) // LANES) * LANES
    new_spec = pl.BlockSpec((1, t_new, w), lambda i: (i, 0, 0))
    cache_spec = pl.BlockSpec((1, past, N_HEADS, HEAD_DIM), lambda i: (i, 0, 0, 0))
    in_specs = [new_spec, new_spec, new_spec, cache_spec, cache_spec]
    args = [q, k_new, v_new, cache_k, cache_v]
    scratch = [pltpu.VMEM((keys, w), BF16), pltpu.VMEM((keys, w), BF16)]
    vmem = 4 * _nbytes((past, w), F32) + 2 * _nbytes((keys, w), BF16)
    vmem += 24 * _nbytes((N_HEADS * t_new, keys), F32)
    if kind == "fox":
        in_specs.append(pl.BlockSpec((1, HEAD_ROWS, keys), lambda i: (i, 0, 0)))
        args.append(logf_all)
    in_specs.append(new_spec)
    args.append(z)
    return pl.pallas_call(
        functools.partial(_sample_kernel, kind=kind, past=past, t_new=t_new, keys=keys),
        grid=(b,),
        in_specs=in_specs,
        out_specs=new_spec,
        out_shape=jax.ShapeDtypeStruct((b, t_new, w), BF16),
        scratch_shapes=scratch,
        compiler_params=_params(1, vmem),
        name=kind + "_sample_attention",
    )(*args)


def _causal_mix_weights(w_ref):
    n_groups, chunk, _ = w_ref.shape
    r = lax.broadcasted_iota(jnp.int32, (chunk, chunk), 0)
    c = lax.broadcasted_iota(jnp.int32, (chunk, chunk), 1)
    return [jnp.where(c <= r, w_ref[g], 0.0).astype(BF16) for g in range(n_groups)]


def _layer_norm(v, g_ref, b_ref):
    mu = jnp.mean(v, axis=-1, keepdims=True)
    vc = v - mu
    var = jnp.mean(vc * vc, axis=-1, keepdims=True)
    return vc * lax.rsqrt(var + LN_EPS) * g_ref[...] + b_ref[...]


def _mix_and_gate(w_causal, vnb, bias_ref, u_ref, z_ref, rows, store):
    group_dim = vnb.shape[1] // len(w_causal)
    for g, wg in enumerate(w_causal):
        sl = slice(g * group_dim, (g + 1) * group_dim)
        mixed = _dot(wg, vnb[:, sl]) + bias_ref[:, g:g + 1]
        gate = _silu(z_ref[rows, sl].astype(F32))
        store(sl, (u_ref[rows, sl].astype(F32) * mixed * gate).astype(BF16))


def _sgu_kernel(u_ref, v_ref, z_ref, g_ref, b_ref, w_ref, bias_ref, o_ref, *vn_refs):
    chunk = w_ref.shape[1]
    w_causal = _causal_mix_weights(w_ref)
    for s in range(v_ref.shape[0] // chunk):
        rs = slice(s * chunk, (s + 1) * chunk)
        vn = _layer_norm(v_ref[rs, :].astype(F32), g_ref, b_ref)
        if vn_refs:
            vn_refs[0][rs, :] = vn

        def store(sl, val, rs=rs):
            o_ref[rs, sl] = val

        _mix_and_gate(w_causal, vn.astype(BF16), bias_ref, u_ref, z_ref, rs, store)


SGU_CHUNKS_PER_STEP = 2


def spatial_gate(uvz, ln_g, ln_b, w_mix, bias_t, emit_vn):
    m, c3 = uvz.shape
    c = c3 // 3
    n_groups, chunk, _ = w_mix.shape
    rows = SGU_CHUNKS_PER_STEP * chunk
    assert m % rows == 0
    blk = lambda j: pl.BlockSpec((rows, c), lambda i, j=j: (i, j))
    out_shape = [jax.ShapeDtypeStruct((m, c), BF16)]
    out_specs = [pl.BlockSpec((rows, c), lambda i: (i, 0))]
    if emit_vn:
        out_shape.append(jax.ShapeDtypeStruct((m, c), F32))
        out_specs.append(pl.BlockSpec((rows, c), lambda i: (i, 0)))
    vmem = 6 * _nbytes((rows, c), uvz.dtype) + 2 * _nbytes((rows, c), BF16) + 8 * _nbytes((rows, c), F32)
    vmem += 2 * _nbytes(w_mix.shape, F32)
    return pl.pallas_call(
        _sgu_kernel,
        grid=(m // rows,),
        in_specs=[blk(0), blk(1), blk(2),
                  pl.BlockSpec((1, c), lambda i: (0, 0)),
                  pl.BlockSpec((1, c), lambda i: (0, 0)),
                  pl.BlockSpec((n_groups, chunk, chunk), lambda i: (0, 0, 0)),
                  pl.BlockSpec((chunk, n_groups), lambda i: (0, 0))],
        out_specs=out_specs,
        out_shape=out_shape,
        compiler_params=_params(1, vmem),
        name="spatial_gate",
    )(uvz, uvz, uvz, ln_g.reshape(1, c), ln_b.reshape(1, c), w_mix, bias_t)


def _layer0_inputs(xn_prompt, xn_sample, w_t):
    parts = ([], [])
    kv_out = ([], [])
    for p in range(8):
        is_kv = p in (1, 2, 5, 6)
        outs = project([([xn_prompt], BF16, None), ([xn_sample], BF16, None)], w_t, col0=p * MIX_WIDTH,
                       n_cols=MIX_WIDTH, heads_out=is_kv, w_transposed=True,
                       out_scale=QK_SCALE if p in (0, 4) else None, name="in_proj0")
        for g, out in enumerate(outs):
            if is_kv:
                parts[g].append(out[0])
                kv_out[g].append(out[1])
            else:
                parts[g].append(out)
    return parts, kv_out


def kernel(x_prompt, x_sample, cache_fox_k, cache_fox_v, cache_fox_logf, cache_sb_k, cache_sb_v,
           norm0_g, w_in0, b_forget, w_out0, norm1_g, w_in1, sgu_ln_g, sgu_ln_b, w_sp, b_sp, w_out1, final_g):
    bp, s, d = x_prompt.shape
    bs, t_new, _ = x_sample.shape
    past = cache_fox_k.shape[1]
    hp = x_prompt.reshape(bp * s, d)
    hs = x_sample.reshape(bs * t_new, d)

    n_main = 8 * MIX_WIDTH
    w_t = w_in0.T
    wf = jnp.pad(w_in0[:, n_main:], ((0, 0), (0, LANES - N_HEADS))).astype(BF16)
    bf = jnp.pad(b_forget, (0, LANES - N_HEADS)).reshape(1, LANES).astype(F32)

    xn_p = rmsnorm(hp, norm0_g, BF16)
    xn_s = rmsnorm(hs, norm0_g, BF16)
    (parts_p, parts_s), (kv_prompt, kv_sample) = _layer0_inputs(xn_p, xn_s, w_t)
    fq, fk, fv, fz, sq, sk, sv, sz = parts_p
    gq, gk, gv, gz, tq, tk, tv, tz = parts_s
    logf_t, cum_t = forget_gate(xn_p, wf, bf, bp, True)
    (logf_s,) = forget_gate(xn_s, wf, bf, 1, False)

    as_seq = lambda a: a.reshape(bp, s, MIX_WIDTH)
    f4 = cum_t.reshape(bp, HEAD_ROWS, s // ATTN_BLOCK, ATTN_BLOCK)
    mix_f = prompt_attention("fox", as_seq(fq), as_seq(fk), as_seq(fv), as_seq(fz), f4)
    mix_s = prompt_attention("sb", as_seq(sq), as_seq(sk), as_seq(sv), as_seq(sz))
    fox_logf_prompt = jnp.swapaxes(logf_t[:, :N_HEADS, :], 1, 2)

    as_new = lambda a: a.reshape(bs, t_new, MIX_WIDTH)
    logf_new_t = jnp.swapaxes(logf_s[0, :N_HEADS, :].reshape(N_HEADS, bs, t_new), 0, 1)
    keys = -(-(past + t_new) // LANES) * LANES
    logf_all = jnp.concatenate([jnp.swapaxes(cache_fox_logf.astype(F32), 1, 2), logf_new_t], axis=2)
    logf_all = jnp.pad(logf_all, ((0, 0), (0, HEAD_ROWS - N_HEADS), (0, keys - past - t_new)))
    mix_g = sample_attention("fox", as_new(gq), as_new(gk), as_new(gv), cache_fox_k, cache_fox_v,
                             as_new(gz), logf_all)
    mix_t = sample_attention("sb", as_new(tq), as_new(tk), as_new(tv), cache_sb_k, cache_sb_v, as_new(tz))
    fox_logf_sample = jnp.swapaxes(logf_new_t, 1, 2)
    flat = lambda a: a.reshape(-1, MIX_WIDTH)
    hp, hs = project([([flat(mix_f), flat(mix_s)], F32, hp), ([flat(mix_g), flat(mix_t)], F32, hs)],
                     w_out0.astype(BF16), name="out_proj0")

    c_width = w_in1.shape[1] // 3
    bias_t = b_sp.T.astype(F32)
    uvz_p, uvz_s = project([([rmsnorm(hp, norm1_g, BF16)], BF16, None), ([rmsnorm(hs, norm1_g, BF16)], F32, None)],
                           w_in1, name="in_proj1")
    (act_p,) = spatial_gate(uvz_p, sgu_ln_g, sgu_ln_b, w_sp, bias_t, False)

    per_blk = C_CHUNK // t_new
    eye = jnp.eye(per_blk, dtype=F32)
    w_corner = w_sp[:, :t_new, :t_new]
    w_bd = jnp.einsum("ab,gts->gatbs", eye, w_corner).reshape(C_GROUPS, C_CHUNK, C_CHUNK)
    bias_bd = jnp.tile(b_sp[:, :t_new].T.astype(F32), (per_blk, 1))
    act_s, vn_s = spatial_gate(uvz_s, sgu_ln_g, sgu_ln_b, w_bd, bias_bd, True)
    hp, hs = project([([act_p], F32, hp), ([act_s], F32, hs)], w_out1, tn=512, name="out_proj1")

    y_prompt = rmsnorm(hp, final_g, F32).reshape(bp, s, d)
    y_sample = rmsnorm(hs, final_g, F32).reshape(bs, t_new, d)
    fk4, fv4, sk4, sv4 = (a.reshape(bp, s, N_HEADS, HEAD_DIM) for a in kv_prompt)
    gk4, gv4, tk4, tv4 = (a.reshape(bs, t_new, N_HEADS, HEAD_DIM) for a in kv_sample)
    return (y_prompt, y_sample,
            fk4, fv4, fox_logf_prompt,
            gk4, gv4, fox_logf_sample,
            sk4, sv4,
            tk4, tv4,
            vn_s.reshape(bs, t_new, c_width))
```

```python
import functools

import jax
import jax.numpy as jnp
from jax import lax
from jax.experimental import pallas as pl
from jax.experimental.pallas import tpu as pltpu

F32 = jnp.float32
BF16 = jnp.bfloat16

HEAD_DIM = 128
N_HEADS = 8
MIX_WIDTH = N_HEADS * HEAD_DIM
C_GROUPS = 16
C_CHUNK = 128
RMS_EPS = 1e-6
LN_EPS = 1e-5
ATTN_SCALE = HEAD_DIM ** -0.5
LOG2E = 1.4426950408889634
QK_SCALE = ATTN_SCALE * LOG2E
QZ_PARTS = (0, 3, 4, 7)
KV_PARTS = (1, 2, 5, 6)

LANES = 128
ATTN_BLOCK = 256
FOX_BLOCK = 512
ATTN_SUPER = 1024
V7X_VMEM_BYTES = 64 * 1024 * 1024
VMEM_LIMIT_CAP = V7X_VMEM_BYTES - 8 * 1024 * 1024


def _params(n_grid_axes, vmem_bytes, flags=None):
    limit = min(int(vmem_bytes * 1.25) + (4 << 20), VMEM_LIMIT_CAP)
    return pltpu.CompilerParams(
        dimension_semantics=("arbitrary",) * n_grid_axes,
        vmem_limit_bytes=limit,
        flags=flags,
    )


def _nbytes(shape, dtype):
    n = 1
    for s in shape:
        n *= s
    return n * jnp.dtype(dtype).itemsize


def _dot(a, b):
    return jnp.dot(a, b, preferred_element_type=F32)


def _dot_nt(a, b):
    return lax.dot_general(a, b, (((1,), (1,)), ((), ())), preferred_element_type=F32)


def _log_sigmoid(x):
    return jnp.minimum(x, 0.0) - jnp.log(1.0 + jnp.exp(-jnp.abs(x)))


def _silu(x):
    return x / (1.0 + jnp.exp2(x * -LOG2E))


def _widen(x, n):
    return x if n == LANES else jnp.concatenate([x] * (n // LANES), axis=1)


def _shift_div(x, n):
    assert n & (n - 1) == 0
    return lax.shift_right_logical(x, jnp.full_like(x, n.bit_length() - 1))


def _split3(x):
    hi = x.astype(BF16)
    r1 = x - hi.astype(F32)
    mid = r1.astype(BF16)
    lo = (r1 - mid.astype(F32)).astype(BF16)
    return hi, mid, lo


def _rms_kernel(x_ref, g_ref, o_ref):
    x = x_ref[...]
    ms = jnp.mean(x * x, axis=-1, keepdims=True)
    o_ref[...] = (x * lax.rsqrt(ms + RMS_EPS) * g_ref[...]).astype(o_ref.dtype)


def rmsnorm(x, g, out_dtype, tm=512):
    m, d = x.shape
    tm = min(tm, m)
    vmem = 2 * _nbytes((tm, d), F32) + 2 * _nbytes((tm, d), out_dtype)
    return pl.pallas_call(
        _rms_kernel,
        grid=(m // tm,),
        in_specs=[pl.BlockSpec((tm, d), lambda i: (i, 0)),
                  pl.BlockSpec((1, d), lambda i: (0, 0))],
        out_specs=pl.BlockSpec((tm, d), lambda i: (i, 0)),
        out_shape=jax.ShapeDtypeStruct((m, d), out_dtype),
        compiler_params=_params(1, vmem),
        name="rmsnorm",
    )(x, g.reshape(1, d))


def _proj_kernel(*refs, k_sizes, has_res, w_transposed, heads_out, has_extra, out_scale):
    it = iter(refs)
    n_sets = 2 if has_extra else 1
    x_sets = [[next(it) for _ in k_sizes] for _ in range(n_sets)]
    w_ref = next(it)
    res_refs = [next(it) if has_res else None for _ in range(n_sets)]
    out_sets = [(next(it), next(it) if heads_out else None) for _ in range(n_sets)]
    if w_ref.dtype == BF16:
        wb_ref = w_ref
    else:
        wb_ref = next(it)

        @pl.when(pl.program_id(1) == 0)
        def _():
            wb_ref[...] = w_ref[...].astype(BF16)

    def row_set(x_refs, res_ref, o_ref, o4_ref):
        k0 = 0
        acc = None
        for x_ref, k in zip(x_refs, k_sizes):
            if w_transposed:
                part = _dot_nt(x_ref[...], wb_ref[:, k0:k0 + k])
            else:
                part = _dot(x_ref[...], wb_ref[k0:k0 + k, :])
            acc = part if acc is None else acc + part
            k0 += k
        if has_res:
            acc = acc + res_ref[...]
        scaled = acc if out_scale is None else acc * _pick(out_scale, pl.program_id(0))
        o_ref[...] = scaled.astype(o_ref.dtype)
        if heads_out:
            o4_ref[...] = acc.reshape(acc.shape[0], N_HEADS, HEAD_DIM)

    row_set(x_sets[0], res_refs[0], *out_sets[0])
    if has_extra:
        @pl.when(pl.program_id(1) == pl.num_programs(1) - 1)
        def _():
            row_set(x_sets[1], res_refs[1], *out_sets[1])


def _pick(values, j):
    out = values[0]
    for t in range(1, len(values)):
        out = jnp.where(j >= t, values[t], out)
    return out


def project(row_sets, w, *, col0=0, n_cols=None, col_blocks=None, tn=1024, tm=1024, heads_out=False,
            w_transposed=False, out_scale=None, name="proj"):
    assert 1 <= len(row_sets) <= 2
    xs0, _, res0 = row_sets[0]
    m = xs0[0].shape[0]
    k_sizes = tuple(x.shape[1] for x in xs0)
    k_total = sum(k_sizes)
    k_axis, n_axis = (1, 0) if w_transposed else (0, 1)
    assert k_total == w.shape[k_axis]
    if col_blocks is None:
        n_cols = w.shape[n_axis] - col0 if n_cols is None else n_cols
        assert n_cols % tn == 0 and col0 % tn == 0
        col_blocks = tuple(col0 // tn + j for j in range(n_cols // tn))
    n_j = len(col_blocks)
    n_cols = n_j * tn
    tm = min(tm, m)
    assert m % tm == 0
    has_res = res0 is not None
    if w_transposed:
        w_block, w_index = (tn, k_total), lambda j, i: (_pick(col_blocks, j), 0)
    else:
        w_block, w_index = (k_total, tn), lambda j, i: (0, _pick(col_blocks, j))
    fixed = pl.Buffered(1)
    w_spec = pl.BlockSpec(w_block, w_index, pipeline_mode=fixed) if n_j == 1 else pl.BlockSpec(w_block, w_index)
    cast_w = w.dtype != BF16
    vmem = (1 if n_j == 1 else 2) * _nbytes(w_block, w.dtype) + (_nbytes(w_block, BF16) if cast_w else 0)

    x_specs, x_args, res_specs, res_args, out_specs, out_shape = [], [], [], [], [], []
    for s, (xs, out_dtype, res) in enumerate(row_sets):
        rows = xs[0].shape[0]
        assert tuple(x.shape[1] for x in xs) == k_sizes and (res is not None) == has_res
        if s == 0:
            tile, row_index, bufs = tm, (lambda j, i: i), 2
            x_specs += [pl.BlockSpec((tile, k), lambda j, i: (i, 0)) for k in k_sizes]
        else:
            tile, row_index, bufs = rows, (lambda j, i: 0), 1
            x_specs += [pl.BlockSpec((tile, k), lambda j, i: (0, 0), pipeline_mode=fixed) for k in k_sizes]
        x_args += list(xs)
        vmem += sum(bufs * _nbytes((tile, k), BF16) for k in k_sizes)
        if has_res:
            res_specs.append(pl.BlockSpec((tile, tn), lambda j, i, r=row_index: (r(j, i), j)))
            res_args.append(res)
            vmem += 2 * _nbytes((tile, tn), F32)
        out_specs.append(pl.BlockSpec((tile, tn), lambda j, i, r=row_index: (r(j, i), j)))
        out_shape.append(jax.ShapeDtypeStruct((rows, n_cols), out_dtype))
        vmem += 2 * _nbytes((tile, tn), out_dtype) + _nbytes((tile, tn), F32)
        if heads_out:
            assert n_cols == tn == MIX_WIDTH
            out_specs.append(pl.BlockSpec((tile, N_HEADS, HEAD_DIM), lambda j, i, r=row_index: (r(j, i), 0, 0)))
            out_shape.append(jax.ShapeDtypeStruct((rows, N_HEADS, HEAD_DIM), F32))
            vmem += 2 * _nbytes((tile, tn), F32)
    outs = pl.pallas_call(
        functools.partial(_proj_kernel, k_sizes=k_sizes, has_res=has_res, w_transposed=w_transposed,
                          heads_out=heads_out, has_extra=len(row_sets) == 2, out_scale=out_scale),
        grid=(n_j, m // tm),
        in_specs=x_specs + [w_spec] + res_specs,
        out_specs=out_specs,
        out_shape=out_shape,
        scratch_shapes=[pltpu.VMEM(w_block, BF16)] if cast_w else [],
        compiler_params=_params(2, vmem),
        name=name,
    )(*x_args, w, *res_args)
    per_set = 2 if heads_out else 1
    return [tuple(outs[s * per_set:(s + 1) * per_set]) if heads_out else outs[s] for s in range(len(row_sets))]


CUM_CHUNK = 256
HEAD_ROWS = 16


def _tri_incl(n):
    r = lax.broadcasted_iota(jnp.int32, (n, n), 0)
    c = lax.broadcasted_iota(jnp.int32, (n, n), 1)
    return jnp.where(r <= c, 1.0, 0.0).astype(BF16)


def _cumsum_lanes(x, chunk):
    n = x.shape[1]
    tri = _tri_incl(chunk)
    carry = jnp.zeros((x.shape[0], 1), F32)
    outs = []
    for c in range(n // chunk):
        hi, mid, lo = _split3(x[:, c * chunk:(c + 1) * chunk])
        loc = _dot(hi, tri) + _dot(mid, tri) + _dot(lo, tri)
        outs.append(loc + carry)
        carry = carry + loc[:, chunk - 1:chunk]
    return jnp.concatenate(outs, axis=1)


def _forget_kernel(x_ref, w_ref, b_ref, logf_ref, *cum_refs):
    logit = _dot(x_ref[...], w_ref[...]) + b_ref[...]
    logf_t = _log_sigmoid(logit).T[:HEAD_ROWS, :]
    logf_ref[0] = logf_t
    if cum_refs:
        cum_refs[0][0] = _cumsum_lanes(logf_t, CUM_CHUNK)


def forget_gate(xn, wf, bf, n_seq, with_cumsum):
    m, d = xn.shape
    t = m // n_seq
    out_shape = [jax.ShapeDtypeStruct((n_seq, HEAD_ROWS, t), F32)]
    out_specs = [pl.BlockSpec((1, HEAD_ROWS, t), lambda s: (s, 0, 0))]
    if with_cumsum:
        out_shape = out_shape * 2
        out_specs = out_specs * 2
    vmem = 2 * _nbytes((t, d), BF16) + 2 * _nbytes((d, LANES), BF16) + 8 * _nbytes((t, LANES), F32)
    return pl.pallas_call(
        _forget_kernel,
        grid=(n_seq,),
        in_specs=[pl.BlockSpec((t, d), lambda s: (s, 0)),
                  pl.BlockSpec((d, LANES), lambda s: (0, 0)),
                  pl.BlockSpec((1, LANES), lambda s: (0, 0))],
        out_specs=out_specs,
        out_shape=out_shape,
        compiler_params=_params(1, vmem),
        name="forget_gate",
    )(xn, wf, bf)


def _pipelined(steps, stages):
    vals = [None] * len(steps)
    for t in range(len(steps) + len(stages) - 1):
        for j, stage in enumerate(stages):
            i = t - j
            if 0 <= i < len(steps):
                vals[i] = stage(steps[i]) if j == 0 else stage(steps[i], vals[i])


def _fox_prompt_kernel(q_ref, k_ref, v_ref, f_ref, z_ref, o_ref, m_ref, l_ref, acc_ref, *, seq, blk, sup):
    h = pl.program_id(1)
    n_sub = sup // blk

    for qs in range(seq // sup):
        q0 = qs * sup
        m_ref[...] = jnp.full(m_ref.shape, -jnp.inf, F32)
        l_ref[...] = jnp.zeros(l_ref.shape, F32)
        acc_ref[...] = jnp.zeros(acc_ref.shape, F32)

        def logits(step):
            kj, row_lo, diagonal = step
            q = q_ref[0, q0 + row_lo:q0 + sup, :]
            s = _dot_nt(q, k_ref[0, kj * blk:(kj + 1) * blk, :])
            s = s - f_ref[0, pl.ds(h, 1), kj:kj + 1, :].reshape(1, blk) * LOG2E
            if diagonal:
                r = lax.broadcasted_iota(jnp.int32, s.shape, 0)
                c = lax.broadcasted_iota(jnp.int32, s.shape, 1)
                s = jnp.where(c <= r, s, -jnp.inf)
            return s

        def weights(step, s):
            sl = slice(step[1], sup)
            m_old = m_ref[sl, :]
            m_new = jnp.maximum(m_old, jnp.max(s, axis=-1, keepdims=True))
            alpha = jnp.exp2(m_old - m_new)
            p = jnp.exp2(s - _widen(m_new, blk))
            m_ref[sl, :] = m_new
            l_ref[sl, :] = alpha * l_ref[sl, :] + jnp.sum(p, axis=-1, keepdims=True)
            return alpha, p.astype(BF16)

        def accumulate(step, alpha_p):
            kj, row_lo, _ = step
            sl = slice(row_lo, sup)
            alpha, p = alpha_p
            acc_ref[sl, :] = alpha * acc_ref[sl, :] + _dot(p, v_ref[0, kj * blk:(kj + 1) * blk, :])

        steps = [(kj, 0, False) for kj in range(qs * n_sub)]
        steps += [(qs * n_sub + c, c * blk, True) for c in range(n_sub)]
        _pipelined(steps, [logits, weights, accumulate])
        gate = _silu(z_ref[0, q0:q0 + sup, :].astype(F32))
        o_ref[0, q0:q0 + sup, :] = (acc_ref[...] / l_ref[...] * gate).astype(o_ref.dtype)


def _tri_after(n):
    r = lax.broadcasted_iota(jnp.int32, (n, n), 0)
    c = lax.broadcasted_iota(jnp.int32, (n, n), 1)
    return jnp.where(r > c, 1.0, 0.0).astype(BF16)


def _sb_block(w, carry, before, tri):
    log_stay, log_take = _sb_logs(w, before)
    a = _sb_weights(log_take, _sb_suffix(log_stay, tri), carry, before)
    return a, carry + jnp.sum(log_stay, axis=-1, keepdims=True)


def _sb_logs(w, before):
    nw = -w
    soft = jnp.log(1.0 + jnp.exp2(jnp.minimum(w, nw))) * LOG2E
    log_stay = jnp.minimum(nw, 0.0) - soft
    log_take = log_stay + w
    if before is not None:
        log_stay = jnp.where(before, log_stay, 0.0)
    return log_stay, log_take


def _sb_suffix(log_stay, tri):
    return _dot(log_stay.astype(BF16), tri)


def _sb_weights(log_take, suffix, carry, before):
    a = jnp.exp2(log_take + (suffix + _widen(carry, suffix.shape[1])))
    if before is not None:
        a = jnp.where(before, a, 0.0)
    return a


def _sb_prompt_kernel(q_ref, k_ref, v_ref, z_ref, o_ref, carry_ref, acc_ref, *, seq, blk, sup):
    n_sub = sup // blk
    tri = _tri_after(blk)

    for qs in range(seq // sup):
        q0 = qs * sup
        carry_ref[...] = jnp.zeros(carry_ref.shape, F32)
        acc_ref[...] = jnp.zeros(acc_ref.shape, F32)

        def mask(step):
            _, row_lo, row_hi, diagonal = step
            if not diagonal:
                return None
            r = lax.broadcasted_iota(jnp.int32, (row_hi - row_lo, blk), 0)
            c = lax.broadcasted_iota(jnp.int32, (row_hi - row_lo, blk), 1)
            return c < r

        def logits(step):
            kj, row_lo, row_hi, _ = step
            q = q_ref[0, q0 + row_lo:q0 + row_hi, :]
            return _dot_nt(q, k_ref[0, kj * blk:(kj + 1) * blk, :])

        def suffix_sums(step, w):
            sl = slice(step[1], step[2])
            log_stay, log_take = _sb_logs(w, mask(step))
            carry = carry_ref[sl, :]
            carry_ref[sl, :] = carry + jnp.sum(log_stay, axis=-1, keepdims=True)
            return log_take, _sb_suffix(log_stay, tri), carry

        def accumulate(step, parts):
            kj, row_lo, row_hi, _ = step
            sl = slice(row_lo, row_hi)
            a = _sb_weights(*parts, mask(step)).astype(BF16)
            acc_ref[sl, :] = acc_ref[sl, :] + _dot(a, v_ref[0, kj * blk:(kj + 1) * blk, :])

        steps = []
        for c in reversed(range(n_sub)):
            steps.append((qs * n_sub + c, c * blk, (c + 1) * blk, True))
            if c + 1 < n_sub:
                steps.append((qs * n_sub + c, (c + 1) * blk, sup, False))
        steps += [(kj, 0, sup, False) for kj in reversed(range(qs * n_sub))]
        _pipelined(steps, [logits, suffix_sums, accumulate])
        gate = _silu(z_ref[0, q0:q0 + sup, :].astype(F32))
        o_ref[0, q0:q0 + sup, :] = (acc_ref[...] * gate).astype(o_ref.dtype)


def prompt_attention(kind, q, k, v, z, f4=None):
    b, s, w = k.shape
    heads = w // HEAD_DIM
    blk = f4.shape[3] if kind == "fox" else ATTN_BLOCK
    sup = min(ATTN_SUPER, s)
    head_spec = pl.BlockSpec((1, s, HEAD_DIM), lambda bi, hi: (bi, 0, hi))
    part_spec = lambda part: pl.BlockSpec((1, s, HEAD_DIM), lambda bi, hi: (bi, 0, part * heads + hi))
    in_specs = [part_spec(q[1]), head_spec, head_spec]
    args = [q[0], k, v]
    col_state = pltpu.VMEM((sup, LANES), F32)
    acc_state = pltpu.VMEM((sup, HEAD_DIM), F32)
    if kind == "fox":
        in_specs.append(pl.BlockSpec((1, HEAD_ROWS, s // blk, blk), lambda bi, hi: (bi, 0, 0, 0)))
        args.append(f4)
        body = functools.partial(_fox_prompt_kernel, seq=s, blk=blk, sup=sup)
        scratch = [col_state, col_state, acc_state]
    else:
        body = functools.partial(_sb_prompt_kernel, seq=s, blk=blk, sup=sup)
        scratch = [col_state, acc_state]
    in_specs.append(part_spec(z[1]))
    args.append(z[0])
    vmem = 2 * 5 * _nbytes((s, HEAD_DIM), BF16) + 2 * _nbytes((HEAD_ROWS, s), F32)
    vmem += 3 * _nbytes((sup, LANES), F32) + 12 * _nbytes((sup, blk), F32)
    return pl.pallas_call(
        body,
        grid=(b, w // HEAD_DIM),
        in_specs=in_specs,
        out_specs=head_spec,
        out_shape=jax.ShapeDtypeStruct((b, s, w), BF16),
        scratch_shapes=scratch,
        compiler_params=_params(2, vmem),
        name=kind + "_prompt_attention",
    )(*args)


def _sample_kernel(*refs, past, t_new, keys):
    (fq, fkn, fvn, fkc, fvc, flf, fz, sq, skn, svn, skc, svc, sz,
     fo, so, fka, fva, ska, sva) = refs
    _sample_one("fox", fq, fkn, fvn, fkc, fvc, flf, fz, fo, fka, fva, past=past, t_new=t_new, keys=keys)
    _sample_one("sb", sq, skn, svn, skc, svc, None, sz, so, ska, sva, past=past, t_new=t_new, keys=keys)


def _sample_one(kind, q_ref, kn_ref, vn_ref, kc_ref, vc_ref, lf_ref, z_ref, o_ref, ka_ref, va_ref, *,
                past, t_new, keys):
    rows = N_HEADS * t_new
    width = N_HEADS * HEAD_DIM
    pad = keys - past - t_new

    for src_c, src_n, dst in ((kc_ref, kn_ref, ka_ref), (vc_ref, vn_ref, va_ref)):
        dst[0:past, :] = src_c[0].reshape(past, width).astype(BF16)
        dst[past:past + t_new, :] = src_n[0]
        dst[past + t_new:keys, :] = jnp.zeros((pad, width), BF16)

    r_q = lax.broadcasted_iota(jnp.int32, (rows, width), 0)
    c_q = lax.broadcasted_iota(jnp.int32, (rows, width), 1)
    q_tiled = jnp.concatenate([q_ref[0]] * N_HEADS, axis=0)
    same_head = _shift_div(r_q, t_new) == _shift_div(c_q, HEAD_DIM)
    q_bd = jnp.where(same_head, q_tiled, jnp.zeros_like(q_tiled))

    qk = _dot_nt(q_bd, ka_ref[...])
    r_s = lax.broadcasted_iota(jnp.int32, (rows, keys), 0)
    c_s = lax.broadcasted_iota(jnp.int32, (rows, keys), 1)
    q_pos = past + (r_s & (t_new - 1))

    if kind == "fox":
        f_all = _cumsum_lanes(lf_ref[0], LANES)
        f_rows = jnp.concatenate(
            [jnp.broadcast_to(f_all[hh:hh + 1, :], (t_new, keys)) for hh in range(N_HEADS)], axis=0)
        s = jnp.where(c_s <= q_pos, qk - f_rows * LOG2E, -jnp.inf)
        m = jnp.max(s, axis=-1, keepdims=True)
        p = jnp.exp2(s - m)
        wts = p
        norm = jnp.sum(p, axis=-1, keepdims=True)
    else:
        w = qk
        before = c_s < q_pos
        tri = _tri_after(LANES)
        carry = jnp.zeros((rows, LANES), F32)
        chunks = [None] * (keys // LANES)
        for c in reversed(range(keys // LANES)):
            sl = slice(c * LANES, (c + 1) * LANES)
            chunks[c], carry = _sb_block(w[:, sl], carry, before[:, sl], tri)
        wts = jnp.concatenate(chunks, axis=1)
        norm = None

    o_full = _dot(wts.astype(BF16), va_ref[...])
    if norm is not None:
        o_full = o_full / norm
    o = jnp.concatenate(
        [o_full[hh * t_new:(hh + 1) * t_new, hh * HEAD_DIM:(hh + 1) * HEAD_DIM] for hh in range(N_HEADS)],
        axis=1)
    o_ref[0] = (o * _silu(z_ref[0].astype(F32))).astype(o_ref.dtype)


def sample_attention(fox, sb, logf_all):
    b, t_new, w = fox[1].shape
    past = fox[3].shape[1]
    assert t_new & (t_new - 1) == 0
    keys = -(-(past + t_new) // LANES) * LANES
    new_spec = pl.BlockSpec((1, t_new, w), lambda i: (i, 0, 0))
    part_spec = lambda part: pl.BlockSpec((1, t_new, w), lambda i: (i, 0, part))
    cache_spec = pl.BlockSpec((1, past, N_HEADS, HEAD_DIM), lambda i: (i, 0, 0, 0))
    logf_spec = pl.BlockSpec((1, HEAD_ROWS, keys), lambda i: (i, 0, 0))
    mixer_specs = lambda m: [part_spec(m[0][1]), new_spec, new_spec, cache_spec, cache_spec]
    vmem = 8 * _nbytes((past, w), F32) + 4 * _nbytes((keys, w), BF16)
    vmem += 24 * _nbytes((N_HEADS * t_new, keys), F32)
    return pl.pallas_call(
        functools.partial(_sample_kernel, past=past, t_new=t_new, keys=keys),
        grid=(b,),
        in_specs=(mixer_specs(fox) + [logf_spec, part_spec(fox[5][1])]
                  + mixer_specs(sb) + [part_spec(sb[5][1])]),
        out_specs=[new_spec, new_spec],
        out_shape=[jax.ShapeDtypeStruct((b, t_new, w), BF16)] * 2,
        scratch_shapes=[pltpu.VMEM((keys, w), BF16)] * 4,
        compiler_params=_params(1, vmem),
        name="sample_attention",
    )(fox[0][0], *fox[1:5], logf_all, fox[5][0], sb[0][0], *sb[1:5], sb[5][0])


def _causal_mix_weights(w_ref):
    n_groups, chunk, _ = w_ref.shape
    r = lax.broadcasted_iota(jnp.int32, (chunk, chunk), 0)
    c = lax.broadcasted_iota(jnp.int32, (chunk, chunk), 1)
    return [jnp.where(c <= r, w_ref[g], 0.0).astype(BF16) for g in range(n_groups)]


def _layer_norm(v, g_ref, b_ref):
    mu = jnp.mean(v, axis=-1, keepdims=True)
    vc = v - mu
    var = jnp.mean(vc * vc, axis=-1, keepdims=True)
    return vc * lax.rsqrt(var + LN_EPS) * g_ref[...] + b_ref[...]


def _mix_and_gate(w_causal, vnb, bias_ref, u_ref, z_ref, rows, store):
    group_dim = vnb.shape[1] // len(w_causal)
    for g, wg in enumerate(w_causal):
        sl = slice(g * group_dim, (g + 1) * group_dim)
        mixed = _dot(wg, vnb[:, sl]) + bias_ref[:, g:g + 1]
        gate = _silu(z_ref[rows, sl].astype(F32))
        store(sl, (u_ref[rows, sl].astype(F32) * mixed * gate).astype(BF16))


def _sgu_kernel(u_ref, v_ref, z_ref, g_ref, b_ref, w_ref, bias_ref, o_ref, *vn_refs):
    chunk = w_ref.shape[1]
    w_causal = _causal_mix_weights(w_ref)
    for s in range(v_ref.shape[0] // chunk):
        rs = slice(s * chunk, (s + 1) * chunk)
        vn = _layer_norm(v_ref[rs, :].astype(F32), g_ref, b_ref)
        if vn_refs:
            vn_refs[0][rs, :] = vn

        def store(sl, val, rs=rs):
            o_ref[rs, sl] = val

        _mix_and_gate(w_causal, vn.astype(BF16), bias_ref, u_ref, z_ref, rs, store)


SGU_CHUNKS_PER_STEP = 2


def spatial_gate(uvz, ln_g, ln_b, w_mix, bias_t, emit_vn):
    m, c3 = uvz.shape
    c = c3 // 3
    n_groups, chunk, _ = w_mix.shape
    rows = SGU_CHUNKS_PER_STEP * chunk
    assert m % rows == 0
    blk = lambda j: pl.BlockSpec((rows, c), lambda i, j=j: (i, j))
    out_shape = [jax.ShapeDtypeStruct((m, c), BF16)]
    out_specs = [pl.BlockSpec((rows, c), lambda i: (i, 0))]
    if emit_vn:
        out_shape.append(jax.ShapeDtypeStruct((m, c), F32))
        out_specs.append(pl.BlockSpec((rows, c), lambda i: (i, 0)))
    vmem = 6 * _nbytes((rows, c), uvz.dtype) + 2 * _nbytes((rows, c), BF16) + 8 * _nbytes((rows, c), F32)
    vmem += 2 * _nbytes(w_mix.shape, F32)
    return pl.pallas_call(
        _sgu_kernel,
        grid=(m // rows,),
        in_specs=[blk(0), blk(1), blk(2),
                  pl.BlockSpec((1, c), lambda i: (0, 0)),
                  pl.BlockSpec((1, c), lambda i: (0, 0)),
                  pl.BlockSpec((n_groups, chunk, chunk), lambda i: (0, 0, 0)),
                  pl.BlockSpec((chunk, n_groups), lambda i: (0, 0))],
        out_specs=out_specs,
        out_shape=out_shape,
        compiler_params=_params(1, vmem),
        name="spatial_gate",
    )(uvz, uvz, uvz, ln_g.reshape(1, c), ln_b.reshape(1, c), w_mix, bias_t)


def _layer0_inputs(xn_prompt, xn_sample, w_t):
    row_sets = [([xn_prompt], BF16, None), ([xn_sample], BF16, None)]
    qz = project(row_sets, w_t, col_blocks=QZ_PARTS, w_transposed=True,
                 out_scale=(QK_SCALE, 1.0, QK_SCALE, 1.0), name="in_proj0_qz")
    kv, kv_out = ([], []), ([], [])
    for p in KV_PARTS:
        outs = project(row_sets, w_t, col_blocks=(p,), heads_out=True, w_transposed=True, name="in_proj0_kv")
        for g, out in enumerate(outs):
            kv[g].append(out[0])
            kv_out[g].append(out[1])
    return qz, kv, kv_out


def kernel(x_prompt, x_sample, cache_fox_k, cache_fox_v, cache_fox_logf, cache_sb_k, cache_sb_v,
           norm0_g, w_in0, b_forget, w_out0, norm1_g, w_in1, sgu_ln_g, sgu_ln_b, w_sp, b_sp, w_out1, final_g):
    bp, s, d = x_prompt.shape
    bs, t_new, _ = x_sample.shape
    past = cache_fox_k.shape[1]
    hp = x_prompt.reshape(bp * s, d)
    hs = x_sample.reshape(bs * t_new, d)

    n_main = 8 * MIX_WIDTH
    w_t = w_in0.T
    wf = jnp.pad(w_in0[:, n_main:], ((0, 0), (0, LANES - N_HEADS))).astype(BF16)
    bf = jnp.pad(b_forget, (0, LANES - N_HEADS)).reshape(1, LANES).astype(F32)

    xn_p = rmsnorm(hp, norm0_g, BF16)
    xn_s = rmsnorm(hs, norm0_g, BF16)
    (qz_p, qz_s), ((fk, fv, sk, sv), (gk, gv, tk, tv)), (kv_prompt, kv_sample) = _layer0_inputs(xn_p, xn_s, w_t)
    logf_t, cum_t = forget_gate(xn_p, wf, bf, bp, True)
    (logf_s,) = forget_gate(xn_s, wf, bf, 1, False)

    as_seq = lambda a: a.reshape(bp, s, -1)
    qz_p = as_seq(qz_p)
    f4 = cum_t.reshape(bp, HEAD_ROWS, s // FOX_BLOCK, FOX_BLOCK)
    mix_f = prompt_attention("fox", (qz_p, 0), as_seq(fk), as_seq(fv), (qz_p, 1), f4)
    mix_s = prompt_attention("sb", (qz_p, 2), as_seq(sk), as_seq(sv), (qz_p, 3))
    fox_logf_prompt = jnp.swapaxes(logf_t[:, :N_HEADS, :], 1, 2)

    as_new = lambda a: a.reshape(bs, t_new, -1)
    qz_s = as_new(qz_s)
    logf_new_t = jnp.swapaxes(logf_s[0, :N_HEADS, :].reshape(N_HEADS, bs, t_new), 0, 1)
    keys = -(-(past + t_new) // LANES) * LANES
    logf_all = jnp.concatenate([jnp.swapaxes(cache_fox_logf.astype(F32), 1, 2), logf_new_t], axis=2)
    logf_all = jnp.pad(logf_all, ((0, 0), (0, HEAD_ROWS - N_HEADS), (0, keys - past - t_new)))
    mix_g, mix_t = sample_attention(
        ((qz_s, 0), as_new(gk), as_new(gv), cache_fox_k, cache_fox_v, (qz_s, 1)),
        ((qz_s, 2), as_new(tk), as_new(tv), cache_sb_k, cache_sb_v, (qz_s, 3)), logf_all)
    fox_logf_sample = jnp.swapaxes(logf_new_t, 1, 2)
    flat = lambda a: a.reshape(-1, MIX_WIDTH)
    hp, hs = project([([flat(mix_f), flat(mix_s)], F32, hp), ([flat(mix_g), flat(mix_t)], F32, hs)],
                     w_out0.astype(BF16), name="out_proj0")

    c_width = w_in1.shape[1] // 3
    bias_t = b_sp.T.astype(F32)
    uvz_p, uvz_s = project([([rmsnorm(hp, norm1_g, BF16)], BF16, None), ([rmsnorm(hs, norm1_g, BF16)], F32, None)],
                           w_in1, name="in_proj1")
    (act_p,) = spatial_gate(uvz_p, sgu_ln_g, sgu_ln_b, w_sp, bias_t, False)

    per_blk = C_CHUNK // t_new
    eye = jnp.eye(per_blk, dtype=F32)
    w_corner = w_sp[:, :t_new, :t_new]
    w_bd = jnp.einsum("ab,gts->gatbs", eye, w_corner).reshape(C_GROUPS, C_CHUNK, C_CHUNK)
    bias_bd = jnp.tile(b_sp[:, :t_new].T.astype(F32), (per_blk, 1))
    act_s, vn_s = spatial_gate(uvz_s, sgu_ln_g, sgu_ln_b, w_bd, bias_bd, True)
    hp, hs = project([([act_p], F32, hp), ([act_s], F32, hs)], w_out1, tn=512, name="out_proj1")

    y_prompt = rmsnorm(hp, final_g, F32).reshape(bp, s, d)
    y_sample = rmsnorm(hs, final_g, F32).reshape(bs, t_new, d)
    fk4, fv4, sk4, sv4 = (a.reshape(bp, s, N_HEADS, HEAD_DIM) for a in kv_prompt)
    gk4, gv4, tk4, tv4 = (a.reshape(bs, t_new, N_HEADS, HEAD_DIM) for a in kv_sample)
    return (y_prompt, y_sample,
            fk4, fv4, fox_logf_prompt,
            gk4, gv4, fox_logf_sample,
            sk4, sv4,
            tk4, tv4,
            vn_s.reshape(bs, t_new, c_width))
```

```python
import functools

import jax
import jax.numpy as jnp
from jax import lax
from jax.experimental import pallas as pl
from jax.experimental.pallas import tpu as pltpu

F32 = jnp.float32
BF16 = jnp.bfloat16

HEAD_DIM = 128
N_HEADS = 8
MIX_WIDTH = N_HEADS * HEAD_DIM
C_GROUPS = 16
C_CHUNK = 128
RMS_EPS = 1e-6
LN_EPS = 1e-5
ATTN_SCALE = HEAD_DIM ** -0.5
LOG2E = 1.4426950408889634
QK_SCALE = ATTN_SCALE * LOG2E
QZ_PARTS = (0, 3, 4, 7)
KV_PARTS = (1, 2, 5, 6)

LANES = 128
ATTN_BLOCK = 256
FOX_BLOCK = 512
ATTN_SUPER = 1024
V7X_VMEM_BYTES = 64 * 1024 * 1024
VMEM_LIMIT_CAP = V7X_VMEM_BYTES - 8 * 1024 * 1024
OUTPUT_TO_HBM = VMEM_LIMIT_CAP


def _params(n_grid_axes, vmem_bytes, flags=None):
    limit = min(int(vmem_bytes * 1.25) + (4 << 20), VMEM_LIMIT_CAP)
    return pltpu.CompilerParams(
        dimension_semantics=("arbitrary",) * n_grid_axes,
        vmem_limit_bytes=limit,
        flags=flags,
    )


def _nbytes(shape, dtype):
    n = 1
    for s in shape:
        n *= s
    return n * jnp.dtype(dtype).itemsize


def _dot(a, b):
    return jnp.dot(a, b, preferred_element_type=F32)


def _dot_nt(a, b):
    return lax.dot_general(a, b, (((1,), (1,)), ((), ())), preferred_element_type=F32)


def _log_sigmoid(x):
    return jnp.minimum(x, 0.0) - jnp.log(1.0 + jnp.exp(-jnp.abs(x)))


def _silu(x):
    return x / (1.0 + jnp.exp2(x * -LOG2E))


def _widen(x, n):
    return x if n == LANES else jnp.concatenate([x] * (n // LANES), axis=1)


def _shift_div(x, n):
    assert n & (n - 1) == 0
    return lax.shift_right_logical(x, jnp.full_like(x, n.bit_length() - 1))


def _split3(x):
    hi = x.astype(BF16)
    r1 = x - hi.astype(F32)
    mid = r1.astype(BF16)
    lo = (r1 - mid.astype(F32)).astype(BF16)
    return hi, mid, lo


def _rms_kernel(x_ref, g_ref, o_ref):
    x = x_ref[...]
    ms = jnp.mean(x * x, axis=-1, keepdims=True)
    o_ref[...] = (x * lax.rsqrt(ms + RMS_EPS) * g_ref[...]).astype(o_ref.dtype)


def rmsnorm(x, g, out_dtype, tm=512):
    m, d = x.shape
    tm = min(tm, m)
    return pl.pallas_call(
        _rms_kernel,
        grid=(m // tm,),
        in_specs=[pl.BlockSpec((tm, d), lambda i: (i, 0)),
                  pl.BlockSpec((1, d), lambda i: (0, 0))],
        out_specs=pl.BlockSpec((tm, d), lambda i: (i, 0)),
        out_shape=jax.ShapeDtypeStruct((m, d), out_dtype),
        compiler_params=_params(1, OUTPUT_TO_HBM),
        name="rmsnorm",
    )(x, g.reshape(1, d))


def _proj_kernel(*refs, k_sizes, has_res, w_transposed, heads_out, has_extra, out_scale):
    it = iter(refs)
    n_sets = 2 if has_extra else 1
    x_sets = [[next(it) for _ in k_sizes] for _ in range(n_sets)]
    w_ref = next(it)
    res_refs = [next(it) if has_res else None for _ in range(n_sets)]
    out_sets = [(next(it), next(it) if heads_out else None) for _ in range(n_sets)]
    if w_ref.dtype == BF16:
        wb_ref = w_ref
    else:
        wb_ref = next(it)

        @pl.when(pl.program_id(1) == 0)
        def _():
            wb_ref[...] = w_ref[...].astype(BF16)

    def row_set(x_refs, res_ref, o_ref, o4_ref):
        k0 = 0
        acc = None
        for x_ref, k in zip(x_refs, k_sizes):
            if w_transposed:
                part = _dot_nt(x_ref[...], wb_ref[:, k0:k0 + k])
            else:
                part = _dot(x_ref[...], wb_ref[k0:k0 + k, :])
            acc = part if acc is None else acc + part
            k0 += k
        if has_res:
            acc = acc + res_ref[...]
        scaled = acc if out_scale is None else acc * _pick(out_scale, pl.program_id(0))
        o_ref[...] = scaled.astype(o_ref.dtype)
        if heads_out:
            o4_ref[...] = acc.reshape(acc.shape[0], N_HEADS, HEAD_DIM)

    row_set(x_sets[0], res_refs[0], *out_sets[0])
    if has_extra:
        @pl.when(pl.program_id(1) == pl.num_programs(1) - 1)
        def _():
            row_set(x_sets[1], res_refs[1], *out_sets[1])


def _pick(values, j):
    out = values[0]
    for t in range(1, len(values)):
        out = jnp.where(j >= t, values[t], out)
    return out


def project(row_sets, w, *, col0=0, n_cols=None, col_blocks=None, tn=1024, tm=1024, heads_out=False,
            w_transposed=False, out_scale=None, name="proj"):
    assert 1 <= len(row_sets) <= 2
    xs0, _, res0 = row_sets[0]
    m = xs0[0].shape[0]
    k_sizes = tuple(x.shape[1] for x in xs0)
    k_total = sum(k_sizes)
    k_axis, n_axis = (1, 0) if w_transposed else (0, 1)
    assert k_total == w.shape[k_axis]
    if col_blocks is None:
        n_cols = w.shape[n_axis] - col0 if n_cols is None else n_cols
        assert n_cols % tn == 0 and col0 % tn == 0
        col_blocks = tuple(col0 // tn + j for j in range(n_cols // tn))
    n_j = len(col_blocks)
    n_cols = n_j * tn
    tm = min(tm, m)
    assert m % tm == 0
    has_res = res0 is not None
    if w_transposed:
        w_block, w_index = (tn, k_total), lambda j, i: (_pick(col_blocks, j), 0)
    else:
        w_block, w_index = (k_total, tn), lambda j, i: (0, _pick(col_blocks, j))
    fixed = pl.Buffered(1)
    w_spec = pl.BlockSpec(w_block, w_index, pipeline_mode=fixed) if n_j == 1 else pl.BlockSpec(w_block, w_index)
    cast_w = w.dtype != BF16
    vmem = (1 if n_j == 1 else 2) * _nbytes(w_block, w.dtype) + (_nbytes(w_block, BF16) if cast_w else 0)

    x_specs, x_args, res_specs, res_args, out_specs, out_shape = [], [], [], [], [], []
    for s, (xs, out_dtype, res) in enumerate(row_sets):
        rows = xs[0].shape[0]
        assert tuple(x.shape[1] for x in xs) == k_sizes and (res is not None) == has_res
        if s == 0:
            tile, row_index, bufs = tm, (lambda j, i: i), 2
            x_specs += [pl.BlockSpec((tile, k), lambda j, i: (i, 0)) for k in k_sizes]
        else:
            tile, row_index, bufs = rows, (lambda j, i: 0), 1
            x_specs += [pl.BlockSpec((tile, k), lambda j, i: (0, 0), pipeline_mode=fixed) for k in k_sizes]
        x_args += list(xs)
        vmem += sum(bufs * _nbytes((tile, k), BF16) for k in k_sizes)
        if has_res:
            res_specs.append(pl.BlockSpec((tile, tn), lambda j, i, r=row_index: (r(j, i), j)))
            res_args.append(res)
            vmem += 2 * _nbytes((tile, tn), F32)
        out_specs.append(pl.BlockSpec((tile, tn), lambda j, i, r=row_index: (r(j, i), j)))
        out_shape.append(jax.ShapeDtypeStruct((rows, n_cols), out_dtype))
        vmem += 2 * _nbytes((tile, tn), out_dtype) + _nbytes((tile, tn), F32)
        if heads_out:
            assert n_cols == tn == MIX_WIDTH
            out_specs.append(pl.BlockSpec((tile, N_HEADS, HEAD_DIM), lambda j, i, r=row_index: (r(j, i), 0, 0)))
            out_shape.append(jax.ShapeDtypeStruct((rows, N_HEADS, HEAD_DIM), F32))
            vmem += 2 * _nbytes((tile, tn), F32)
    outs = pl.pallas_call(
        functools.partial(_proj_kernel, k_sizes=k_sizes, has_res=has_res, w_transposed=w_transposed,
                          heads_out=heads_out, has_extra=len(row_sets) == 2, out_scale=out_scale),
        grid=(n_j, m // tm),
        in_specs=x_specs + [w_spec] + res_specs,
        out_specs=out_specs,
        out_shape=out_shape,
        scratch_shapes=[pltpu.VMEM(w_block, BF16)] if cast_w else [],
        compiler_params=_params(2, vmem),
        name=name,
    )(*x_args, w, *res_args)
    per_set = 2 if heads_out else 1
    return [tuple(outs[s * per_set:(s + 1) * per_set]) if heads_out else outs[s] for s in range(len(row_sets))]


CUM_CHUNK = 256
HEAD_ROWS = 16


def _tri_incl(n):
    r = lax.broadcasted_iota(jnp.int32, (n, n), 0)
    c = lax.broadcasted_iota(jnp.int32, (n, n), 1)
    return jnp.where(r <= c, 1.0, 0.0).astype(BF16)


def _cumsum_lanes(x, chunk):
    n = x.shape[1]
    tri = _tri_incl(chunk)
    carry = jnp.zeros((x.shape[0], 1), F32)
    outs = []
    for c in range(n // chunk):
        hi, mid, lo = _split3(x[:, c * chunk:(c + 1) * chunk])
        loc = _dot(hi, tri) + _dot(mid, tri) + _dot(lo, tri)
        outs.append(loc + carry)
        carry = carry + loc[:, chunk - 1:chunk]
    return jnp.concatenate(outs, axis=1)


def _forget_kernel(x_ref, w_ref, b_ref, logf_ref, *cum_refs):
    logit = _dot(x_ref[...], w_ref[...]) + b_ref[...]
    logf_t = _log_sigmoid(logit).T[:HEAD_ROWS, :]
    logf_ref[0] = logf_t
    if cum_refs:
        cum_refs[0][0] = _cumsum_lanes(logf_t, CUM_CHUNK)


def forget_gate(xn, wf, bf, n_seq, with_cumsum):
    m, d = xn.shape
    t = m // n_seq
    out_shape = [jax.ShapeDtypeStruct((n_seq, HEAD_ROWS, t), F32)]
    out_specs = [pl.BlockSpec((1, HEAD_ROWS, t), lambda s: (s, 0, 0))]
    if with_cumsum:
        out_shape = out_shape * 2
        out_specs = out_specs * 2
    vmem = 2 * _nbytes((t, d), BF16) + 2 * _nbytes((d, LANES), BF16) + 8 * _nbytes((t, LANES), F32)
    return pl.pallas_call(
        _forget_kernel,
        grid=(n_seq,),
        in_specs=[pl.BlockSpec((t, d), lambda s: (s, 0)),
                  pl.BlockSpec((d, LANES), lambda s: (0, 0)),
                  pl.BlockSpec((1, LANES), lambda s: (0, 0))],
        out_specs=out_specs,
        out_shape=out_shape,
        compiler_params=_params(1, vmem),
        name="forget_gate",
    )(xn, wf, bf)


def _pipelined(steps, stages):
    vals = [None] * len(steps)
    for t in range(len(steps) + len(stages) - 1):
        for j, stage in enumerate(stages):
            i = t - j
            if 0 <= i < len(steps):
                vals[i] = stage(steps[i]) if j == 0 else stage(steps[i], vals[i])


def _fox_prompt_kernel(q_ref, k_ref, v_ref, f_ref, z_ref, o_ref, m_ref, l_ref, acc_ref, *, seq, blk, sup):
    h = pl.program_id(1)
    n_sub = sup // blk

    for qs in range(seq // sup):
        q0 = qs * sup
        m_ref[...] = jnp.full(m_ref.shape, -jnp.inf, F32)
        l_ref[...] = jnp.zeros(l_ref.shape, F32)
        acc_ref[...] = jnp.zeros(acc_ref.shape, F32)

        def logits(step):
            kj, row_lo, diagonal = step
            q = q_ref[0, q0 + row_lo:q0 + sup, :]
            s = _dot_nt(q, k_ref[0, kj * blk:(kj + 1) * blk, :])
            s = s - f_ref[0, pl.ds(h, 1), kj:kj + 1, :].reshape(1, blk) * LOG2E
            if diagonal:
                r = lax.broadcasted_iota(jnp.int32, s.shape, 0)
                c = lax.broadcasted_iota(jnp.int32, s.shape, 1)
                s = jnp.where(c <= r, s, -jnp.inf)
            return s

        def weights(step, s):
            sl = slice(step[1], sup)
            m_old = m_ref[sl, :]
            m_new = jnp.maximum(m_old, jnp.max(s, axis=-1, keepdims=True))
            alpha = jnp.exp2(m_old - m_new)
            p = jnp.exp2(s - _widen(m_new, blk))
            m_ref[sl, :] = m_new
            l_ref[sl, :] = alpha * l_ref[sl, :] + jnp.sum(p, axis=-1, keepdims=True)
            return alpha, p.astype(BF16)

        def accumulate(step, alpha_p):
            kj, row_lo, _ = step
            sl = slice(row_lo, sup)
            alpha, p = alpha_p
            acc_ref[sl, :] = alpha * acc_ref[sl, :] + _dot(p, v_ref[0, kj * blk:(kj + 1) * blk, :])

        steps = [(kj, 0, False) for kj in range(qs * n_sub)]
        steps += [(qs * n_sub + c, c * blk, True) for c in range(n_sub)]
        _pipelined(steps, [logits, weights, accumulate])
        gate = _silu(z_ref[0, q0:q0 + sup, :].astype(F32))
        o_ref[0, q0:q0 + sup, :] = (acc_ref[...] / l_ref[...] * gate).astype(o_ref.dtype)


def _tri_after(n):
    r = lax.broadcasted_iota(jnp.int32, (n, n), 0)
    c = lax.broadcasted_iota(jnp.int32, (n, n), 1)
    return jnp.where(r > c, 1.0, 0.0).astype(BF16)


def _sb_block(w, carry, before, tri):
    log_stay, log_take = _sb_logs(w, before)
    a = _sb_weights(log_take, _sb_suffix(log_stay, tri), carry, before)
    return a, carry + jnp.sum(log_stay, axis=-1, keepdims=True)


def _sb_logs(w, before):
    nw = -w
    soft = jnp.log(1.0 + jnp.exp2(jnp.minimum(w, nw))) * LOG2E
    log_stay = jnp.minimum(nw, 0.0) - soft
    log_take = log_stay + w
    if before is not None:
        log_stay = jnp.where(before, log_stay, 0.0)
    return log_stay, log_take


def _sb_suffix(log_stay, tri):
    return _dot(log_stay.astype(BF16), tri)


def _sb_weights(log_take, suffix, carry, before):
    a = jnp.exp2(log_take + (suffix + _widen(carry, suffix.shape[1])))
    if before is not None:
        a = jnp.where(before, a, 0.0)
    return a


def _sb_prompt_kernel(q_ref, k_ref, v_ref, z_ref, o_ref, carry_ref, acc_ref, *, seq, blk, sup):
    n_sub = sup // blk
    tri = _tri_after(blk)

    for qs in range(seq // sup):
        q0 = qs * sup
        carry_ref[...] = jnp.zeros(carry_ref.shape, F32)
        acc_ref[...] = jnp.zeros(acc_ref.shape, F32)

        def mask(step):
            _, row_lo, row_hi, diagonal = step
            if not diagonal:
                return None
            r = lax.broadcasted_iota(jnp.int32, (row_hi - row_lo, blk), 0)
            c = lax.broadcasted_iota(jnp.int32, (row_hi - row_lo, blk), 1)
            return c < r

        def logits(step):
            kj, row_lo, row_hi, _ = step
            q = q_ref[0, q0 + row_lo:q0 + row_hi, :]
            return _dot_nt(q, k_ref[0, kj * blk:(kj + 1) * blk, :])

        def suffix_sums(step, w):
            sl = slice(step[1], step[2])
            log_stay, log_take = _sb_logs(w, mask(step))
            carry = carry_ref[sl, :]
            carry_ref[sl, :] = carry + jnp.sum(log_stay, axis=-1, keepdims=True)
            return log_take, _sb_suffix(log_stay, tri), carry

        def accumulate(step, parts):
            kj, row_lo, row_hi, _ = step
            sl = slice(row_lo, row_hi)
            a = _sb_weights(*parts, mask(step)).astype(BF16)
            acc_ref[sl, :] = acc_ref[sl, :] + _dot(a, v_ref[0, kj * blk:(kj + 1) * blk, :])

        steps = []
        for c in reversed(range(n_sub)):
            steps.append((qs * n_sub + c, c * blk, (c + 1) * blk, True))
            if c + 1 < n_sub:
                steps.append((qs * n_sub + c, (c + 1) * blk, sup, False))
        steps += [(kj, 0, sup, False) for kj in reversed(range(qs * n_sub))]
        _pipelined(steps, [logits, suffix_sums, accumulate])
        gate = _silu(z_ref[0, q0:q0 + sup, :].astype(F32))
        o_ref[0, q0:q0 + sup, :] = (acc_ref[...] * gate).astype(o_ref.dtype)


def prompt_attention(kind, q, k, v, z, f4=None):
    b, s, w = k.shape
    heads = w // HEAD_DIM
    blk = f4.shape[3] if kind == "fox" else ATTN_BLOCK
    sup = min(ATTN_SUPER, s)
    head_spec = pl.BlockSpec((1, s, HEAD_DIM), lambda bi, hi: (bi, 0, hi))
    part_spec = lambda part: pl.BlockSpec((1, s, HEAD_DIM), lambda bi, hi: (bi, 0, part * heads + hi))
    in_specs = [part_spec(q[1]), head_spec, head_spec]
    args = [q[0], k, v]
    col_state = pltpu.VMEM((sup, LANES), F32)
    acc_state = pltpu.VMEM((sup, HEAD_DIM), F32)
    if kind == "fox":
        in_specs.append(pl.BlockSpec((1, HEAD_ROWS, s // blk, blk), lambda bi, hi: (bi, 0, 0, 0)))
        args.append(f4)
        body = functools.partial(_fox_prompt_kernel, seq=s, blk=blk, sup=sup)
        scratch = [col_state, col_state, acc_state]
    else:
        body = functools.partial(_sb_prompt_kernel, seq=s, blk=blk, sup=sup)
        scratch = [col_state, acc_state]
    in_specs.append(part_spec(z[1]))
    args.append(z[0])
    vmem = 2 * 5 * _nbytes((s, HEAD_DIM), BF16) + 2 * _nbytes((HEAD_ROWS, s), F32)
    vmem += 3 * _nbytes((sup, LANES), F32) + 12 * _nbytes((sup, blk), F32)
    return pl.pallas_call(
        body,
        grid=(b, w // HEAD_DIM),
        in_specs=in_specs,
        out_specs=head_spec,
        out_shape=jax.ShapeDtypeStruct((b, s, w), BF16),
        scratch_shapes=scratch,
        compiler_params=_params(2, max(vmem, OUTPUT_TO_HBM)),
        name=kind + "_prompt_attention",
    )(*args)


def _sample_kernel(*refs, past, t_new, keys):
    (fq, fkn, fvn, fkc, fvc, flf, fz, sq, skn, svn, skc, svc, sz,
     fo, so, fka, fva, ska, sva) = refs
    _sample_one("fox", fq, fkn, fvn, fkc, fvc, flf, fz, fo, fka, fva, past=past, t_new=t_new, keys=keys)
    _sample_one("sb", sq, skn, svn, skc, svc, None, sz, so, ska, sva, past=past, t_new=t_new, keys=keys)


def _sample_one(kind, q_ref, kn_ref, vn_ref, kc_ref, vc_ref, lf_ref, z_ref, o_ref, ka_ref, va_ref, *,
                past, t_new, keys):
    rows = N_HEADS * t_new
    width = N_HEADS * HEAD_DIM
    pad = keys - past - t_new

    for src_c, src_n, dst in ((kc_ref, kn_ref, ka_ref), (vc_ref, vn_ref, va_ref)):
        dst[0:past, :] = src_c[0].reshape(past, width).astype(BF16)
        dst[past:past + t_new, :] = src_n[0]
        dst[past + t_new:keys, :] = jnp.zeros((pad, width), BF16)

    r_q = lax.broadcasted_iota(jnp.int32, (rows, width), 0)
    c_q = lax.broadcasted_iota(jnp.int32, (rows, width), 1)
    q_tiled = jnp.concatenate([q_ref[0]] * N_HEADS, axis=0)
    same_head = _shift_div(r_q, t_new) == _shift_div(c_q, HEAD_DIM)
    q_bd = jnp.where(same_head, q_tiled, jnp.zeros_like(q_tiled))

    qk = _dot_nt(q_bd, ka_ref[...])
    r_s = lax.broadcasted_iota(jnp.int32, (rows, keys), 0)
    c_s = lax.broadcasted_iota(jnp.int32, (rows, keys), 1)
    q_pos = past + (r_s & (t_new - 1))

    if kind == "fox":
        f_all = _cumsum_lanes(lf_ref[0], LANES)
        f_rows = jnp.concatenate(
            [jnp.broadcast_to(f_all[hh:hh + 1, :], (t_new, keys)) for hh in range(N_HEADS)], axis=0)
        s = jnp.where(c_s <= q_pos, qk - f_rows * LOG2E, -jnp.inf)
        m = jnp.max(s, axis=-1, keepdims=True)
        p = jnp.exp2(s - m)
        wts = p
        norm = jnp.sum(p, axis=-1, keepdims=True)
    else:
        w = qk
        before = c_s < q_pos
        tri = _tri_after(LANES)
        carry = jnp.zeros((rows, LANES), F32)
        chunks = [None] * (keys // LANES)
        for c in reversed(range(keys // LANES)):
            sl = slice(c * LANES, (c + 1) * LANES)
            chunks[c], carry = _sb_block(w[:, sl], carry, before[:, sl], tri)
        wts = jnp.concatenate(chunks, axis=1)
        norm = None

    o_full = _dot(wts.astype(BF16), va_ref[...])
    if norm is not None:
        o_full = o_full / norm
    o = jnp.concatenate(
        [o_full[hh * t_new:(hh + 1) * t_new, hh * HEAD_DIM:(hh + 1) * HEAD_DIM] for hh in range(N_HEADS)],
        axis=1)
    o_ref[0] = (o * _silu(z_ref[0].astype(F32))).astype(o_ref.dtype)


def sample_attention(fox, sb, logf_all):
    b, t_new, w = fox[1].shape
    past = fox[3].shape[1]
    assert t_new & (t_new - 1) == 0
    keys = -(-(past + t_new) // LANES) * LANES
    new_spec = pl.BlockSpec((1, t_new, w), lambda i: (i, 0, 0))
    part_spec = lambda part: pl.BlockSpec((1, t_new, w), lambda i: (i, 0, part))
    cache_spec = pl.BlockSpec((1, past, N_HEADS, HEAD_DIM), lambda i: (i, 0, 0, 0))
    logf_spec = pl.BlockSpec((1, HEAD_ROWS, keys), lambda i: (i, 0, 0))
    mixer_specs = lambda m: [part_spec(m[0][1]), new_spec, new_spec, cache_spec, cache_spec]
    vmem = 8 * _nbytes((past, w), F32) + 4 * _nbytes((keys, w), BF16)
    vmem += 24 * _nbytes((N_HEADS * t_new, keys), F32)
    return pl.pallas_call(
        functools.partial(_sample_kernel, past=past, t_new=t_new, keys=keys),
        grid=(b,),
        in_specs=(mixer_specs(fox) + [logf_spec, part_spec(fox[5][1])]
                  + mixer_specs(sb) + [part_spec(sb[5][1])]),
        out_specs=[new_spec, new_spec],
        out_shape=[jax.ShapeDtypeStruct((b, t_new, w), BF16)] * 2,
        scratch_shapes=[pltpu.VMEM((keys, w), BF16)] * 4,
        compiler_params=_params(1, vmem),
        name="sample_attention",
    )(fox[0][0], *fox[1:5], logf_all, fox[5][0], sb[0][0], *sb[1:5], sb[5][0])


def _causal_mix_weights(w_ref):
    n_groups, chunk, _ = w_ref.shape
    r = lax.broadcasted_iota(jnp.int32, (chunk, chunk), 0)
    c = lax.broadcasted_iota(jnp.int32, (chunk, chunk), 1)
    return [jnp.where(c <= r, w_ref[g], 0.0).astype(BF16) for g in range(n_groups)]


def _layer_norm(v, g_ref, b_ref):
    mu = jnp.mean(v, axis=-1, keepdims=True)
    vc = v - mu
    var = jnp.mean(vc * vc, axis=-1, keepdims=True)
    return vc * lax.rsqrt(var + LN_EPS) * g_ref[...] + b_ref[...]


def _mix_and_gate(w_causal, vnb, bias_ref, u_ref, z_ref, rows, store):
    group_dim = vnb.shape[1] // len(w_causal)
    for g, wg in enumerate(w_causal):
        sl = slice(g * group_dim, (g + 1) * group_dim)
        mixed = _dot(wg, vnb[:, sl]) + bias_ref[:, g:g + 1]
        gate = _silu(z_ref[rows, sl].astype(F32))
        store(sl, (u_ref[rows, sl].astype(F32) * mixed * gate).astype(BF16))


def _sgu_kernel(u_ref, v_ref, z_ref, g_ref, b_ref, w_ref, bias_ref, o_ref, *vn_refs):
    chunk = w_ref.shape[1]
    w_causal = _causal_mix_weights(w_ref)
    for s in range(v_ref.shape[0] // chunk):
        rs = slice(s * chunk, (s + 1) * chunk)
        vn = _layer_norm(v_ref[rs, :].astype(F32), g_ref, b_ref)
        if vn_refs:
            vn_refs[0][rs, :] = vn

        def store(sl, val, rs=rs):
            o_ref[rs, sl] = val

        _mix_and_gate(w_causal, vn.astype(BF16), bias_ref, u_ref, z_ref, rs, store)


SGU_CHUNKS_PER_STEP = 2


def spatial_gate(uvz, ln_g, ln_b, w_mix, bias_t, emit_vn):
    m, c3 = uvz.shape
    c = c3 // 3
    n_groups, chunk, _ = w_mix.shape
    rows = SGU_CHUNKS_PER_STEP * chunk
    assert m % rows == 0
    blk = lambda j: pl.BlockSpec((rows, c), lambda i, j=j: (i, j))
    out_shape = [jax.ShapeDtypeStruct((m, c), BF16)]
    out_specs = [pl.BlockSpec((rows, c), lambda i: (i, 0))]
    if emit_vn:
        out_shape.append(jax.ShapeDtypeStruct((m, c), F32))
        out_specs.append(pl.BlockSpec((rows, c), lambda i: (i, 0)))
    vmem = 6 * _nbytes((rows, c), uvz.dtype) + 2 * _nbytes((rows, c), BF16) + 8 * _nbytes((rows, c), F32)
    vmem += 2 * _nbytes(w_mix.shape, F32)
    return pl.pallas_call(
        _sgu_kernel,
        grid=(m // rows,),
        in_specs=[blk(0), blk(1), blk(2),
                  pl.BlockSpec((1, c), lambda i: (0, 0)),
                  pl.BlockSpec((1, c), lambda i: (0, 0)),
                  pl.BlockSpec((n_groups, chunk, chunk), lambda i: (0, 0, 0)),
                  pl.BlockSpec((chunk, n_groups), lambda i: (0, 0))],
        out_specs=out_specs,
        out_shape=out_shape,
        compiler_params=_params(1, vmem),
        name="spatial_gate",
    )(uvz, uvz, uvz, ln_g.reshape(1, c), ln_b.reshape(1, c), w_mix, bias_t)


def _layer0_inputs(xn_prompt, xn_sample, w_t):
    row_sets = [([xn_prompt], BF16, None), ([xn_sample], BF16, None)]
    qz = project(row_sets, w_t, col_blocks=QZ_PARTS, w_transposed=True,
                 out_scale=(QK_SCALE, 1.0, QK_SCALE, 1.0), name="in_proj0_qz")
    kv, kv_out = ([], []), ([], [])
    for p in KV_PARTS:
        outs = project(row_sets, w_t, col_blocks=(p,), heads_out=True, w_transposed=True, name="in_proj0_kv")
        for g, out in enumerate(outs):
            kv[g].append(out[0])
            kv_out[g].append(out[1])
    return qz, kv, kv_out


def kernel(x_prompt, x_sample, cache_fox_k, cache_fox_v, cache_fox_logf, cache_sb_k, cache_sb_v,
           norm0_g, w_in0, b_forget, w_out0, norm1_g, w_in1, sgu_ln_g, sgu_ln_b, w_sp, b_sp, w_out1, final_g):
    bp, s, d = x_prompt.shape
    bs, t_new, _ = x_sample.shape
    past = cache_fox_k.shape[1]
    hp = x_prompt.reshape(bp * s, d)
    hs = x_sample.reshape(bs * t_new, d)

    n_main = 8 * MIX_WIDTH
    w_t = w_in0.T
    wf = jnp.pad(w_in0[:, n_main:], ((0, 0), (0, LANES - N_HEADS))).astype(BF16)
    bf = jnp.pad(b_forget, (0, LANES - N_HEADS)).reshape(1, LANES).astype(F32)

    xn_p = rmsnorm(hp, norm0_g, BF16)
    xn_s = rmsnorm(hs, norm0_g, BF16)
    (qz_p, qz_s), ((fk, fv, sk, sv), (gk, gv, tk, tv)), (kv_prompt, kv_sample) = _layer0_inputs(xn_p, xn_s, w_t)
    logf_t, cum_t = forget_gate(xn_p, wf, bf, bp, True)
    (logf_s,) = forget_gate(xn_s, wf, bf, 1, False)

    as_seq = lambda a: a.reshape(bp, s, -1)
    qz_p = as_seq(qz_p)
    f4 = cum_t.reshape(bp, HEAD_ROWS, s // FOX_BLOCK, FOX_BLOCK)
    mix_f = prompt_attention("fox", (qz_p, 0), as_seq(fk), as_seq(fv), (qz_p, 1), f4)
    mix_s = prompt_attention("sb", (qz_p, 2), as_seq(sk), as_seq(sv), (qz_p, 3))
    fox_logf_prompt = jnp.swapaxes(logf_t[:, :N_HEADS, :], 1, 2)

    as_new = lambda a: a.reshape(bs, t_new, -1)
    qz_s = as_new(qz_s)
    logf_new_t = jnp.swapaxes(logf_s[0, :N_HEADS, :].reshape(N_HEADS, bs, t_new), 0, 1)
    keys = -(-(past + t_new) // LANES) * LANES
    logf_all = jnp.concatenate([jnp.swapaxes(cache_fox_logf.astype(F32), 1, 2), logf_new_t], axis=2)
    logf_all = jnp.pad(logf_all, ((0, 0), (0, HEAD_ROWS - N_HEADS), (0, keys - past - t_new)))
    mix_g, mix_t = sample_attention(
        ((qz_s, 0), as_new(gk), as_new(gv), cache_fox_k, cache_fox_v, (qz_s, 1)),
        ((qz_s, 2), as_new(tk), as_new(tv), cache_sb_k, cache_sb_v, (qz_s, 3)), logf_all)
    fox_logf_sample = jnp.swapaxes(logf_new_t, 1, 2)
    flat = lambda a: a.reshape(-1, MIX_WIDTH)
    hp, hs = project([([flat(mix_f), flat(mix_s)], F32, hp), ([flat(mix_g), flat(mix_t)], F32, hs)],
                     w_out0.astype(BF16), name="out_proj0")

    c_width = w_in1.shape[1] // 3
    bias_t = b_sp.T.astype(F32)
    uvz_p, uvz_s = project([([rmsnorm(hp, norm1_g, BF16)], BF16, None), ([rmsnorm(hs, norm1_g, BF16)], F32, None)],
                           w_in1, name="in_proj1")
    (act_p,) = spatial_gate(uvz_p, sgu_ln_g, sgu_ln_b, w_sp, bias_t, False)

    per_blk = C_CHUNK // t_new
    eye = jnp.eye(per_blk, dtype=F32)
    w_corner = w_sp[:, :t_new, :t_new]
    w_bd = jnp.einsum("ab,gts->gatbs", eye, w_corner).reshape(C_GROUPS, C_CHUNK, C_CHUNK)
    bias_bd = jnp.tile(b_sp[:, :t_new].T.astype(F32), (per_blk, 1))
    act_s, vn_s = spatial_gate(uvz_s, sgu_ln_g, sgu_ln_b, w_bd, bias_bd, True)
    hp, hs = project([([act_p], F32, hp), ([act_s], F32, hs)], w_out1, tn=512, name="out_proj1")

    y_prompt = rmsnorm(hp, final_g, F32).reshape(bp, s, d)
    y_sample = rmsnorm(hs, final_g, F32).reshape(bs, t_new, d)
    fk4, fv4, sk4, sv4 = (a.reshape(bp, s, N_HEADS, HEAD_DIM) for a in kv_prompt)
    gk4, gv4, tk4, tv4 = (a.reshape(bs, t_new, N_HEADS, HEAD_DIM) for a in kv_sample)
    return (y_prompt, y_sample,
            fk4, fv4, fox_logf_prompt,
            gk4, gv4, fox_logf_sample,
            sk4, sv4,
            tk4, tv4,
            vn_s.reshape(bs, t_new, c_width))
```

```python
import functools

import jax
import jax.numpy as jnp
from jax import lax
from jax.experimental import pallas as pl
from jax.experimental.pallas import tpu as pltpu

F32 = jnp.float32
BF16 = jnp.bfloat16

HEAD_DIM = 128
N_HEADS = 8
MIX_WIDTH = N_HEADS * HEAD_DIM
C_GROUPS = 16
C_CHUNK = 128
RMS_EPS = 1e-6
LN_EPS = 1e-5
ATTN_SCALE = HEAD_DIM ** -0.5
LOG2E = 1.4426950408889634
QK_SCALE = ATTN_SCALE * LOG2E
QZ_PARTS = (0, 3, 4, 7)
KV_PARTS = (1, 2, 5, 6)

LANES = 128
ATTN_BLOCK = 256
FOX_BLOCK = 512
ATTN_SUPER = 1024
V7X_VMEM_BYTES = 64 * 1024 * 1024
VMEM_LIMIT_CAP = V7X_VMEM_BYTES - 8 * 1024 * 1024


def _params(n_grid_axes, vmem_bytes, flags=None):
    limit = min(int(vmem_bytes * 1.25) + (4 << 20), VMEM_LIMIT_CAP)
    return pltpu.CompilerParams(
        dimension_semantics=("arbitrary",) * n_grid_axes,
        vmem_limit_bytes=limit,
        flags=flags,
    )


def _nbytes(shape, dtype):
    n = 1
    for s in shape:
        n *= s
    return n * jnp.dtype(dtype).itemsize


def _dot(a, b):
    return jnp.dot(a, b, preferred_element_type=F32)


def _dot_nt(a, b):
    return lax.dot_general(a, b, (((1,), (1,)), ((), ())), preferred_element_type=F32)


def _log_sigmoid(x):
    return jnp.minimum(x, 0.0) - jnp.log(1.0 + jnp.exp(-jnp.abs(x)))


def _silu(x):
    return x / (1.0 + jnp.exp2(x * -LOG2E))


def _widen(x, n):
    return x if n == LANES else jnp.concatenate([x] * (n // LANES), axis=1)


def _shift_div(x, n):
    assert n & (n - 1) == 0
    return lax.shift_right_logical(x, jnp.full_like(x, n.bit_length() - 1))


def _split3(x):
    hi = x.astype(BF16)
    r1 = x - hi.astype(F32)
    mid = r1.astype(BF16)
    lo = (r1 - mid.astype(F32)).astype(BF16)
    return hi, mid, lo


def _rms_kernel(x_ref, g_ref, o_ref):
    x = x_ref[...]
    ms = jnp.mean(x * x, axis=-1, keepdims=True)
    o_ref[...] = (x * lax.rsqrt(ms + RMS_EPS) * g_ref[...]).astype(o_ref.dtype)


def rmsnorm(x, g, out_dtype, tm=512):
    m, d = x.shape
    tm = min(tm, m)
    vmem = 2 * _nbytes((tm, d), F32) + 2 * _nbytes((tm, d), out_dtype)
    return pl.pallas_call(
        _rms_kernel,
        grid=(m // tm,),
        in_specs=[pl.BlockSpec((tm, d), lambda i: (i, 0)),
                  pl.BlockSpec((1, d), lambda i: (0, 0))],
        out_specs=pl.BlockSpec((tm, d), lambda i: (i, 0)),
        out_shape=jax.ShapeDtypeStruct((m, d), out_dtype),
        compiler_params=_params(1, vmem),
        name="rmsnorm",
    )(x, g.reshape(1, d))


def _proj_kernel(*refs, k_sizes, has_res, w_transposed, heads_out, has_extra, out_scale, post_norm):
    it = iter(refs)
    n_sets = 2 if has_extra else 1
    x_sets = [[next(it) for _ in k_sizes] for _ in range(n_sets)]
    w_ref = next(it)
    g_ref = next(it) if post_norm else None
    res_refs = [next(it) if has_res else None for _ in range(n_sets)]
    out_sets = [(next(it), next(it) if heads_out else None) for _ in range(n_sets)]
    if w_ref.dtype == BF16:
        wb_ref = w_ref
    else:
        wb_ref = next(it)

        @pl.when(pl.program_id(1) == 0)
        def _():
            wb_ref[...] = w_ref[...].astype(BF16)

    def row_set(x_refs, res_ref, o_ref, o4_ref):
        k0 = 0
        acc = None
        for x_ref, k in zip(x_refs, k_sizes):
            if w_transposed:
                part = _dot_nt(x_ref[...], wb_ref[:, k0:k0 + k])
            else:
                part = _dot(x_ref[...], wb_ref[k0:k0 + k, :])
            acc = part if acc is None else acc + part
            k0 += k
        if has_res:
            acc = acc + res_ref[...]
        if post_norm:
            ms = jnp.mean(acc * acc, axis=-1, keepdims=True)
            acc = acc * lax.rsqrt(ms + RMS_EPS) * g_ref[...]
        scaled = acc if out_scale is None else acc * _pick(out_scale, pl.program_id(0))
        o_ref[...] = scaled.astype(o_ref.dtype)
        if heads_out:
            o4_ref[...] = acc.reshape(acc.shape[0], N_HEADS, HEAD_DIM)

    row_set(x_sets[0], res_refs[0], *out_sets[0])
    if has_extra:
        @pl.when(pl.program_id(1) == pl.num_programs(1) - 1)
        def _():
            row_set(x_sets[1], res_refs[1], *out_sets[1])


def _pick(values, j):
    out = values[0]
    for t in range(1, len(values)):
        out = jnp.where(j >= t, values[t], out)
    return out


def project(row_sets, w, *, col0=0, n_cols=None, col_blocks=None, tn=1024, tm=1024, heads_out=False,
            w_transposed=False, out_scale=None, post_norm_g=None, name="proj"):
    assert 1 <= len(row_sets) <= 2
    xs0, _, res0 = row_sets[0]
    m = xs0[0].shape[0]
    k_sizes = tuple(x.shape[1] for x in xs0)
    k_total = sum(k_sizes)
    k_axis, n_axis = (1, 0) if w_transposed else (0, 1)
    assert k_total == w.shape[k_axis]
    if col_blocks is None:
        n_cols = w.shape[n_axis] - col0 if n_cols is None else n_cols
        assert n_cols % tn == 0 and col0 % tn == 0
        col_blocks = tuple(col0 // tn + j for j in range(n_cols // tn))
    n_j = len(col_blocks)
    n_cols = n_j * tn
    tm = min(tm, m)
    assert m % tm == 0
    has_res = res0 is not None
    if w_transposed:
        w_block, w_index = (tn, k_total), lambda j, i: (_pick(col_blocks, j), 0)
    else:
        w_block, w_index = (k_total, tn), lambda j, i: (0, _pick(col_blocks, j))
    fixed = pl.Buffered(1)
    w_spec = pl.BlockSpec(w_block, w_index, pipeline_mode=fixed) if n_j == 1 else pl.BlockSpec(w_block, w_index)
    cast_w = w.dtype != BF16
    vmem = (1 if n_j == 1 else 2) * _nbytes(w_block, w.dtype) + (_nbytes(w_block, BF16) if cast_w else 0)

    x_specs, x_args, res_specs, res_args, out_specs, out_shape = [], [], [], [], [], []
    for s, (xs, out_dtype, res) in enumerate(row_sets):
        rows = xs[0].shape[0]
        assert tuple(x.shape[1] for x in xs) == k_sizes and (res is not None) == has_res
        if s == 0:
            tile, row_index, bufs = tm, (lambda j, i: i), 2
            x_specs += [pl.BlockSpec((tile, k), lambda j, i: (i, 0)) for k in k_sizes]
        else:
            tile, row_index, bufs = rows, (lambda j, i: 0), 1
            x_specs += [pl.BlockSpec((tile, k), lambda j, i: (0, 0), pipeline_mode=fixed) for k in k_sizes]
        x_args += list(xs)
        vmem += sum(bufs * _nbytes((tile, k), BF16) for k in k_sizes)
        if has_res:
            res_specs.append(pl.BlockSpec((tile, tn), lambda j, i, r=row_index: (r(j, i), j)))
            res_args.append(res)
            vmem += 2 * _nbytes((tile, tn), F32)
        out_specs.append(pl.BlockSpec((tile, tn), lambda j, i, r=row_index: (r(j, i), j)))
        out_shape.append(jax.ShapeDtypeStruct((rows, n_cols), out_dtype))
        vmem += 2 * _nbytes((tile, tn), out_dtype) + _nbytes((tile, tn), F32)
        if heads_out:
            assert n_cols == tn == MIX_WIDTH
            out_specs.append(pl.BlockSpec((tile, N_HEADS, HEAD_DIM), lambda j, i, r=row_index: (r(j, i), 0, 0)))
            out_shape.append(jax.ShapeDtypeStruct((rows, N_HEADS, HEAD_DIM), F32))
            vmem += 2 * _nbytes((tile, tn), F32)
    norm_specs, norm_args = [], []
    if post_norm_g is not None:
        assert n_j == 1 and col_blocks == (0,) and n_cols == w.shape[n_axis]
        norm_specs.append(pl.BlockSpec((1, n_cols), lambda j, i: (0, 0)))
        norm_args.append(post_norm_g.reshape(1, n_cols))
        vmem += 2 * _nbytes((tm, n_cols), F32)
    outs = pl.pallas_call(
        functools.partial(_proj_kernel, k_sizes=k_sizes, has_res=has_res, w_transposed=w_transposed,
                          heads_out=heads_out, has_extra=len(row_sets) == 2, out_scale=out_scale,
                          post_norm=post_norm_g is not None),
        grid=(n_j, m // tm),
        in_specs=x_specs + [w_spec] + norm_specs + res_specs,
        out_specs=out_specs,
        out_shape=out_shape,
        scratch_shapes=[pltpu.VMEM(w_block, BF16)] if cast_w else [],
        compiler_params=_params(2, vmem),
        name=name,
    )(*x_args, w, *norm_args, *res_args)
    per_set = 2 if heads_out else 1
    return [tuple(outs[s * per_set:(s + 1) * per_set]) if heads_out else outs[s] for s in range(len(row_sets))]


CUM_CHUNK = 256
HEAD_ROWS = 16


def _tri_incl(n):
    r = lax.broadcasted_iota(jnp.int32, (n, n), 0)
    c = lax.broadcasted_iota(jnp.int32, (n, n), 1)
    return jnp.where(r <= c, 1.0, 0.0).astype(BF16)


def _cumsum_lanes(x, chunk):
    n = x.shape[1]
    tri = _tri_incl(chunk)
    carry = jnp.zeros((x.shape[0], 1), F32)
    outs = []
    for c in range(n // chunk):
        hi, mid, lo = _split3(x[:, c * chunk:(c + 1) * chunk])
        loc = _dot(hi, tri) + _dot(mid, tri) + _dot(lo, tri)
        outs.append(loc + carry)
        carry = carry + loc[:, chunk - 1:chunk]
    return jnp.concatenate(outs, axis=1)


def _forget_kernel(x_ref, w_ref, b_ref, logf_ref, *cum_refs):
    logit = _dot(x_ref[...], w_ref[...]) + b_ref[...]
    logf_t = _log_sigmoid(logit).T[:HEAD_ROWS, :]
    logf_ref[0] = logf_t
    if cum_refs:
        cum_refs[0][0] = _cumsum_lanes(logf_t, CUM_CHUNK)


def forget_gate(xn, wf, bf, n_seq, with_cumsum):
    m, d = xn.shape
    t = m // n_seq
    out_shape = [jax.ShapeDtypeStruct((n_seq, HEAD_ROWS, t), F32)]
    out_specs = [pl.BlockSpec((1, HEAD_ROWS, t), lambda s: (s, 0, 0))]
    if with_cumsum:
        out_shape = out_shape * 2
        out_specs = out_specs * 2
    vmem = 2 * _nbytes((t, d), BF16) + 2 * _nbytes((d, LANES), BF16) + 8 * _nbytes((t, LANES), F32)
    return pl.pallas_call(
        _forget_kernel,
        grid=(n_seq,),
        in_specs=[pl.BlockSpec((t, d), lambda s: (s, 0)),
                  pl.BlockSpec((d, LANES), lambda s: (0, 0)),
                  pl.BlockSpec((1, LANES), lambda s: (0, 0))],
        out_specs=out_specs,
        out_shape=out_shape,
        compiler_params=_params(1, vmem),
        name="forget_gate",
    )(xn, wf, bf)


def _pipelined(steps, stages):
    vals = [None] * len(steps)
    for t in range(len(steps) + len(stages) - 1):
        for j, stage in enumerate(stages):
            i = t - j
            if 0 <= i < len(steps):
                vals[i] = stage(steps[i]) if j == 0 else stage(steps[i], vals[i])


def _fox_prompt_kernel(q_ref, k_ref, v_ref, f_ref, z_ref, o_ref, m_ref, l_ref, acc_ref, *, seq, blk, sup):
    h = pl.program_id(1)
    n_sub = sup // blk

    for qs in range(seq // sup):
        q0 = qs * sup
        m_ref[...] = jnp.full(m_ref.shape, -jnp.inf, F32)
        l_ref[...] = jnp.zeros(l_ref.shape, F32)
        acc_ref[...] = jnp.zeros(acc_ref.shape, F32)

        def logits(step):
            kj, row_lo, diagonal = step
            q = q_ref[0, q0 + row_lo:q0 + sup, :]
            s = _dot_nt(q, k_ref[0, kj * blk:(kj + 1) * blk, :])
            s = s - f_ref[0, pl.ds(h, 1), kj:kj + 1, :].reshape(1, blk) * LOG2E
            if diagonal:
                r = lax.broadcasted_iota(jnp.int32, s.shape, 0)
                c = lax.broadcasted_iota(jnp.int32, s.shape, 1)
                s = jnp.where(c <= r, s, -jnp.inf)
            return s

        def weights(step, s):
            sl = slice(step[1], sup)
            m_old = m_ref[sl, :]
            m_new = jnp.maximum(m_old, jnp.max(s, axis=-1, keepdims=True))
            alpha = jnp.exp2(m_old - m_new)
            p = jnp.exp2(s - _widen(m_new, blk))
            m_ref[sl, :] = m_new
            l_ref[sl, :] = alpha * l_ref[sl, :] + jnp.sum(p, axis=-1, keepdims=True)
            return alpha, p.astype(BF16)

        def accumulate(step, alpha_p):
            kj, row_lo, _ = step
            sl = slice(row_lo, sup)
            alpha, p = alpha_p
            acc_ref[sl, :] = alpha * acc_ref[sl, :] + _dot(p, v_ref[0, kj * blk:(kj + 1) * blk, :])

        steps = [(kj, 0, False) for kj in range(qs * n_sub)]
        steps += [(qs * n_sub + c, c * blk, True) for c in range(n_sub)]
        _pipelined(steps, [logits, weights, accumulate])
        gate = _silu(z_ref[0, q0:q0 + sup, :].astype(F32))
        o_ref[0, q0:q0 + sup, :] = (acc_ref[...] / l_ref[...] * gate).astype(o_ref.dtype)


def _tri_after(n):
    r = lax.broadcasted_iota(jnp.int32, (n, n), 0)
    c = lax.broadcasted_iota(jnp.int32, (n, n), 1)
    return jnp.where(r > c, 1.0, 0.0).astype(BF16)


def _sb_block(w, carry, before, tri):
    log_stay, log_take = _sb_logs(w, before)
    a = _sb_weights(log_take, _sb_suffix(log_stay, tri), carry, before)
    return a, carry + jnp.sum(log_stay, axis=-1, keepdims=True)


def _sb_logs(w, before):
    nw = -w
    soft = jnp.log(1.0 + jnp.exp2(jnp.minimum(w, nw))) * LOG2E
    log_stay = jnp.minimum(nw, 0.0) - soft
    log_take = log_stay + w
    if before is not None:
        log_stay = jnp.where(before, log_stay, 0.0)
    return log_stay, log_take


def _sb_suffix(log_stay, tri):
    return _dot(log_stay.astype(BF16), tri)


def _sb_weights(log_take, suffix, carry, before):
    a = jnp.exp2(log_take + (suffix + _widen(carry, suffix.shape[1])))
    if before is not None:
        a = jnp.where(before, a, 0.0)
    return a


def _sb_prompt_kernel(q_ref, k_ref, v_ref, z_ref, o_ref, carry_ref, acc_ref, *, seq, blk, sup):
    n_sub = sup // blk
    tri = _tri_after(blk)

    for qs in range(seq // sup):
        q0 = qs * sup
        carry_ref[...] = jnp.zeros(carry_ref.shape, F32)
        acc_ref[...] = jnp.zeros(acc_ref.shape, F32)

        def mask(step):
            _, row_lo, row_hi, diagonal = step
            if not diagonal:
                return None
            r = lax.broadcasted_iota(jnp.int32, (row_hi - row_lo, blk), 0)
            c = lax.broadcasted_iota(jnp.int32, (row_hi - row_lo, blk), 1)
            return c < r

        def logits(step):
            kj, row_lo, row_hi, _ = step
            q = q_ref[0, q0 + row_lo:q0 + row_hi, :]
            return _dot_nt(q, k_ref[0, kj * blk:(kj + 1) * blk, :])

        def suffix_sums(step, w):
            sl = slice(step[1], step[2])
            log_stay, log_take = _sb_logs(w, mask(step))
            carry = carry_ref[sl, :]
            carry_ref[sl, :] = carry + jnp.sum(log_stay, axis=-1, keepdims=True)
            return log_take, _sb_suffix(log_stay, tri), carry

        def accumulate(step, parts):
            kj, row_lo, row_hi, _ = step
            sl = slice(row_lo, row_hi)
            a = _sb_weights(*parts, mask(step)).astype(BF16)
            acc_ref[sl, :] = acc_ref[sl, :] + _dot(a, v_ref[0, kj * blk:(kj + 1) * blk, :])

        steps = []
        for c in reversed(range(n_sub)):
            steps.append((qs * n_sub + c, c * blk, (c + 1) * blk, True))
            if c + 1 < n_sub:
                steps.append((qs * n_sub + c, (c + 1) * blk, sup, False))
        steps += [(kj, 0, sup, False) for kj in reversed(range(qs * n_sub))]
        _pipelined(steps, [logits, suffix_sums, accumulate])
        gate = _silu(z_ref[0, q0:q0 + sup, :].astype(F32))
        o_ref[0, q0:q0 + sup, :] = (acc_ref[...] * gate).astype(o_ref.dtype)


def prompt_attention(kind, q, k, v, z, f4=None):
    b, s, w = k.shape
    heads = w // HEAD_DIM
    blk = f4.shape[3] if kind == "fox" else ATTN_BLOCK
    sup = min(ATTN_SUPER, s)
    head_spec = pl.BlockSpec((1, s, HEAD_DIM), lambda bi, hi: (bi, 0, hi))
    part_spec = lambda part: pl.BlockSpec((1, s, HEAD_DIM), lambda bi, hi: (bi, 0, part * heads + hi))
    in_specs = [part_spec(q[1]), head_spec, head_spec]
    args = [q[0], k, v]
    col_state = pltpu.VMEM((sup, LANES), F32)
    acc_state = pltpu.VMEM((sup, HEAD_DIM), F32)
    if kind == "fox":
        in_specs.append(pl.BlockSpec((1, HEAD_ROWS, s // blk, blk), lambda bi, hi: (bi, 0, 0, 0)))
        args.append(f4)
        body = functools.partial(_fox_prompt_kernel, seq=s, blk=blk, sup=sup)
        scratch = [col_state, col_state, acc_state]
    else:
        body = functools.partial(_sb_prompt_kernel, seq=s, blk=blk, sup=sup)
        scratch = [col_state, acc_state]
    in_specs.append(part_spec(z[1]))
    args.append(z[0])
    vmem = 2 * 5 * _nbytes((s, HEAD_DIM), BF16) + 2 * _nbytes((HEAD_ROWS, s), F32)
    vmem += 3 * _nbytes((sup, LANES), F32) + 12 * _nbytes((sup, blk), F32)
    return pl.pallas_call(
        body,
        grid=(b, w // HEAD_DIM),
        in_specs=in_specs,
        out_specs=head_spec,
        out_shape=jax.ShapeDtypeStruct((b, s, w), BF16),
        scratch_shapes=scratch,
        compiler_params=_params(2, vmem),
        name=kind + "_prompt_attention",
    )(*args)


def _sample_kernel(*refs, past, t_new, keys):
    (fq, fkn, fvn, fkc, fvc, flf, fz, sq, skn, svn, skc, svc, sz,
     fo, so, fka, fva, ska, sva) = refs
    _sample_one("fox", fq, fkn, fvn, fkc, fvc, flf, fz, fo, fka, fva, past=past, t_new=t_new, keys=keys)
    _sample_one("sb", sq, skn, svn, skc, svc, None, sz, so, ska, sva, past=past, t_new=t_new, keys=keys)


def _sample_one(kind, q_ref, kn_ref, vn_ref, kc_ref, vc_ref, lf_ref, z_ref, o_ref, ka_ref, va_ref, *,
                past, t_new, keys):
    rows = N_HEADS * t_new
    width = N_HEADS * HEAD_DIM
    pad = keys - past - t_new

    for src_c, src_n, dst in ((kc_ref, kn_ref, ka_ref), (vc_ref, vn_ref, va_ref)):
        dst[0:past, :] = src_c[0].reshape(past, width).astype(BF16)
        dst[past:past + t_new, :] = src_n[0]
        dst[past + t_new:keys, :] = jnp.zeros((pad, width), BF16)

    r_q = lax.broadcasted_iota(jnp.int32, (rows, width), 0)
    c_q = lax.broadcasted_iota(jnp.int32, (rows, width), 1)
    q_tiled = jnp.concatenate([q_ref[0]] * N_HEADS, axis=0)
    same_head = _shift_div(r_q, t_new) == _shift_div(c_q, HEAD_DIM)
    q_bd = jnp.where(same_head, q_tiled, jnp.zeros_like(q_tiled))

    qk = _dot_nt(q_bd, ka_ref[...])
    r_s = lax.broadcasted_iota(jnp.int32, (rows, keys), 0)
    c_s = lax.broadcasted_iota(jnp.int32, (rows, keys), 1)
    q_pos = past + (r_s & (t_new - 1))

    if kind == "fox":
        f_all = _cumsum_lanes(lf_ref[0], LANES)
        f_rows = jnp.concatenate(
            [jnp.broadcast_to(f_all[hh:hh + 1, :], (t_new, keys)) for hh in range(N_HEADS)], axis=0)
        s = jnp.where(c_s <= q_pos, qk - f_rows * LOG2E, -jnp.inf)
        m = jnp.max(s, axis=-1, keepdims=True)
        p = jnp.exp2(s - m)
        wts = p
        norm = jnp.sum(p, axis=-1, keepdims=True)
    else:
        w = qk
        before = c_s < q_pos
        tri = _tri_after(LANES)
        carry = jnp.zeros((rows, LANES), F32)
        chunks = [None] * (keys // LANES)
        for c in reversed(range(keys // LANES)):
            sl = slice(c * LANES, (c + 1) * LANES)
            chunks[c], carry = _sb_block(w[:, sl], carry, before[:, sl], tri)
        wts = jnp.concatenate(chunks, axis=1)
        norm = None

    o_full = _dot(wts.astype(BF16), va_ref[...])
    if norm is not None:
        o_full = o_full / norm
    o = jnp.concatenate(
        [o_full[hh * t_new:(hh + 1) * t_new, hh * HEAD_DIM:(hh + 1) * HEAD_DIM] for hh in range(N_HEADS)],
        axis=1)
    o_ref[0] = (o * _silu(z_ref[0].astype(F32))).astype(o_ref.dtype)


def sample_attention(fox, sb, logf_all):
    b, t_new, w = fox[1].shape
    past = fox[3].shape[1]
    assert t_new & (t_new - 1) == 0
    keys = -(-(past + t_new) // LANES) * LANES
    new_spec = pl.BlockSpec((1, t_new, w), lambda i: (i, 0, 0))
    part_spec = lambda part: pl.BlockSpec((1, t_new, w), lambda i: (i, 0, part))
    cache_spec = pl.BlockSpec((1, past, N_HEADS, HEAD_DIM), lambda i: (i, 0, 0, 0))
    logf_spec = pl.BlockSpec((1, HEAD_ROWS, keys), lambda i: (i, 0, 0))
    mixer_specs = lambda m: [part_spec(m[0][1]), new_spec, new_spec, cache_spec, cache_spec]
    vmem = 8 * _nbytes((past, w), F32) + 4 * _nbytes((keys, w), BF16)
    vmem += 24 * _nbytes((N_HEADS * t_new, keys), F32)
    return pl.pallas_call(
        functools.partial(_sample_kernel, past=past, t_new=t_new, keys=keys),
        grid=(b,),
        in_specs=(mixer_specs(fox) + [logf_spec, part_spec(fox[5][1])]
                  + mixer_specs(sb) + [part_spec(sb[5][1])]),
        out_specs=[new_spec, new_spec],
        out_shape=[jax.ShapeDtypeStruct((b, t_new, w), BF16)] * 2,
        scratch_shapes=[pltpu.VMEM((keys, w), BF16)] * 4,
        compiler_params=_params(1, vmem),
        name="sample_attention",
    )(fox[0][0], *fox[1:5], logf_all, fox[5][0], sb[0][0], *sb[1:5], sb[5][0])


def _causal_mix_weights(w_ref):
    n_groups, chunk, _ = w_ref.shape
    r = lax.broadcasted_iota(jnp.int32, (chunk, chunk), 0)
    c = lax.broadcasted_iota(jnp.int32, (chunk, chunk), 1)
    return [jnp.where(c <= r, w_ref[g], 0.0).astype(BF16) for g in range(n_groups)]


def _layer_norm(v, g_ref, b_ref):
    mu = jnp.mean(v, axis=-1, keepdims=True)
    vc = v - mu
    var = jnp.mean(vc * vc, axis=-1, keepdims=True)
    return vc * lax.rsqrt(var + LN_EPS) * g_ref[...] + b_ref[...]


def _mix_and_gate(w_causal, vnb, bias_ref, u_ref, z_ref, rows, store):
    group_dim = vnb.shape[1] // len(w_causal)
    for g, wg in enumerate(w_causal):
        sl = slice(g * group_dim, (g + 1) * group_dim)
        mixed = _dot(wg, vnb[:, sl]) + bias_ref[:, g:g + 1]
        gate = _silu(z_ref[rows, sl].astype(F32))
        store(sl, (u_ref[rows, sl].astype(F32) * mixed * gate).astype(BF16))


def _sgu_kernel(u_ref, v_ref, z_ref, g_ref, b_ref, w_ref, bias_ref, o_ref, *vn_refs):
    chunk = w_ref.shape[1]
    w_causal = _causal_mix_weights(w_ref)
    for s in range(v_ref.shape[0] // chunk):
        rs = slice(s * chunk, (s + 1) * chunk)
        vn = _layer_norm(v_ref[rs, :].astype(F32), g_ref, b_ref)
        if vn_refs:
            vn_refs[0][rs, :] = vn

        def store(sl, val, rs=rs):
            o_ref[rs, sl] = val

        _mix_and_gate(w_causal, vn.astype(BF16), bias_ref, u_ref, z_ref, rs, store)


SGU_CHUNKS_PER_STEP = 2


def spatial_gate(uvz, ln_g, ln_b, w_mix, bias_t, emit_vn):
    m, c3 = uvz.shape
    c = c3 // 3
    n_groups, chunk, _ = w_mix.shape
    rows = SGU_CHUNKS_PER_STEP * chunk
    assert m % rows == 0
    blk = lambda j: pl.BlockSpec((rows, c), lambda i, j=j: (i, j))
    out_shape = [jax.ShapeDtypeStruct((m, c), BF16)]
    out_specs = [pl.BlockSpec((rows, c), lambda i: (i, 0))]
    if emit_vn:
        out_shape.append(jax.ShapeDtypeStruct((m, c), F32))
        out_specs.append(pl.BlockSpec((rows, c), lambda i: (i, 0)))
    vmem = 6 * _nbytes((rows, c), uvz.dtype) + 2 * _nbytes((rows, c), BF16) + 8 * _nbytes((rows, c), F32)
    vmem += 2 * _nbytes(w_mix.shape, F32)
    return pl.pallas_call(
        _sgu_kernel,
        grid=(m // rows,),
        in_specs=[blk(0), blk(1), blk(2),
                  pl.BlockSpec((1, c), lambda i: (0, 0)),
                  pl.BlockSpec((1, c), lambda i: (0, 0)),
                  pl.BlockSpec((n_groups, chunk, chunk), lambda i: (0, 0, 0)),
                  pl.BlockSpec((chunk, n_groups), lambda i: (0, 0))],
        out_specs=out_specs,
        out_shape=out_shape,
        compiler_params=_params(1, vmem),
        name="spatial_gate",
    )(uvz, uvz, uvz, ln_g.reshape(1, c), ln_b.reshape(1, c), w_mix, bias_t)


def _layer0_inputs(xn_prompt, xn_sample, w_t):
    row_sets = [([xn_prompt], BF16, None), ([xn_sample], BF16, None)]
    qz = project(row_sets, w_t, col_blocks=QZ_PARTS, w_transposed=True,
                 out_scale=(QK_SCALE, 1.0, QK_SCALE, 1.0), name="in_proj0_qz")
    kv, kv_out = ([], []), ([], [])
    for p in KV_PARTS:
        outs = project(row_sets, w_t, col_blocks=(p,), heads_out=True, w_transposed=True, name="in_proj0_kv")
        for g, out in enumerate(outs):
            kv[g].append(out[0])
            kv_out[g].append(out[1])
    return qz, kv, kv_out


def kernel(x_prompt, x_sample, cache_fox_k, cache_fox_v, cache_fox_logf, cache_sb_k, cache_sb_v,
           norm0_g, w_in0, b_forget, w_out0, norm1_g, w_in1, sgu_ln_g, sgu_ln_b, w_sp, b_sp, w_out1, final_g):
    bp, s, d = x_prompt.shape
    bs, t_new, _ = x_sample.shape
    past = cache_fox_k.shape[1]
    hp = x_prompt.reshape(bp * s, d)
    hs = x_sample.reshape(bs * t_new, d)

    n_main = 8 * MIX_WIDTH
    w_t = w_in0.T
    wf = jnp.pad(w_in0[:, n_main:], ((0, 0), (0, LANES - N_HEADS))).astype(BF16)
    bf = jnp.pad(b_forget, (0, LANES - N_HEADS)).reshape(1, LANES).astype(F32)

    xn_p = rmsnorm(hp, norm0_g, BF16)
    xn_s = rmsnorm(hs, norm0_g, BF16)
    (qz_p, qz_s), ((fk, fv, sk, sv), (gk, gv, tk, tv)), (kv_prompt, kv_sample) = _layer0_inputs(xn_p, xn_s, w_t)
    logf_t, cum_t = forget_gate(xn_p, wf, bf, bp, True)
    (logf_s,) = forget_gate(xn_s, wf, bf, 1, False)

    as_seq = lambda a: a.reshape(bp, s, -1)
    qz_p = as_seq(qz_p)
    f4 = cum_t.reshape(bp, HEAD_ROWS, s // FOX_BLOCK, FOX_BLOCK)
    mix_f = prompt_attention("fox", (qz_p, 0), as_seq(fk), as_seq(fv), (qz_p, 1), f4)
    mix_s = prompt_attention("sb", (qz_p, 2), as_seq(sk), as_seq(sv), (qz_p, 3))
    fox_logf_prompt = jnp.swapaxes(logf_t[:, :N_HEADS, :], 1, 2)

    as_new = lambda a: a.reshape(bs, t_new, -1)
    qz_s = as_new(qz_s)
    logf_new_t = jnp.swapaxes(logf_s[0, :N_HEADS, :].reshape(N_HEADS, bs, t_new), 0, 1)
    keys = -(-(past + t_new) // LANES) * LANES
    logf_all = jnp.concatenate([jnp.swapaxes(cache_fox_logf.astype(F32), 1, 2), logf_new_t], axis=2)
    logf_all = jnp.pad(logf_all, ((0, 0), (0, HEAD_ROWS - N_HEADS), (0, keys - past - t_new)))
    mix_g, mix_t = sample_attention(
        ((qz_s, 0), as_new(gk), as_new(gv), cache_fox_k, cache_fox_v, (qz_s, 1)),
        ((qz_s, 2), as_new(tk), as_new(tv), cache_sb_k, cache_sb_v, (qz_s, 3)), logf_all)
    fox_logf_sample = jnp.swapaxes(logf_new_t, 1, 2)
    flat = lambda a: a.reshape(-1, MIX_WIDTH)
    hp, hs = project([([flat(mix_f), flat(mix_s)], F32, hp), ([flat(mix_g), flat(mix_t)], F32, hs)],
                     w_out0.astype(BF16), name="out_proj0")

    c_width = w_in1.shape[1] // 3
    bias_t = b_sp.T.astype(F32)
    uvz_p, uvz_s = project([([rmsnorm(hp, norm1_g, BF16)], BF16, None), ([rmsnorm(hs, norm1_g, BF16)], F32, None)],
                           w_in1, name="in_proj1")
    (act_p,) = spatial_gate(uvz_p, sgu_ln_g, sgu_ln_b, w_sp, bias_t, False)

    per_blk = C_CHUNK // t_new
    eye = jnp.eye(per_blk, dtype=F32)
    w_corner = w_sp[:, :t_new, :t_new]
    w_bd = jnp.einsum("ab,gts->gatbs", eye, w_corner).reshape(C_GROUPS, C_CHUNK, C_CHUNK)
    bias_bd = jnp.tile(b_sp[:, :t_new].T.astype(F32), (per_blk, 1))
    act_s, vn_s = spatial_gate(uvz_s, sgu_ln_g, sgu_ln_b, w_bd, bias_bd, True)
    w_out1_b = w_out1.astype(BF16)
    (y_prompt,) = project([([act_p], F32, hp)], w_out1_b, tn=d, tm=512, post_norm_g=final_g, name="out_proj1")
    (y_sample,) = project([([act_s], F32, hs)], w_out1_b, tn=d, tm=512, post_norm_g=final_g, name="out_proj1")
    y_prompt = y_prompt.reshape(bp, s, d)
    y_sample = y_sample.reshape(bs, t_new, d)
    fk4, fv4, sk4, sv4 = (a.reshape(bp, s, N_HEADS, HEAD_DIM) for a in kv_prompt)
    gk4, gv4, tk4, tv4 = (a.reshape(bs, t_new, N_HEADS, HEAD_DIM) for a in kv_sample)
    return (y_prompt, y_sample,
            fk4, fv4, fox_logf_prompt,
            gk4, gv4, fox_logf_sample,
            sk4, sv4,
            tk4, tv4,
            vn_s.reshape(bs, t_new, c_width))
```

```python
import functools

import jax
import jax.numpy as jnp
from jax import lax
from jax.experimental import pallas as pl
from jax.experimental.pallas import tpu as pltpu

F32 = jnp.float32
BF16 = jnp.bfloat16

HEAD_DIM = 128
N_HEADS = 8
MIX_WIDTH = N_HEADS * HEAD_DIM
C_GROUPS = 16
C_CHUNK = 128
RMS_EPS = 1e-6
LN_EPS = 1e-5
ATTN_SCALE = HEAD_DIM ** -0.5
LOG2E = 1.4426950408889634
QK_SCALE = ATTN_SCALE * LOG2E
QZ_PARTS = (0, 3, 4, 7)
KV_PARTS = (1, 2, 5, 6)

LANES = 128
ATTN_BLOCK = 256
FOX_BLOCK = 512
ATTN_SUPER = {"fox": 1024, "sb": 2048}
V7X_VMEM_BYTES = 64 * 1024 * 1024
VMEM_LIMIT_CAP = V7X_VMEM_BYTES - 8 * 1024 * 1024


def _params(n_grid_axes, vmem_bytes, flags=None):
    limit = min(int(vmem_bytes * 1.25) + (4 << 20), VMEM_LIMIT_CAP)
    return pltpu.CompilerParams(
        dimension_semantics=("arbitrary",) * n_grid_axes,
        vmem_limit_bytes=limit,
        flags=flags,
    )


def _nbytes(shape, dtype):
    n = 1
    for s in shape:
        n *= s
    return n * jnp.dtype(dtype).itemsize


def _dot(a, b):
    return jnp.dot(a, b, preferred_element_type=F32)


def _dot_nt(a, b):
    return lax.dot_general(a, b, (((1,), (1,)), ((), ())), preferred_element_type=F32)


def _log_sigmoid(x):
    return jnp.minimum(x, 0.0) - jnp.log(1.0 + jnp.exp(-jnp.abs(x)))


def _silu(x):
    return x / (1.0 + jnp.exp2(x * -LOG2E))


def _widen(x, n):
    return x if n == LANES else jnp.concatenate([x] * (n // LANES), axis=1)


def _shift_div(x, n):
    assert n & (n - 1) == 0
    return lax.shift_right_logical(x, jnp.full_like(x, n.bit_length() - 1))


def _split3(x):
    hi = x.astype(BF16)
    r1 = x - hi.astype(F32)
    mid = r1.astype(BF16)
    lo = (r1 - mid.astype(F32)).astype(BF16)
    return hi, mid, lo


def _rms_kernel(x_ref, g_ref, o_ref):
    x = x_ref[...]
    ms = jnp.mean(x * x, axis=-1, keepdims=True)
    o_ref[...] = (x * lax.rsqrt(ms + RMS_EPS) * g_ref[...]).astype(o_ref.dtype)


def rmsnorm(x, g, out_dtype, tm=512):
    m, d = x.shape
    tm = min(tm, m)
    vmem = 2 * _nbytes((tm, d), F32) + 2 * _nbytes((tm, d), out_dtype)
    return pl.pallas_call(
        _rms_kernel,
        grid=(m // tm,),
        in_specs=[pl.BlockSpec((tm, d), lambda i: (i, 0)),
                  pl.BlockSpec((1, d), lambda i: (0, 0))],
        out_specs=pl.BlockSpec((tm, d), lambda i: (i, 0)),
        out_shape=jax.ShapeDtypeStruct((m, d), out_dtype),
        compiler_params=_params(1, vmem),
        name="rmsnorm",
    )(x, g.reshape(1, d))


def _proj_kernel(*refs, k_sizes, has_res, w_transposed, heads_out, has_extra, out_scale, post_norm):
    it = iter(refs)
    n_sets = 2 if has_extra else 1
    x_sets = [[next(it) for _ in k_sizes] for _ in range(n_sets)]
    w_ref = next(it)
    g_ref = next(it) if post_norm else None
    res_refs = [next(it) if has_res else None for _ in range(n_sets)]
    second = heads_out or post_norm == "also"
    out_sets = [(next(it), next(it) if second else None) for _ in range(n_sets)]
    if w_ref.dtype == BF16:
        wb_ref = w_ref
    else:
        wb_ref = next(it)

        @pl.when(pl.program_id(1) == 0)
        def _():
            wb_ref[...] = w_ref[...].astype(BF16)

    def row_set(x_refs, res_ref, o_ref, o4_ref):
        k0 = 0
        acc = None
        for x_ref, k in zip(x_refs, k_sizes):
            if w_transposed:
                part = _dot_nt(x_ref[...], wb_ref[:, k0:k0 + k])
            else:
                part = _dot(x_ref[...], wb_ref[k0:k0 + k, :])
            acc = part if acc is None else acc + part
            k0 += k
        if has_res:
            acc = acc + res_ref[...]
        if post_norm:
            ms = jnp.mean(acc * acc, axis=-1, keepdims=True)
            normed = acc * lax.rsqrt(ms + RMS_EPS) * g_ref[...]
            if post_norm == "also":
                o4_ref[...] = normed.astype(o4_ref.dtype)
            else:
                acc = normed
        scaled = acc if out_scale is None else acc * _pick(out_scale, pl.program_id(0))
        o_ref[...] = scaled.astype(o_ref.dtype)
        if heads_out:
            o4_ref[...] = acc.reshape(acc.shape[0], N_HEADS, HEAD_DIM)

    row_set(x_sets[0], res_refs[0], *out_sets[0])
    if has_extra:
        @pl.when(pl.program_id(1) == pl.num_programs(1) - 1)
        def _():
            row_set(x_sets[1], res_refs[1], *out_sets[1])


def _pick(values, j):
    out = values[0]
    for t in range(1, len(values)):
        out = jnp.where(j >= t, values[t], out)
    return out


def project(row_sets, w, *, col0=0, n_cols=None, col_blocks=None, tn=1024, tm=1024, heads_out=False,
            w_transposed=False, out_scale=None, post_norm_g=None, post_norm_also=None, name="proj"):
    assert 1 <= len(row_sets) <= 2
    xs0, _, res0 = row_sets[0]
    m = xs0[0].shape[0]
    k_sizes = tuple(x.shape[1] for x in xs0)
    k_total = sum(k_sizes)
    k_axis, n_axis = (1, 0) if w_transposed else (0, 1)
    assert k_total == w.shape[k_axis]
    if col_blocks is None:
        n_cols = w.shape[n_axis] - col0 if n_cols is None else n_cols
        assert n_cols % tn == 0 and col0 % tn == 0
        col_blocks = tuple(col0 // tn + j for j in range(n_cols // tn))
    n_j = len(col_blocks)
    n_cols = n_j * tn
    tm = min(tm, m)
    assert m % tm == 0
    has_res = res0 is not None
    if w_transposed:
        w_block, w_index = (tn, k_total), lambda j, i: (_pick(col_blocks, j), 0)
    else:
        w_block, w_index = (k_total, tn), lambda j, i: (0, _pick(col_blocks, j))
    fixed = pl.Buffered(1)
    w_spec = pl.BlockSpec(w_block, w_index, pipeline_mode=fixed) if n_j == 1 else pl.BlockSpec(w_block, w_index)
    cast_w = w.dtype != BF16
    vmem = (1 if n_j == 1 else 2) * _nbytes(w_block, w.dtype) + (_nbytes(w_block, BF16) if cast_w else 0)

    x_specs, x_args, res_specs, res_args, out_specs, out_shape = [], [], [], [], [], []
    for s, (xs, out_dtype, res) in enumerate(row_sets):
        rows = xs[0].shape[0]
        assert tuple(x.shape[1] for x in xs) == k_sizes and (res is not None) == has_res
        if s == 0:
            tile, row_index, bufs = tm, (lambda j, i: i), 2
            x_specs += [pl.BlockSpec((tile, k), lambda j, i: (i, 0)) for k in k_sizes]
        else:
            tile, row_index, bufs = rows, (lambda j, i: 0), 1
            x_specs += [pl.BlockSpec((tile, k), lambda j, i: (0, 0), pipeline_mode=fixed) for k in k_sizes]
        x_args += list(xs)
        vmem += sum(bufs * _nbytes((tile, k), BF16) for k in k_sizes)
        if has_res:
            res_specs.append(pl.BlockSpec((tile, tn), lambda j, i, r=row_index: (r(j, i), j)))
            res_args.append(res)
            vmem += 2 * _nbytes((tile, tn), F32)
        out_specs.append(pl.BlockSpec((tile, tn), lambda j, i, r=row_index: (r(j, i), j)))
        out_shape.append(jax.ShapeDtypeStruct((rows, n_cols), out_dtype))
        vmem += 2 * _nbytes((tile, tn), out_dtype) + _nbytes((tile, tn), F32)
        if heads_out:
            assert n_cols == tn == MIX_WIDTH
            out_specs.append(pl.BlockSpec((tile, N_HEADS, HEAD_DIM), lambda j, i, r=row_index: (r(j, i), 0, 0)))
            out_shape.append(jax.ShapeDtypeStruct((rows, N_HEADS, HEAD_DIM), F32))
            vmem += 2 * _nbytes((tile, tn), F32)
        if post_norm_also is not None:
            assert not heads_out and post_norm_g is not None
            out_specs.append(pl.BlockSpec((tile, tn), lambda j, i, r=row_index: (r(j, i), j)))
            out_shape.append(jax.ShapeDtypeStruct((rows, n_cols), post_norm_also))
            vmem += 2 * _nbytes((tile, tn), post_norm_also)
    norm_specs, norm_args = [], []
    if post_norm_g is not None:
        assert n_j == 1 and col_blocks == (0,) and n_cols == w.shape[n_axis]
        norm_specs.append(pl.BlockSpec((1, n_cols), lambda j, i: (0, 0)))
        norm_args.append(post_norm_g.reshape(1, n_cols))
        vmem += 2 * _nbytes((tm, n_cols), F32)
    post_norm = None if post_norm_g is None else ("replace" if post_norm_also is None else "also")
    outs = pl.pallas_call(
        functools.partial(_proj_kernel, k_sizes=k_sizes, has_res=has_res, w_transposed=w_transposed,
                          heads_out=heads_out, has_extra=len(row_sets) == 2, out_scale=out_scale,
                          post_norm=post_norm),
        grid=(n_j, m // tm),
        in_specs=x_specs + [w_spec] + norm_specs + res_specs,
        out_specs=out_specs,
        out_shape=out_shape,
        scratch_shapes=[pltpu.VMEM(w_block, BF16)] if cast_w else [],
        compiler_params=_params(2, vmem),
        name=name,
    )(*x_args, w, *norm_args, *res_args)
    per_set = 2 if heads_out or post_norm == "also" else 1
    return [tuple(outs[s * per_set:(s + 1) * per_set]) if per_set == 2 else outs[s] for s in range(len(row_sets))]


CUM_CHUNK = 256
HEAD_ROWS = 16


def _tri_incl(n):
    r = lax.broadcasted_iota(jnp.int32, (n, n), 0)
    c = lax.broadcasted_iota(jnp.int32, (n, n), 1)
    return jnp.where(r <= c, 1.0, 0.0).astype(BF16)


def _cumsum_lanes(x, chunk):
    n = x.shape[1]
    tri = _tri_incl(chunk)
    carry = jnp.zeros((x.shape[0], 1), F32)
    outs = []
    for c in range(n // chunk):
        hi, mid, lo = _split3(x[:, c * chunk:(c + 1) * chunk])
        loc = _dot(hi, tri) + _dot(mid, tri) + _dot(lo, tri)
        outs.append(loc + carry)
        carry = carry + loc[:, chunk - 1:chunk]
    return jnp.concatenate(outs, axis=1)


def _forget_kernel(x_ref, w_ref, b_ref, logf_ref, *cum_refs):
    logit = _dot(x_ref[...], w_ref[...]) + b_ref[...]
    logf_t = _log_sigmoid(logit).T[:HEAD_ROWS, :]
    logf_ref[0] = logf_t
    if cum_refs:
        cum_refs[0][0] = _cumsum_lanes(logf_t, CUM_CHUNK)


def forget_gate(xn, wf, bf, n_seq, with_cumsum):
    m, d = xn.shape
    t = m // n_seq
    out_shape = [jax.ShapeDtypeStruct((n_seq, HEAD_ROWS, t), F32)]
    out_specs = [pl.BlockSpec((1, HEAD_ROWS, t), lambda s: (s, 0, 0))]
    if with_cumsum:
        out_shape = out_shape * 2
        out_specs = out_specs * 2
    vmem = 2 * _nbytes((t, d), BF16) + 2 * _nbytes((d, LANES), BF16) + 8 * _nbytes((t, LANES), F32)
    return pl.pallas_call(
        _forget_kernel,
        grid=(n_seq,),
        in_specs=[pl.BlockSpec((t, d), lambda s: (s, 0)),
                  pl.BlockSpec((d, LANES), lambda s: (0, 0)),
                  pl.BlockSpec((1, LANES), lambda s: (0, 0))],
        out_specs=out_specs,
        out_shape=out_shape,
        compiler_params=_params(1, vmem),
        name="forget_gate",
    )(xn, wf, bf)


def _pipeline_slots(steps, stages):
    vals = [None] * len(steps)

    def slot(t):
        for j, stage in enumerate(stages):
            i = t - j
            if 0 <= i < len(steps):
                vals[i] = stage(steps[i]) if j == 0 else stage(steps[i], vals[i])

    return [functools.partial(slot, t) for t in range(len(steps) + len(stages) - 1)]


def _trace_interleaved(slot_lists):
    total = max(len(slots) for slots in slot_lists)
    done = [0] * len(slot_lists)
    for t in range(total):
        for n, slots in enumerate(slot_lists):
            while done[n] * total < (t + 1) * len(slots):
                slots[done[n]]()
                done[n] += 1


def _fox_super_block(qs, h, q_ref, k_ref, v_ref, f_ref, z_ref, o_ref, m_ref, l_ref, acc_ref, *, blk, sup):
    n_sub = sup // blk
    q0 = qs * sup
    m_ref[...] = jnp.full(m_ref.shape, -jnp.inf, F32)
    l_ref[...] = jnp.zeros(l_ref.shape, F32)
    acc_ref[...] = jnp.zeros(acc_ref.shape, F32)

    def logits(step):
        kj, row_lo, diagonal = step
        q = q_ref[0, q0 + row_lo:q0 + sup, :]
        s = _dot_nt(q, k_ref[0, kj * blk:(kj + 1) * blk, :])
        s = s - f_ref[0, pl.ds(h, 1), kj:kj + 1, :].reshape(1, blk) * LOG2E
        if diagonal:
            r = lax.broadcasted_iota(jnp.int32, s.shape, 0)
            c = lax.broadcasted_iota(jnp.int32, s.shape, 1)
            s = jnp.where(c <= r, s, -jnp.inf)
        return s

    def weights(step, s):
        sl = slice(step[1], sup)
        m_old = m_ref[sl, :]
        m_new = jnp.maximum(m_old, jnp.max(s, axis=-1, keepdims=True))
        alpha = jnp.exp2(m_old - m_new)
        p = jnp.exp2(s - _widen(m_new, blk))
        m_ref[sl, :] = m_new
        l_ref[sl, :] = alpha * l_ref[sl, :] + jnp.sum(p, axis=-1, keepdims=True)
        return alpha, p.astype(BF16)

    def accumulate(step, alpha_p):
        kj, row_lo, _ = step
        sl = slice(row_lo, sup)
        alpha, p = alpha_p
        acc_ref[sl, :] = alpha * acc_ref[sl, :] + _dot(p, v_ref[0, kj * blk:(kj + 1) * blk, :])

    def finish():
        gate = _silu(z_ref[0, q0:q0 + sup, :].astype(F32))
        o_ref[0, q0:q0 + sup, :] = (acc_ref[...] / l_ref[...] * gate).astype(o_ref.dtype)

    steps = [(kj, 0, False) for kj in range(qs * n_sub)]
    steps += [(qs * n_sub + c, c * blk, True) for c in range(n_sub)]
    return _pipeline_slots(steps, [logits, weights, accumulate]), finish


def _tri_after(n):
    r = lax.broadcasted_iota(jnp.int32, (n, n), 0)
    c = lax.broadcasted_iota(jnp.int32, (n, n), 1)
    return jnp.where(r > c, 1.0, 0.0).astype(BF16)


def _sb_block(w, carry, before, tri):
    log_stay, log_take = _sb_logs(w, before)
    a = _sb_weights(log_take, _sb_suffix(log_stay, tri), carry, before)
    return a, carry + jnp.sum(log_stay, axis=-1, keepdims=True)


def _sb_logs(w, before):
    nw = -w
    soft = jnp.log(1.0 + jnp.exp2(jnp.minimum(w, nw))) * LOG2E
    log_stay = jnp.minimum(nw, 0.0) - soft
    log_take = log_stay + w
    if before is not None:
        log_stay = jnp.where(before, log_stay, 0.0)
    return log_stay, log_take


def _sb_suffix(log_stay, tri):
    return _dot(log_stay.astype(BF16), tri)


def _sb_weights(log_take, suffix, carry, before):
    a = jnp.exp2(log_take + (suffix + _widen(carry, suffix.shape[1])))
    if before is not None:
        a = jnp.where(before, a, 0.0)
    return a


def _sb_super_block(qs, q_ref, k_ref, v_ref, z_ref, o_ref, carry_ref, acc_ref, tri, *, blk, sup):
    n_sub = sup // blk
    q0 = qs * sup
    carry_ref[...] = jnp.zeros(carry_ref.shape, F32)
    acc_ref[...] = jnp.zeros(acc_ref.shape, F32)

    def mask(step):
        _, row_lo, row_hi, diagonal = step
        if not diagonal:
            return None
        r = lax.broadcasted_iota(jnp.int32, (row_hi - row_lo, blk), 0)
        c = lax.broadcasted_iota(jnp.int32, (row_hi - row_lo, blk), 1)
        return c < r

    def logits(step):
        kj, row_lo, row_hi, _ = step
        q = q_ref[0, q0 + row_lo:q0 + row_hi, :]
        return _dot_nt(q, k_ref[0, kj * blk:(kj + 1) * blk, :])

    def suffix_sums(step, w):
        sl = slice(step[1], step[2])
        log_stay, log_take = _sb_logs(w, mask(step))
        carry = carry_ref[sl, :]
        carry_ref[sl, :] = carry + jnp.sum(log_stay, axis=-1, keepdims=True)
        return log_take, _sb_suffix(log_stay, tri), carry

    def accumulate(step, parts):
        kj, row_lo, row_hi, _ = step
        sl = slice(row_lo, row_hi)
        a = _sb_weights(*parts, mask(step)).astype(BF16)
        acc_ref[sl, :] = acc_ref[sl, :] + _dot(a, v_ref[0, kj * blk:(kj + 1) * blk, :])

    def finish():
        gate = _silu(z_ref[0, q0:q0 + sup, :].astype(F32))
        o_ref[0, q0:q0 + sup, :] = (acc_ref[...] * gate).astype(o_ref.dtype)

    steps = []
    for c in reversed(range(n_sub)):
        steps.append((qs * n_sub + c, c * blk, (c + 1) * blk, True))
        if c + 1 < n_sub:
            steps.append((qs * n_sub + c, (c + 1) * blk, sup, False))
    steps += [(kj, 0, sup, False) for kj in reversed(range(qs * n_sub))]
    return _pipeline_slots(steps, [logits, suffix_sums, accumulate]), finish


def _fox_prompt_kernel(q_ref, k_ref, v_ref, f_ref, z_ref, o_ref, m_ref, l_ref, acc_ref, *, seq, blk, sup):
    h = pl.program_id(1)
    for qs in range(seq // sup):
        slots, finish = _fox_super_block(qs, h, q_ref, k_ref, v_ref, f_ref, z_ref, o_ref, m_ref, l_ref, acc_ref,
                                         blk=blk, sup=sup)
        _trace_interleaved([slots])
        finish()


def _sb_prompt_kernel(q_ref, k_ref, v_ref, z_ref, o_ref, carry_ref, acc_ref, *, seq, blk, sup):
    tri = _tri_after(blk)
    for qs in range(seq // sup):
        slots, finish = _sb_super_block(qs, q_ref, k_ref, v_ref, z_ref, o_ref, carry_ref, acc_ref, tri,
                                        blk=blk, sup=sup)
        _trace_interleaved([slots])
        finish()


def prompt_attention(kind, q, k, v, z, f4=None):
    b, s, w = k.shape
    heads = w // HEAD_DIM
    blk = f4.shape[3] if kind == "fox" else ATTN_BLOCK
    sup = min(ATTN_SUPER[kind], s)
    head_spec = pl.BlockSpec((1, s, HEAD_DIM), lambda bi, hi: (bi, 0, hi))
    part_spec = lambda part: pl.BlockSpec((1, s, HEAD_DIM), lambda bi, hi: (bi, 0, part * heads + hi))
    in_specs = [part_spec(q[1]), head_spec, head_spec]
    args = [q[0], k, v]
    col_state = pltpu.VMEM((sup, LANES), F32)
    acc_state = pltpu.VMEM((sup, HEAD_DIM), F32)
    if kind == "fox":
        in_specs.append(pl.BlockSpec((1, HEAD_ROWS, s // blk, blk), lambda bi, hi: (bi, 0, 0, 0)))
        args.append(f4)
        body = functools.partial(_fox_prompt_kernel, seq=s, blk=blk, sup=sup)
        scratch = [col_state, col_state, acc_state]
    else:
        body = functools.partial(_sb_prompt_kernel, seq=s, blk=blk, sup=sup)
        scratch = [col_state, acc_state]
    in_specs.append(part_spec(z[1]))
    args.append(z[0])
    vmem = 2 * 5 * _nbytes((s, HEAD_DIM), BF16) + 2 * _nbytes((HEAD_ROWS, s), F32)
    vmem += 3 * _nbytes((sup, LANES), F32) + 12 * _nbytes((sup, blk), F32)
    return pl.pallas_call(
        body,
        grid=(b, heads),
        in_specs=in_specs,
        out_specs=head_spec,
        out_shape=jax.ShapeDtypeStruct((b, s, w), BF16),
        scratch_shapes=scratch,
        compiler_params=_params(2, vmem),
        name=kind + "_prompt_attention",
    )(*args)


def _sample_kernel(*refs, past, t_new, keys):
    (fq, fkn, fvn, fkc, fvc, flf, fz, sq, skn, svn, skc, svc, sz,
     fo, so, fka, fva, ska, sva) = refs
    _sample_one("fox", fq, fkn, fvn, fkc, fvc, flf, fz, fo, fka, fva, past=past, t_new=t_new, keys=keys)
    _sample_one("sb", sq, skn, svn, skc, svc, None, sz, so, ska, sva, past=past, t_new=t_new, keys=keys)


def _sample_one(kind, q_ref, kn_ref, vn_ref, kc_ref, vc_ref, lf_ref, z_ref, o_ref, ka_ref, va_ref, *,
                past, t_new, keys):
    rows = N_HEADS * t_new
    width = N_HEADS * HEAD_DIM
    pad = keys - past - t_new

    for src_c, src_n, dst in ((kc_ref, kn_ref, ka_ref), (vc_ref, vn_ref, va_ref)):
        dst[0:past, :] = src_c[0].reshape(past, width).astype(BF16)
        dst[past:past + t_new, :] = src_n[0]
        dst[past + t_new:keys, :] = jnp.zeros((pad, width), BF16)

    r_q = lax.broadcasted_iota(jnp.int32, (rows, width), 0)
    c_q = lax.broadcasted_iota(jnp.int32, (rows, width), 1)
    q_tiled = jnp.concatenate([q_ref[0]] * N_HEADS, axis=0)
    same_head = _shift_div(r_q, t_new) == _shift_div(c_q, HEAD_DIM)
    q_bd = jnp.where(same_head, q_tiled, jnp.zeros_like(q_tiled))

    qk = _dot_nt(q_bd, ka_ref[...])
    r_s = lax.broadcasted_iota(jnp.int32, (rows, keys), 0)
    c_s = lax.broadcasted_iota(jnp.int32, (rows, keys), 1)
    q_pos = past + (r_s & (t_new - 1))

    if kind == "fox":
        f_all = _cumsum_lanes(lf_ref[0], LANES)
        f_rows = jnp.concatenate(
            [jnp.broadcast_to(f_all[hh:hh + 1, :], (t_new, keys)) for hh in range(N_HEADS)], axis=0)
        s = jnp.where(c_s <= q_pos, qk - f_rows * LOG2E, -jnp.inf)
        m = jnp.max(s, axis=-1, keepdims=True)
        p = jnp.exp2(s - m)
        wts = p
        norm = jnp.sum(p, axis=-1, keepdims=True)
    else:
        w = qk
        before = c_s < q_pos
        tri = _tri_after(LANES)
        carry = jnp.zeros((rows, LANES), F32)
        chunks = [None] * (keys // LANES)
        for c in reversed(range(keys // LANES)):
            sl = slice(c * LANES, (c + 1) * LANES)
            chunks[c], carry = _sb_block(w[:, sl], carry, before[:, sl], tri)
        wts = jnp.concatenate(chunks, axis=1)
        norm = None

    o_full = _dot(wts.astype(BF16), va_ref[...])
    if norm is not None:
        o_full = o_full / norm
    o = jnp.concatenate(
        [o_full[hh * t_new:(hh + 1) * t_new, hh * HEAD_DIM:(hh + 1) * HEAD_DIM] for hh in range(N_HEADS)],
        axis=1)
    o_ref[0] = (o * _silu(z_ref[0].astype(F32))).astype(o_ref.dtype)


def sample_attention(fox, sb, logf_all):
    b, t_new, w = fox[1].shape
    past = fox[3].shape[1]
    assert t_new & (t_new - 1) == 0
    keys = -(-(past + t_new) // LANES) * LANES
    new_spec = pl.BlockSpec((1, t_new, w), lambda i: (i, 0, 0))
    part_spec = lambda part: pl.BlockSpec((1, t_new, w), lambda i: (i, 0, part))
    cache_spec = pl.BlockSpec((1, past, N_HEADS, HEAD_DIM), lambda i: (i, 0, 0, 0))
    logf_spec = pl.BlockSpec((1, HEAD_ROWS, keys), lambda i: (i, 0, 0))
    mixer_specs = lambda m: [part_spec(m[0][1]), new_spec, new_spec, cache_spec, cache_spec]
    vmem = 8 * _nbytes((past, w), F32) + 4 * _nbytes((keys, w), BF16)
    vmem += 24 * _nbytes((N_HEADS * t_new, keys), F32)
    return pl.pallas_call(
        functools.partial(_sample_kernel, past=past, t_new=t_new, keys=keys),
        grid=(b,),
        in_specs=(mixer_specs(fox) + [logf_spec, part_spec(fox[5][1])]
                  + mixer_specs(sb) + [part_spec(sb[5][1])]),
        out_specs=[new_spec, new_spec],
        out_shape=[jax.ShapeDtypeStruct((b, t_new, w), BF16)] * 2,
        scratch_shapes=[pltpu.VMEM((keys, w), BF16)] * 4,
        compiler_params=_params(1, vmem),
        name="sample_attention",
    )(fox[0][0], *fox[1:5], logf_all, fox[5][0], sb[0][0], *sb[1:5], sb[5][0])


def _causal_mix_weights(w_ref):
    n_groups, chunk, _ = w_ref.shape
    r = lax.broadcasted_iota(jnp.int32, (chunk, chunk), 0)
    c = lax.broadcasted_iota(jnp.int32, (chunk, chunk), 1)
    return [jnp.where(c <= r, w_ref[g], 0.0).astype(BF16) for g in range(n_groups)]


def _layer_norm(v, g_ref, b_ref):
    mu = jnp.mean(v, axis=-1, keepdims=True)
    vc = v - mu
    var = jnp.mean(vc * vc, axis=-1, keepdims=True)
    return vc * lax.rsqrt(var + LN_EPS) * g_ref[...] + b_ref[...]


def _mix_and_gate(w_causal, vnb, bias_ref, u_ref, z_ref, rows, store):
    group_dim = vnb.shape[1] // len(w_causal)
    for g, wg in enumerate(w_causal):
        sl = slice(g * group_dim, (g + 1) * group_dim)
        mixed = _dot(wg, vnb[:, sl]) + bias_ref[:, g:g + 1]
        gate = _silu(z_ref[rows, sl].astype(F32))
        store(sl, (u_ref[rows, sl].astype(F32) * mixed * gate).astype(BF16))


def _sgu_kernel(u_ref, v_ref, z_ref, g_ref, b_ref, w_ref, bias_ref, o_ref, *vn_refs):
    chunk = w_ref.shape[1]
    w_causal = _causal_mix_weights(w_ref)
    for s in range(v_ref.shape[0] // chunk):
        rs = slice(s * chunk, (s + 1) * chunk)
        vn = _layer_norm(v_ref[rs, :].astype(F32), g_ref, b_ref)
        if vn_refs:
            vn_refs[0][rs, :] = vn

        def store(sl, val, rs=rs):
            o_ref[rs, sl] = val

        _mix_and_gate(w_causal, vn.astype(BF16), bias_ref, u_ref, z_ref, rs, store)


SGU_CHUNKS_PER_STEP = 2


def spatial_gate(uvz, ln_g, ln_b, w_mix, bias_t, emit_vn):
    m, c3 = uvz.shape
    c = c3 // 3
    n_groups, chunk, _ = w_mix.shape
    rows = SGU_CHUNKS_PER_STEP * chunk
    assert m % rows == 0
    blk = lambda j: pl.BlockSpec((rows, c), lambda i, j=j: (i, j))
    out_shape = [jax.ShapeDtypeStruct((m, c), BF16)]
    out_specs = [pl.BlockSpec((rows, c), lambda i: (i, 0))]
    if emit_vn:
        out_shape.append(jax.ShapeDtypeStruct((m, c), F32))
        out_specs.append(pl.BlockSpec((rows, c), lambda i: (i, 0)))
    vmem = 6 * _nbytes((rows, c), uvz.dtype) + 2 * _nbytes((rows, c), BF16) + 8 * _nbytes((rows, c), F32)
    vmem += 2 * _nbytes(w_mix.shape, F32)
    return pl.pallas_call(
        _sgu_kernel,
        grid=(m // rows,),
        in_specs=[blk(0), blk(1), blk(2),
                  pl.BlockSpec((1, c), lambda i: (0, 0)),
                  pl.BlockSpec((1, c), lambda i: (0, 0)),
                  pl.BlockSpec((n_groups, chunk, chunk), lambda i: (0, 0, 0)),
                  pl.BlockSpec((chunk, n_groups), lambda i: (0, 0))],
        out_specs=out_specs,
        out_shape=out_shape,
        compiler_params=_params(1, vmem),
        name="spatial_gate",
    )(uvz, uvz, uvz, ln_g.reshape(1, c), ln_b.reshape(1, c), w_mix, bias_t)


def _layer0_inputs(xn_prompt, xn_sample, w_t):
    row_sets = [([xn_prompt], BF16, None), ([xn_sample], BF16, None)]
    qz = project(row_sets, w_t, col_blocks=QZ_PARTS, w_transposed=True,
                 out_scale=(QK_SCALE, 1.0, QK_SCALE, 1.0), name="in_proj0_qz")
    kv, kv_out = ([], []), ([], [])
    for p in KV_PARTS:
        outs = project(row_sets, w_t, col_blocks=(p,), heads_out=True, w_transposed=True, name="in_proj0_kv")
        for g, out in enumerate(outs):
            kv[g].append(out[0])
            kv_out[g].append(out[1])
    return qz, kv, kv_out


def kernel(x_prompt, x_sample, cache_fox_k, cache_fox_v, cache_fox_logf, cache_sb_k, cache_sb_v,
           norm0_g, w_in0, b_forget, w_out0, norm1_g, w_in1, sgu_ln_g, sgu_ln_b, w_sp, b_sp, w_out1, final_g):
    bp, s, d = x_prompt.shape
    bs, t_new, _ = x_sample.shape
    past = cache_fox_k.shape[1]
    hp = x_prompt.reshape(bp * s, d)
    hs = x_sample.reshape(bs * t_new, d)

    n_main = 8 * MIX_WIDTH
    w_t = w_in0.T
    wf = jnp.pad(w_in0[:, n_main:], ((0, 0), (0, LANES - N_HEADS))).astype(BF16)
    bf = jnp.pad(b_forget, (0, LANES - N_HEADS)).reshape(1, LANES).astype(F32)

    xn_p = rmsnorm(hp, norm0_g, BF16)
    xn_s = rmsnorm(hs, norm0_g, BF16)
    (qz_p, qz_s), ((fk, fv, sk, sv), (gk, gv, tk, tv)), (kv_prompt, kv_sample) = _layer0_inputs(xn_p, xn_s, w_t)
    logf_t, cum_t = forget_gate(xn_p, wf, bf, bp, True)
    (logf_s,) = forget_gate(xn_s, wf, bf, 1, False)

    as_seq = lambda a: a.reshape(bp, s, -1)
    qz_p = as_seq(qz_p)
    f4 = cum_t.reshape(bp, HEAD_ROWS, s // FOX_BLOCK, FOX_BLOCK)
    mix_f = prompt_attention("fox", (qz_p, 0), as_seq(fk), as_seq(fv), (qz_p, 1), f4)
    mix_s = prompt_attention("sb", (qz_p, 2), as_seq(sk), as_seq(sv), (qz_p, 3))
    fox_logf_prompt = jnp.swapaxes(logf_t[:, :N_HEADS, :], 1, 2)

    as_new = lambda a: a.reshape(bs, t_new, -1)
    qz_s = as_new(qz_s)
    logf_new_t = jnp.swapaxes(logf_s[0, :N_HEADS, :].reshape(N_HEADS, bs, t_new), 0, 1)
    keys = -(-(past + t_new) // LANES) * LANES
    logf_all = jnp.concatenate([jnp.swapaxes(cache_fox_logf.astype(F32), 1, 2), logf_new_t], axis=2)
    logf_all = jnp.pad(logf_all, ((0, 0), (0, HEAD_ROWS - N_HEADS), (0, keys - past - t_new)))
    mix_g, mix_t = sample_attention(
        ((qz_s, 0), as_new(gk), as_new(gv), cache_fox_k, cache_fox_v, (qz_s, 1)),
        ((qz_s, 2), as_new(tk), as_new(tv), cache_sb_k, cache_sb_v, (qz_s, 3)), logf_all)
    fox_logf_sample = jnp.swapaxes(logf_new_t, 1, 2)
    flat = lambda a: a.reshape(-1, MIX_WIDTH)
    w_out0_b = w_out0.astype(BF16)
    ((hp, xn1_p),) = project([([flat(mix_f), flat(mix_s)], F32, hp)], w_out0_b, tn=d, tm=512,
                             post_norm_g=norm1_g, post_norm_also=BF16, name="out_proj0")
    ((hs, xn1_s),) = project([([flat(mix_g), flat(mix_t)], F32, hs)], w_out0_b, tn=d, tm=512,
                             post_norm_g=norm1_g, post_norm_also=BF16, name="out_proj0")

    c_width = w_in1.shape[1] // 3
    bias_t = b_sp.T.astype(F32)
    uvz_p, uvz_s = project([([xn1_p], BF16, None), ([xn1_s], F32, None)], w_in1, name="in_proj1")
    (act_p,) = spatial_gate(uvz_p, sgu_ln_g, sgu_ln_b, w_sp, bias_t, False)

    per_blk = C_CHUNK // t_new
    eye = jnp.eye(per_blk, dtype=F32)
    w_corner = w_sp[:, :t_new, :t_new]
    w_bd = jnp.einsum("ab,gts->gatbs", eye, w_corner).reshape(C_GROUPS, C_CHUNK, C_CHUNK)
    bias_bd = jnp.tile(b_sp[:, :t_new].T.astype(F32), (per_blk, 1))
    act_s, vn_s = spatial_gate(uvz_s, sgu_ln_g, sgu_ln_b, w_bd, bias_bd, True)
    w_out1_b = w_out1.astype(BF16)
    (y_prompt,) = project([([act_p], F32, hp)], w_out1_b, tn=d, tm=512, post_norm_g=final_g, name="out_proj1")
    (y_sample,) = project([([act_s], F32, hs)], w_out1_b, tn=d, tm=512, post_norm_g=final_g, name="out_proj1")
    y_prompt = y_prompt.reshape(bp, s, d)
    y_sample = y_sample.reshape(bs, t_new, d)
    fk4, fv4, sk4, sv4 = (a.reshape(bp, s, N_HEADS, HEAD_DIM) for a in kv_prompt)
    gk4, gv4, tk4, tv4 = (a.reshape(bs, t_new, N_HEADS, HEAD_DIM) for a in kv_sample)
    return (y_prompt, y_sample,
            fk4, fv4, fox_logf_prompt,
            gk4, gv4, fox_logf_sample,
            sk4, sv4,
            tk4, tv4,
            vn_s.reshape(bs, t_new, c_width))
```

```python
import functools

import jax
import jax.numpy as jnp
from jax import lax
from jax.experimental import pallas as pl
from jax.experimental.pallas import tpu as pltpu

F32 = jnp.float32
BF16 = jnp.bfloat16

HEAD_DIM = 128
N_HEADS = 8
MIX_WIDTH = N_HEADS * HEAD_DIM
C_GROUPS = 16
C_CHUNK = 128
RMS_EPS = 1e-6
LN_EPS = 1e-5
ATTN_SCALE = HEAD_DIM ** -0.5
LOG2E = 1.4426950408889634
QK_SCALE = ATTN_SCALE * LOG2E
QZ_PARTS = (0, 3, 4, 7)
KV_PARTS = (1, 2, 5, 6)

LANES = 128
ATTN_BLOCK = 256
FOX_BLOCK = 512
ATTN_SUPER = {"fox": 1024, "sb": 2048}
V7X_VMEM_BYTES = 64 * 1024 * 1024
VMEM_LIMIT_CAP = V7X_VMEM_BYTES - 8 * 1024 * 1024


def _params(n_grid_axes, vmem_bytes, flags=None):
    limit = min(int(vmem_bytes * 1.25) + (4 << 20), VMEM_LIMIT_CAP)
    return pltpu.CompilerParams(
        dimension_semantics=("arbitrary",) * n_grid_axes,
        vmem_limit_bytes=limit,
        flags=flags,
    )


def _nbytes(shape, dtype):
    n = 1
    for s in shape:
        n *= s
    return n * jnp.dtype(dtype).itemsize


def _dot(a, b):
    return jnp.dot(a, b, preferred_element_type=F32)


def _dot_nt(a, b):
    return lax.dot_general(a, b, (((1,), (1,)), ((), ())), preferred_element_type=F32)


def _log_sigmoid(x):
    return jnp.minimum(x, 0.0) - jnp.log(1.0 + jnp.exp(-jnp.abs(x)))


def _silu(x):
    return x / (1.0 + jnp.exp2(x * -LOG2E))


def _widen(x, n):
    return x if n == LANES else jnp.concatenate([x] * (n // LANES), axis=1)


def _shift_div(x, n):
    assert n & (n - 1) == 0
    return lax.shift_right_logical(x, jnp.full_like(x, n.bit_length() - 1))


def _split3(x):
    hi = x.astype(BF16)
    r1 = x - hi.astype(F32)
    mid = r1.astype(BF16)
    lo = (r1 - mid.astype(F32)).astype(BF16)
    return hi, mid, lo


def _rms_kernel(x_ref, g_ref, o_ref):
    x = x_ref[...]
    ms = jnp.mean(x * x, axis=-1, keepdims=True)
    o_ref[...] = (x * lax.rsqrt(ms + RMS_EPS) * g_ref[...]).astype(o_ref.dtype)


def rmsnorm(x, g, out_dtype, tm=512):
    m, d = x.shape
    tm = min(tm, m)
    vmem = 2 * _nbytes((tm, d), F32) + 2 * _nbytes((tm, d), out_dtype)
    return pl.pallas_call(
        _rms_kernel,
        grid=(m // tm,),
        in_specs=[pl.BlockSpec((tm, d), lambda i: (i, 0)),
                  pl.BlockSpec((1, d), lambda i: (0, 0))],
        out_specs=pl.BlockSpec((tm, d), lambda i: (i, 0)),
        out_shape=jax.ShapeDtypeStruct((m, d), out_dtype),
        compiler_params=_params(1, vmem),
        name="rmsnorm",
    )(x, g.reshape(1, d))


def _proj_kernel(*refs, k_sizes, has_res, w_transposed, heads_out, has_extra, out_scale, post_norm, out_silu):
    it = iter(refs)
    n_sets = 2 if has_extra else 1
    x_sets = [[next(it) for _ in k_sizes] for _ in range(n_sets)]
    w_ref = next(it)
    g_ref = next(it) if post_norm else None
    res_refs = [next(it) if has_res else None for _ in range(n_sets)]
    second = heads_out or post_norm == "also"
    out_sets = [(next(it), next(it) if second else None) for _ in range(n_sets)]
    if w_ref.dtype == BF16:
        wb_ref = w_ref
    else:
        wb_ref = next(it)

        @pl.when(pl.program_id(1) == 0)
        def _():
            wb_ref[...] = w_ref[...].astype(BF16)

    def row_set(x_refs, res_ref, o_ref, o4_ref):
        k0 = 0
        acc = None
        for x_ref, k in zip(x_refs, k_sizes):
            if w_transposed:
                part = _dot_nt(x_ref[...], wb_ref[:, k0:k0 + k])
            else:
                part = _dot(x_ref[...], wb_ref[k0:k0 + k, :])
            acc = part if acc is None else acc + part
            k0 += k
        if has_res:
            acc = acc + res_ref[...]
        if post_norm:
            ms = jnp.mean(acc * acc, axis=-1, keepdims=True)
            normed = acc * lax.rsqrt(ms + RMS_EPS) * g_ref[...]
            if post_norm == "also":
                o4_ref[...] = normed.astype(o4_ref.dtype)
            else:
                acc = normed
        scaled = acc if out_scale is None else acc * _pick(out_scale, pl.program_id(0))
        if out_silu:
            scaled = _silu(scaled)
        o_ref[...] = scaled.astype(o_ref.dtype)
        if heads_out:
            o4_ref[...] = acc.reshape(acc.shape[0], N_HEADS, HEAD_DIM)

    row_set(x_sets[0], res_refs[0], *out_sets[0])
    if has_extra:
        @pl.when(pl.program_id(1) == pl.num_programs(1) - 1)
        def _():
            row_set(x_sets[1], res_refs[1], *out_sets[1])


def _pick(values, j):
    out = values[0]
    for t in range(1, len(values)):
        out = jnp.where(j >= t, values[t], out)
    return out


def project(row_sets, w, *, col0=0, n_cols=None, col_blocks=None, tn=1024, tm=1024, heads_out=False,
            w_transposed=False, out_scale=None, out_silu=False, post_norm_g=None, post_norm_also=None,
            name="proj"):
    assert 1 <= len(row_sets) <= 2
    xs0, _, res0 = row_sets[0]
    m = xs0[0].shape[0]
    k_sizes = tuple(x.shape[1] for x in xs0)
    k_total = sum(k_sizes)
    k_axis, n_axis = (1, 0) if w_transposed else (0, 1)
    assert k_total == w.shape[k_axis]
    if col_blocks is None:
        n_cols = w.shape[n_axis] - col0 if n_cols is None else n_cols
        assert n_cols % tn == 0 and col0 % tn == 0
        col_blocks = tuple(col0 // tn + j for j in range(n_cols // tn))
    n_j = len(col_blocks)
    n_cols = n_j * tn
    tm = min(tm, m)
    assert m % tm == 0
    has_res = res0 is not None
    if w_transposed:
        w_block, w_index = (tn, k_total), lambda j, i: (_pick(col_blocks, j), 0)
    else:
        w_block, w_index = (k_total, tn), lambda j, i: (0, _pick(col_blocks, j))
    fixed = pl.Buffered(1)
    w_spec = pl.BlockSpec(w_block, w_index, pipeline_mode=fixed) if n_j == 1 else pl.BlockSpec(w_block, w_index)
    cast_w = w.dtype != BF16
    vmem = (1 if n_j == 1 else 2) * _nbytes(w_block, w.dtype) + (_nbytes(w_block, BF16) if cast_w else 0)

    x_specs, x_args, res_specs, res_args, out_specs, out_shape = [], [], [], [], [], []
    for s, (xs, out_dtype, res) in enumerate(row_sets):
        rows = xs[0].shape[0]
        assert tuple(x.shape[1] for x in xs) == k_sizes and (res is not None) == has_res
        if s == 0:
            tile, row_index, bufs = tm, (lambda j, i: i), 2
            x_specs += [pl.BlockSpec((tile, k), lambda j, i: (i, 0)) for k in k_sizes]
        else:
            tile, row_index, bufs = rows, (lambda j, i: 0), 1
            x_specs += [pl.BlockSpec((tile, k), lambda j, i: (0, 0), pipeline_mode=fixed) for k in k_sizes]
        x_args += list(xs)
        vmem += sum(bufs * _nbytes((tile, k), BF16) for k in k_sizes)
        if has_res:
            res_specs.append(pl.BlockSpec((tile, tn), lambda j, i, r=row_index: (r(j, i), j)))
            res_args.append(res)
            vmem += 2 * _nbytes((tile, tn), F32)
        out_specs.append(pl.BlockSpec((tile, tn), lambda j, i, r=row_index: (r(j, i), j)))
        out_shape.append(jax.ShapeDtypeStruct((rows, n_cols), out_dtype))
        vmem += 2 * _nbytes((tile, tn), out_dtype) + _nbytes((tile, tn), F32)
        if heads_out:
            assert n_cols == tn == MIX_WIDTH
            out_specs.append(pl.BlockSpec((tile, N_HEADS, HEAD_DIM), lambda j, i, r=row_index: (r(j, i), 0, 0)))
            out_shape.append(jax.ShapeDtypeStruct((rows, N_HEADS, HEAD_DIM), F32))
            vmem += 2 * _nbytes((tile, tn), F32)
        if post_norm_also is not None:
            assert not heads_out and post_norm_g is not None
            out_specs.append(pl.BlockSpec((tile, tn), lambda j, i, r=row_index: (r(j, i), j)))
            out_shape.append(jax.ShapeDtypeStruct((rows, n_cols), post_norm_also))
            vmem += 2 * _nbytes((tile, tn), post_norm_also)
    norm_specs, norm_args = [], []
    if post_norm_g is not None:
        assert n_j == 1 and col_blocks == (0,) and n_cols == w.shape[n_axis]
        norm_specs.append(pl.BlockSpec((1, n_cols), lambda j, i: (0, 0)))
        norm_args.append(post_norm_g.reshape(1, n_cols))
        vmem += 2 * _nbytes((tm, n_cols), F32)
    post_norm = None if post_norm_g is None else ("replace" if post_norm_also is None else "also")
    outs = pl.pallas_call(
        functools.partial(_proj_kernel, k_sizes=k_sizes, has_res=has_res, w_transposed=w_transposed,
                          heads_out=heads_out, has_extra=len(row_sets) == 2, out_scale=out_scale,
                          post_norm=post_norm, out_silu=out_silu),
        grid=(n_j, m // tm),
        in_specs=x_specs + [w_spec] + norm_specs + res_specs,
        out_specs=out_specs,
        out_shape=out_shape,
        scratch_shapes=[pltpu.VMEM(w_block, BF16)] if cast_w else [],
        compiler_params=_params(2, vmem),
        name=name,
    )(*x_args, w, *norm_args, *res_args)
    per_set = 2 if heads_out or post_norm == "also" else 1
    return [tuple(outs[s * per_set:(s + 1) * per_set]) if per_set == 2 else outs[s] for s in range(len(row_sets))]


CUM_CHUNK = 256
HEAD_ROWS = 16


def _tri_incl(n):
    r = lax.broadcasted_iota(jnp.int32, (n, n), 0)
    c = lax.broadcasted_iota(jnp.int32, (n, n), 1)
    return jnp.where(r <= c, 1.0, 0.0).astype(BF16)


def _cumsum_lanes(x, chunk):
    n = x.shape[1]
    tri = _tri_incl(chunk)
    carry = jnp.zeros((x.shape[0], 1), F32)
    outs = []
    for c in range(n // chunk):
        hi, mid, lo = _split3(x[:, c * chunk:(c + 1) * chunk])
        loc = _dot(hi, tri) + _dot(mid, tri) + _dot(lo, tri)
        outs.append(loc + carry)
        carry = carry + loc[:, chunk - 1:chunk]
    return jnp.concatenate(outs, axis=1)


def _forget_kernel(x_ref, w_ref, b_ref, logf_ref, *cum_refs):
    logit = _dot(x_ref[...], w_ref[...]) + b_ref[...]
    logf_t = _log_sigmoid(logit).T[:HEAD_ROWS, :]
    logf_ref[0] = logf_t
    if cum_refs:
        cum_refs[0][0] = _cumsum_lanes(logf_t, CUM_CHUNK)


def forget_gate(xn, wf, bf, n_seq, with_cumsum):
    m, d = xn.shape
    t = m // n_seq
    out_shape = [jax.ShapeDtypeStruct((n_seq, HEAD_ROWS, t), F32)]
    out_specs = [pl.BlockSpec((1, HEAD_ROWS, t), lambda s: (s, 0, 0))]
    if with_cumsum:
        out_shape = out_shape * 2
        out_specs = out_specs * 2
    vmem = 2 * _nbytes((t, d), BF16) + 2 * _nbytes((d, LANES), BF16) + 8 * _nbytes((t, LANES), F32)
    return pl.pallas_call(
        _forget_kernel,
        grid=(n_seq,),
        in_specs=[pl.BlockSpec((t, d), lambda s: (s, 0)),
                  pl.BlockSpec((d, LANES), lambda s: (0, 0)),
                  pl.BlockSpec((1, LANES), lambda s: (0, 0))],
        out_specs=out_specs,
        out_shape=out_shape,
        compiler_params=_params(1, vmem),
        name="forget_gate",
    )(xn, wf, bf)


def _pipeline_slots(steps, stages):
    vals = [None] * len(steps)

    def slot(t):
        for j, stage in enumerate(stages):
            i = t - j
            if 0 <= i < len(steps):
                vals[i] = stage(steps[i]) if j == 0 else stage(steps[i], vals[i])

    return [functools.partial(slot, t) for t in range(len(steps) + len(stages) - 1)]


def _trace_interleaved(slot_lists):
    total = max(len(slots) for slots in slot_lists)
    done = [0] * len(slot_lists)
    for t in range(total):
        for n, slots in enumerate(slot_lists):
            while done[n] * total < (t + 1) * len(slots):
                slots[done[n]]()
                done[n] += 1


def _fox_super_block(qs, h, q_ref, k_ref, v_ref, f_ref, z_ref, o_ref, m_ref, l_ref, acc_ref, *, blk, sup):
    n_sub = sup // blk
    q0 = qs * sup
    m_ref[...] = jnp.full(m_ref.shape, -jnp.inf, F32)
    l_ref[...] = jnp.zeros(l_ref.shape, F32)
    acc_ref[...] = jnp.zeros(acc_ref.shape, F32)

    def logits(step):
        kj, row_lo, diagonal = step
        q = q_ref[0, q0 + row_lo:q0 + sup, :]
        s = _dot_nt(q, k_ref[0, kj * blk:(kj + 1) * blk, :])
        s = s - f_ref[0, pl.ds(h, 1), kj:kj + 1, :].reshape(1, blk) * LOG2E
        if diagonal:
            r = lax.broadcasted_iota(jnp.int32, s.shape, 0)
            c = lax.broadcasted_iota(jnp.int32, s.shape, 1)
            s = jnp.where(c <= r, s, -jnp.inf)
        return s

    def weights(step, s):
        sl = slice(step[1], sup)
        m_old = m_ref[sl, :]
        m_new = jnp.maximum(m_old, jnp.max(s, axis=-1, keepdims=True))
        alpha = jnp.exp2(m_old - m_new)
        p = jnp.exp2(s - _widen(m_new, blk))
        m_ref[sl, :] = m_new
        l_ref[sl, :] = alpha * l_ref[sl, :] + jnp.sum(p, axis=-1, keepdims=True)
        return alpha, p.astype(BF16)

    def accumulate(step, alpha_p):
        kj, row_lo, _ = step
        sl = slice(row_lo, sup)
        alpha, p = alpha_p
        acc_ref[sl, :] = alpha * acc_ref[sl, :] + _dot(p, v_ref[0, kj * blk:(kj + 1) * blk, :])

    def finish():
        gate = _silu(z_ref[0, q0:q0 + sup, :].astype(F32))
        o_ref[0, q0:q0 + sup, :] = (acc_ref[...] / l_ref[...] * gate).astype(o_ref.dtype)

    steps = [(kj, 0, False) for kj in range(qs * n_sub)]
    steps += [(qs * n_sub + c, c * blk, True) for c in range(n_sub)]
    return _pipeline_slots(steps, [logits, weights, accumulate]), finish


def _tri_after(n):
    r = lax.broadcasted_iota(jnp.int32, (n, n), 0)
    c = lax.broadcasted_iota(jnp.int32, (n, n), 1)
    return jnp.where(r > c, 1.0, 0.0).astype(BF16)


def _sb_block(w, carry, before, tri):
    log_stay, log_take = _sb_logs(w, before)
    a = _sb_weights(log_take, _sb_suffix(log_stay, tri), carry, before)
    return a, carry + jnp.sum(log_stay, axis=-1, keepdims=True)


def _sb_logs(w, before):
    nw = -w
    soft = jnp.log(1.0 + jnp.exp2(jnp.minimum(w, nw))) * LOG2E
    log_stay = jnp.minimum(nw, 0.0) - soft
    log_take = log_stay + w
    if before is not None:
        log_stay = jnp.where(before, log_stay, 0.0)
    return log_stay, log_take


def _sb_suffix(log_stay, tri):
    return _dot(log_stay.astype(BF16), tri)


def _sb_weights(log_take, suffix, carry, before):
    a = jnp.exp2(log_take + (suffix + _widen(carry, suffix.shape[1])))
    if before is not None:
        a = jnp.where(before, a, 0.0)
    return a


def _sb_super_block(qs, q_ref, k_ref, v_ref, z_ref, o_ref, carry_ref, acc_ref, tri, *, blk, sup):
    n_sub = sup // blk
    q0 = qs * sup
    carry_ref[...] = jnp.zeros(carry_ref.shape, F32)
    acc_ref[...] = jnp.zeros(acc_ref.shape, F32)

    def mask(step):
        _, row_lo, row_hi, diagonal = step
        if not diagonal:
            return None
        r = lax.broadcasted_iota(jnp.int32, (row_hi - row_lo, blk), 0)
        c = lax.broadcasted_iota(jnp.int32, (row_hi - row_lo, blk), 1)
        return c < r

    def logits(step):
        kj, row_lo, row_hi, _ = step
        q = q_ref[0, q0 + row_lo:q0 + row_hi, :]
        return _dot_nt(q, k_ref[0, kj * blk:(kj + 1) * blk, :])

    def suffix_sums(step, w):
        sl = slice(step[1], step[2])
        log_stay, log_take = _sb_logs(w, mask(step))
        carry = carry_ref[sl, :]
        carry_ref[sl, :] = carry + jnp.sum(log_stay, axis=-1, keepdims=True)
        return log_take, _sb_suffix(log_stay, tri), carry

    def accumulate(step, parts):
        kj, row_lo, row_hi, _ = step
        sl = slice(row_lo, row_hi)
        a = _sb_weights(*parts, mask(step)).astype(BF16)
        acc_ref[sl, :] = acc_ref[sl, :] + _dot(a, v_ref[0, kj * blk:(kj + 1) * blk, :])

    def finish():
        gate = _silu(z_ref[0, q0:q0 + sup, :].astype(F32))
        o_ref[0, q0:q0 + sup, :] = (acc_ref[...] * gate).astype(o_ref.dtype)

    steps = []
    for c in reversed(range(n_sub)):
        steps.append((qs * n_sub + c, c * blk, (c + 1) * blk, True))
        if c + 1 < n_sub:
            steps.append((qs * n_sub + c, (c + 1) * blk, sup, False))
    steps += [(kj, 0, sup, False) for kj in reversed(range(qs * n_sub))]
    return _pipeline_slots(steps, [logits, suffix_sums, accumulate]), finish


def _fox_prompt_kernel(q_ref, k_ref, v_ref, f_ref, z_ref, o_ref, m_ref, l_ref, acc_ref, *, seq, blk, sup):
    h = pl.program_id(1)
    for qs in range(seq // sup):
        slots, finish = _fox_super_block(qs, h, q_ref, k_ref, v_ref, f_ref, z_ref, o_ref, m_ref, l_ref, acc_ref,
                                         blk=blk, sup=sup)
        _trace_interleaved([slots])
        finish()


def _sb_prompt_kernel(q_ref, k_ref, v_ref, z_ref, o_ref, carry_ref, acc_ref, *, seq, blk, sup):
    tri = _tri_after(blk)
    for qs in range(seq // sup):
        slots, finish = _sb_super_block(qs, q_ref, k_ref, v_ref, z_ref, o_ref, carry_ref, acc_ref, tri,
                                        blk=blk, sup=sup)
        _trace_interleaved([slots])
        finish()


def prompt_attention(kind, q, k, v, z, f4=None):
    b, s, w = k.shape
    heads = w // HEAD_DIM
    blk = f4.shape[3] if kind == "fox" else ATTN_BLOCK
    sup = min(ATTN_SUPER[kind], s)
    head_spec = pl.BlockSpec((1, s, HEAD_DIM), lambda bi, hi: (bi, 0, hi))
    part_spec = lambda part: pl.BlockSpec((1, s, HEAD_DIM), lambda bi, hi: (bi, 0, part * heads + hi))
    in_specs = [part_spec(q[1]), head_spec, head_spec]
    args = [q[0], k, v]
    col_state = pltpu.VMEM((sup, LANES), F32)
    acc_state = pltpu.VMEM((sup, HEAD_DIM), F32)
    if kind == "fox":
        in_specs.append(pl.BlockSpec((1, HEAD_ROWS, s // blk, blk), lambda bi, hi: (bi, 0, 0, 0)))
        args.append(f4)
        body = functools.partial(_fox_prompt_kernel, seq=s, blk=blk, sup=sup)
        scratch = [col_state, col_state, acc_state]
    else:
        body = functools.partial(_sb_prompt_kernel, seq=s, blk=blk, sup=sup)
        scratch = [col_state, acc_state]
    in_specs.append(part_spec(z[1]))
    args.append(z[0])
    vmem = 2 * 5 * _nbytes((s, HEAD_DIM), BF16) + 2 * _nbytes((HEAD_ROWS, s), F32)
    vmem += 3 * _nbytes((sup, LANES), F32) + 12 * _nbytes((sup, blk), F32)
    return pl.pallas_call(
        body,
        grid=(b, heads),
        in_specs=in_specs,
        out_specs=head_spec,
        out_shape=jax.ShapeDtypeStruct((b, s, w), BF16),
        scratch_shapes=scratch,
        compiler_params=_params(2, vmem),
        name=kind + "_prompt_attention",
    )(*args)


def _sample_kernel(*refs, past, t_new, keys):
    (fq, fkn, fvn, fkc, fvc, flf, fz, sq, skn, svn, skc, svc, sz,
     fo, so, fka, fva, ska, sva) = refs
    _sample_one("fox", fq, fkn, fvn, fkc, fvc, flf, fz, fo, fka, fva, past=past, t_new=t_new, keys=keys)
    _sample_one("sb", sq, skn, svn, skc, svc, None, sz, so, ska, sva, past=past, t_new=t_new, keys=keys)


def _sample_one(kind, q_ref, kn_ref, vn_ref, kc_ref, vc_ref, lf_ref, z_ref, o_ref, ka_ref, va_ref, *,
                past, t_new, keys):
    rows = N_HEADS * t_new
    width = N_HEADS * HEAD_DIM
    pad = keys - past - t_new

    for src_c, src_n, dst in ((kc_ref, kn_ref, ka_ref), (vc_ref, vn_ref, va_ref)):
        dst[0:past, :] = src_c[0].reshape(past, width).astype(BF16)
        dst[past:past + t_new, :] = src_n[0]
        dst[past + t_new:keys, :] = jnp.zeros((pad, width), BF16)

    r_q = lax.broadcasted_iota(jnp.int32, (rows, width), 0)
    c_q = lax.broadcasted_iota(jnp.int32, (rows, width), 1)
    q_tiled = jnp.concatenate([q_ref[0]] * N_HEADS, axis=0)
    same_head = _shift_div(r_q, t_new) == _shift_div(c_q, HEAD_DIM)
    q_bd = jnp.where(same_head, q_tiled, jnp.zeros_like(q_tiled))

    qk = _dot_nt(q_bd, ka_ref[...])
    r_s = lax.broadcasted_iota(jnp.int32, (rows, keys), 0)
    c_s = lax.broadcasted_iota(jnp.int32, (rows, keys), 1)
    q_pos = past + (r_s & (t_new - 1))

    if kind == "fox":
        f_all = _cumsum_lanes(lf_ref[0], LANES)
        f_rows = jnp.concatenate(
            [jnp.broadcast_to(f_all[hh:hh + 1, :], (t_new, keys)) for hh in range(N_HEADS)], axis=0)
        s = jnp.where(c_s <= q_pos, qk - f_rows * LOG2E, -jnp.inf)
        m = jnp.max(s, axis=-1, keepdims=True)
        p = jnp.exp2(s - m)
        wts = p
        norm = jnp.sum(p, axis=-1, keepdims=True)
    else:
        w = qk
        before = c_s < q_pos
        tri = _tri_after(LANES)
        carry = jnp.zeros((rows, LANES), F32)
        chunks = [None] * (keys // LANES)
        for c in reversed(range(keys // LANES)):
            sl = slice(c * LANES, (c + 1) * LANES)
            chunks[c], carry = _sb_block(w[:, sl], carry, before[:, sl], tri)
        wts = jnp.concatenate(chunks, axis=1)
        norm = None

    o_full = _dot(wts.astype(BF16), va_ref[...])
    if norm is not None:
        o_full = o_full / norm
    o = jnp.concatenate(
        [o_full[hh * t_new:(hh + 1) * t_new, hh * HEAD_DIM:(hh + 1) * HEAD_DIM] for hh in range(N_HEADS)],
        axis=1)
    o_ref[0] = (o * _silu(z_ref[0].astype(F32))).astype(o_ref.dtype)


def sample_attention(fox, sb, logf_all):
    b, t_new, w = fox[1].shape
    past = fox[3].shape[1]
    assert t_new & (t_new - 1) == 0
    keys = -(-(past + t_new) // LANES) * LANES
    new_spec = pl.BlockSpec((1, t_new, w), lambda i: (i, 0, 0))
    part_spec = lambda part: pl.BlockSpec((1, t_new, w), lambda i: (i, 0, part))
    cache_spec = pl.BlockSpec((1, past, N_HEADS, HEAD_DIM), lambda i: (i, 0, 0, 0))
    logf_spec = pl.BlockSpec((1, HEAD_ROWS, keys), lambda i: (i, 0, 0))
    mixer_specs = lambda m: [part_spec(m[0][1]), new_spec, new_spec, cache_spec, cache_spec]
    vmem = 8 * _nbytes((past, w), F32) + 4 * _nbytes((keys, w), BF16)
    vmem += 24 * _nbytes((N_HEADS * t_new, keys), F32)
    return pl.pallas_call(
        functools.partial(_sample_kernel, past=past, t_new=t_new, keys=keys),
        grid=(b,),
        in_specs=(mixer_specs(fox) + [logf_spec, part_spec(fox[5][1])]
                  + mixer_specs(sb) + [part_spec(sb[5][1])]),
        out_specs=[new_spec, new_spec],
        out_shape=[jax.ShapeDtypeStruct((b, t_new, w), BF16)] * 2,
        scratch_shapes=[pltpu.VMEM((keys, w), BF16)] * 4,
        compiler_params=_params(1, vmem),
        name="sample_attention",
    )(fox[0][0], *fox[1:5], logf_all, fox[5][0], sb[0][0], *sb[1:5], sb[5][0])


def _causal_mix_weights(w_ref):
    n_groups, chunk, _ = w_ref.shape
    r = lax.broadcasted_iota(jnp.int32, (chunk, chunk), 0)
    c = lax.broadcasted_iota(jnp.int32, (chunk, chunk), 1)
    return [jnp.where(c <= r, w_ref[g], 0.0).astype(BF16) for g in range(n_groups)]


def _layer_norm(v, g_ref, b_ref):
    mu = jnp.mean(v, axis=-1, keepdims=True)
    vc = v - mu
    var = jnp.mean(vc * vc, axis=-1, keepdims=True)
    return vc * lax.rsqrt(var + LN_EPS) * g_ref[...] + b_ref[...]


def _mix_and_gate(w_causal, vnb, bias_ref, u_ref, z_ref, rows, store):
    group_dim = vnb.shape[1] // len(w_causal)
    for g, wg in enumerate(w_causal):
        sl = slice(g * group_dim, (g + 1) * group_dim)
        mixed = _dot(wg, vnb[:, sl]) + bias_ref[:, g:g + 1]
        store(sl, (u_ref[rows, sl].astype(F32) * mixed * z_ref[rows, sl].astype(F32)).astype(BF16))


def _sgu_kernel(u_ref, v_ref, z_ref, g_ref, b_ref, w_ref, bias_ref, o_ref, *vn_refs):
    chunk = w_ref.shape[1]
    w_causal = _causal_mix_weights(w_ref)
    for s in range(v_ref.shape[0] // chunk):
        rs = slice(s * chunk, (s + 1) * chunk)
        vn = _layer_norm(v_ref[rs, :].astype(F32), g_ref, b_ref)
        if vn_refs:
            vn_refs[0][rs, :] = vn

        def store(sl, val, rs=rs):
            o_ref[rs, sl] = val

        _mix_and_gate(w_causal, vn.astype(BF16), bias_ref, u_ref, z_ref, rs, store)


SGU_STEP_BYTES = 4 << 20


def spatial_gate(uv, gate, ln_g, ln_b, w_mix, bias_t, emit_vn):
    m, c = gate.shape
    n_groups, chunk, _ = w_mix.shape
    rows = min(m, chunk * max(1, SGU_STEP_BYTES // _nbytes((chunk, c), uv.dtype)))
    assert m % rows == 0
    blk = lambda j: pl.BlockSpec((rows, c), lambda i, j=j: (i, j))
    out_shape = [jax.ShapeDtypeStruct((m, c), BF16)]
    out_specs = [pl.BlockSpec((rows, c), lambda i: (i, 0))]
    if emit_vn:
        out_shape.append(jax.ShapeDtypeStruct((m, c), F32))
        out_specs.append(pl.BlockSpec((rows, c), lambda i: (i, 0)))
    vmem = 6 * _nbytes((rows, c), uv.dtype) + 2 * _nbytes((rows, c), BF16) + 8 * _nbytes((rows, c), F32)
    vmem += 2 * _nbytes(w_mix.shape, F32)
    return pl.pallas_call(
        _sgu_kernel,
        grid=(m // rows,),
        in_specs=[blk(0), blk(1), blk(0),
                  pl.BlockSpec((1, c), lambda i: (0, 0)),
                  pl.BlockSpec((1, c), lambda i: (0, 0)),
                  pl.BlockSpec((n_groups, chunk, chunk), lambda i: (0, 0, 0)),
                  pl.BlockSpec((chunk, n_groups), lambda i: (0, 0))],
        out_specs=out_specs,
        out_shape=out_shape,
        compiler_params=_params(1, vmem),
        name="spatial_gate",
    )(uv, uv, gate, ln_g.reshape(1, c), ln_b.reshape(1, c), w_mix, bias_t)


def _layer0_inputs(xn_prompt, xn_sample, w_t):
    row_sets = [([xn_prompt], BF16, None), ([xn_sample], BF16, None)]
    qz = project(row_sets, w_t, col_blocks=QZ_PARTS, w_transposed=True,
                 out_scale=(QK_SCALE, 1.0, QK_SCALE, 1.0), name="in_proj0_qz")
    kv, kv_out = ([], []), ([], [])
    for p in KV_PARTS:
        outs = project(row_sets, w_t, col_blocks=(p,), heads_out=True, w_transposed=True, name="in_proj0_kv")
        for g, out in enumerate(outs):
            kv[g].append(out[0])
            kv_out[g].append(out[1])
    return qz, kv, kv_out


def kernel(x_prompt, x_sample, cache_fox_k, cache_fox_v, cache_fox_logf, cache_sb_k, cache_sb_v,
           norm0_g, w_in0, b_forget, w_out0, norm1_g, w_in1, sgu_ln_g, sgu_ln_b, w_sp, b_sp, w_out1, final_g):
    bp, s, d = x_prompt.shape
    bs, t_new, _ = x_sample.shape
    past = cache_fox_k.shape[1]
    hp = x_prompt.reshape(bp * s, d)
    hs = x_sample.reshape(bs * t_new, d)

    n_main = 8 * MIX_WIDTH
    w_t = w_in0.T
    wf = jnp.pad(w_in0[:, n_main:], ((0, 0), (0, LANES - N_HEADS))).astype(BF16)
    bf = jnp.pad(b_forget, (0, LANES - N_HEADS)).reshape(1, LANES).astype(F32)

    xn_p = rmsnorm(hp, norm0_g, BF16)
    xn_s = rmsnorm(hs, norm0_g, BF16)
    (qz_p, qz_s), ((fk, fv, sk, sv), (gk, gv, tk, tv)), (kv_prompt, kv_sample) = _layer0_inputs(xn_p, xn_s, w_t)
    logf_t, cum_t = forget_gate(xn_p, wf, bf, bp, True)
    (logf_s,) = forget_gate(xn_s, wf, bf, 1, False)

    as_seq = lambda a: a.reshape(bp, s, -1)
    qz_p = as_seq(qz_p)
    f4 = cum_t.reshape(bp, HEAD_ROWS, s // FOX_BLOCK, FOX_BLOCK)
    mix_f = prompt_attention("fox", (qz_p, 0), as_seq(fk), as_seq(fv), (qz_p, 1), f4)
    mix_s = prompt_attention("sb", (qz_p, 2), as_seq(sk), as_seq(sv), (qz_p, 3))
    fox_logf_prompt = jnp.swapaxes(logf_t[:, :N_HEADS, :], 1, 2)

    as_new = lambda a: a.reshape(bs, t_new, -1)
    qz_s = as_new(qz_s)
    logf_new_t = jnp.swapaxes(logf_s[0, :N_HEADS, :].reshape(N_HEADS, bs, t_new), 0, 1)
    keys = -(-(past + t_new) // LANES) * LANES
    logf_all = jnp.concatenate([jnp.swapaxes(cache_fox_logf.astype(F32), 1, 2), logf_new_t], axis=2)
    logf_all = jnp.pad(logf_all, ((0, 0), (0, HEAD_ROWS - N_HEADS), (0, keys - past - t_new)))
    mix_g, mix_t = sample_attention(
        ((qz_s, 0), as_new(gk), as_new(gv), cache_fox_k, cache_fox_v, (qz_s, 1)),
        ((qz_s, 2), as_new(tk), as_new(tv), cache_sb_k, cache_sb_v, (qz_s, 3)), logf_all)
    fox_logf_sample = jnp.swapaxes(logf_new_t, 1, 2)
    flat = lambda a: a.reshape(-1, MIX_WIDTH)
    w_out0_b = w_out0.astype(BF16)
    ((hp, xn1_p),) = project([([flat(mix_f), flat(mix_s)], F32, hp)], w_out0_b, tn=d, tm=512,
                             post_norm_g=norm1_g, post_norm_also=BF16, name="out_proj0")
    ((hs, xn1_s),) = project([([flat(mix_g), flat(mix_t)], F32, hs)], w_out0_b, tn=d, tm=512,
                             post_norm_g=norm1_g, post_norm_also=BF16, name="out_proj0")

    c_width = w_in1.shape[1] // 3
    bias_t = b_sp.T.astype(F32)
    layer1_rows = [([xn1_p], BF16, None), ([xn1_s], F32, None)]
    uv_p, uv_s = project(layer1_rows, w_in1, n_cols=2 * c_width, name="in_proj1_uv")
    gate_p, gate_s = project(layer1_rows, w_in1, col0=2 * c_width, out_silu=True, name="in_proj1_gate")
    (act_p,) = spatial_gate(uv_p, gate_p, sgu_ln_g, sgu_ln_b, w_sp, bias_t, False)

    per_blk = C_CHUNK // t_new
    eye = jnp.eye(per_blk, dtype=F32)
    w_corner = w_sp[:, :t_new, :t_new]
    w_bd = jnp.einsum("ab,gts->gatbs", eye, w_corner).reshape(C_GROUPS, C_CHUNK, C_CHUNK)
    bias_bd = jnp.tile(b_sp[:, :t_new].T.astype(F32), (per_blk, 1))
    act_s, vn_s = spatial_gate(uv_s, gate_s, sgu_ln_g, sgu_ln_b, w_bd, bias_bd, True)
    w_out1_b = w_out1.astype(BF16)
    (y_prompt,) = project([([act_p], F32, hp)], w_out1_b, tn=d, tm=512, post_norm_g=final_g, name="out_proj1")
    (y_sample,) = project([([act_s], F32, hs)], w_out1_b, tn=d, tm=512, post_norm_g=final_g, name="out_proj1")
    y_prompt = y_prompt.reshape(bp, s, d)
    y_sample = y_sample.reshape(bs, t_new, d)
    fk4, fv4, sk4, sv4 = (a.reshape(bp, s, N_HEADS, HEAD_DIM) for a in kv_prompt)
    gk4, gv4, tk4, tv4 = (a.reshape(bs, t_new, N_HEADS, HEAD_DIM) for a in kv_sample)
    return (y_prompt, y_sample,
            fk4, fv4, fox_logf_prompt,
            gk4, gv4, fox_logf_sample,
            sk4, sv4,
            tk4, tv4,
            vn_s.reshape(bs, t_new, c_width))
```

```python
import functools

import jax
import jax.numpy as jnp
from jax import lax
from jax.experimental import pallas as pl
from jax.experimental.pallas import tpu as pltpu

F32 = jnp.float32
BF16 = jnp.bfloat16

HEAD_DIM = 128
N_HEADS = 8
MIX_WIDTH = N_HEADS * HEAD_DIM
C_GROUPS = 16
C_CHUNK = 128
RMS_EPS = 1e-6
LN_EPS = 1e-5
ATTN_SCALE = HEAD_DIM ** -0.5
LOG2E = 1.4426950408889634
QK_SCALE = ATTN_SCALE * LOG2E
QZ_PARTS = (0, 3, 4, 7)
KV_PARTS = (1, 2, 5, 6)

LANES = 128
ATTN_BLOCK = 256
FOX_BLOCK = 512
ATTN_SUPER = {"fox": 1024, "sb": 2048}
V7X_VMEM_BYTES = 64 * 1024 * 1024
VMEM_LIMIT_CAP = V7X_VMEM_BYTES - 8 * 1024 * 1024


def _params(n_grid_axes, vmem_bytes, flags=None):
    limit = min(int(vmem_bytes * 1.25) + (4 << 20), VMEM_LIMIT_CAP)
    return pltpu.CompilerParams(
        dimension_semantics=("arbitrary",) * n_grid_axes,
        vmem_limit_bytes=limit,
        flags=flags,
    )


def _nbytes(shape, dtype):
    n = 1
    for s in shape:
        n *= s
    return n * jnp.dtype(dtype).itemsize


def _dot(a, b):
    return jnp.dot(a, b, preferred_element_type=F32)


def _dot_nt(a, b):
    return lax.dot_general(a, b, (((1,), (1,)), ((), ())), preferred_element_type=F32)


def _log_sigmoid(x):
    return jnp.minimum(x, 0.0) - jnp.log(1.0 + jnp.exp(-jnp.abs(x)))


def _silu(x):
    return x / (1.0 + jnp.exp2(x * -LOG2E))


def _widen(x, n):
    return x if n == LANES else jnp.concatenate([x] * (n // LANES), axis=1)


def _shift_div(x, n):
    assert n & (n - 1) == 0
    return lax.shift_right_logical(x, jnp.full_like(x, n.bit_length() - 1))


def _split3(x):
    hi = x.astype(BF16)
    r1 = x - hi.astype(F32)
    mid = r1.astype(BF16)
    lo = (r1 - mid.astype(F32)).astype(BF16)
    return hi, mid, lo


def _rms_kernel(x_ref, g_ref, o_ref):
    x = x_ref[...]
    ms = jnp.mean(x * x, axis=-1, keepdims=True)
    o_ref[...] = (x * lax.rsqrt(ms + RMS_EPS) * g_ref[...]).astype(o_ref.dtype)


def rmsnorm(x, g, out_dtype, tm=512):
    m, d = x.shape
    tm = min(tm, m)
    vmem = 2 * _nbytes((tm, d), F32) + 2 * _nbytes((tm, d), out_dtype)
    return pl.pallas_call(
        _rms_kernel,
        grid=(m // tm,),
        in_specs=[pl.BlockSpec((tm, d), lambda i: (i, 0)),
                  pl.BlockSpec((1, d), lambda i: (0, 0))],
        out_specs=pl.BlockSpec((tm, d), lambda i: (i, 0)),
        out_shape=jax.ShapeDtypeStruct((m, d), out_dtype),
        compiler_params=_params(1, vmem),
        name="rmsnorm",
    )(x, g.reshape(1, d))


EPILOGUE_ROW_CHUNK = 256
def _proj_kernel(*refs, k_sizes, has_res, w_transposed, heads_out, has_extra, out_scale, post_norm, out_silu):
    it = iter(refs)
    n_sets = 2 if has_extra else 1
    x_sets = [[next(it) for _ in k_sizes] for _ in range(n_sets)]
    w_ref = next(it)
    g_ref = next(it) if post_norm else None
    res_refs = [next(it) if has_res else None for _ in range(n_sets)]
    second = heads_out or post_norm == "also"
    out_sets = [(next(it), next(it) if second else None) for _ in range(n_sets)]
    if w_ref.dtype == BF16:
        wb_ref = w_ref
    else:
        wb_ref = next(it)

        @pl.when(pl.program_id(1) == 0)
        def _():
            wb_ref[...] = w_ref[...].astype(BF16)

    def row_chunk(x_refs, res_ref, o_ref, o4_ref, rs):
        k0 = 0
        acc = None
        for x_ref, k in zip(x_refs, k_sizes):
            if w_transposed:
                part = _dot_nt(x_ref[rs, :], wb_ref[:, k0:k0 + k])
            else:
                part = _dot(x_ref[rs, :], wb_ref[k0:k0 + k, :])
            acc = part if acc is None else acc + part
            k0 += k
        if has_res:
            acc = acc + res_ref[rs, :]
        if post_norm:
            ms = jnp.mean(acc * acc, axis=-1, keepdims=True)
            normed = acc * lax.rsqrt(ms + RMS_EPS) * g_ref[...]
            if post_norm == "also":
                o4_ref[rs, :] = normed.astype(o4_ref.dtype)
            else:
                acc = normed
        scaled = acc if out_scale is None else acc * _pick(out_scale, pl.program_id(0))
        if out_silu:
            scaled = _silu(scaled)
        o_ref[rs, :] = scaled.astype(o_ref.dtype)
        if heads_out:
            o4_ref[rs, :, :] = acc.reshape(acc.shape[0], N_HEADS, HEAD_DIM)

    def row_set(x_refs, res_ref, o_ref, o4_ref):
        rows = o_ref.shape[0]
        chunk = EPILOGUE_ROW_CHUNK if (post_norm or out_silu or heads_out) and rows % EPILOGUE_ROW_CHUNK == 0 else rows
        for c in range(rows // chunk):
            row_chunk(x_refs, res_ref, o_ref, o4_ref, slice(c * chunk, (c + 1) * chunk))

    row_set(x_sets[0], res_refs[0], *out_sets[0])
    if has_extra:
        @pl.when(pl.program_id(1) == pl.num_programs(1) - 1)
        def _():
            row_set(x_sets[1], res_refs[1], *out_sets[1])


def _pick(values, j):
    out = values[0]
    for t in range(1, len(values)):
        out = jnp.where(j >= t, values[t], out)
    return out


def project(row_sets, w, *, col0=0, n_cols=None, col_blocks=None, tn=1024, tm=1024, heads_out=False,
            w_transposed=False, out_scale=None, out_silu=False, post_norm_g=None, post_norm_also=None,
            name="proj"):
    assert 1 <= len(row_sets) <= 2
    xs0, _, res0 = row_sets[0]
    m = xs0[0].shape[0]
    k_sizes = tuple(x.shape[1] for x in xs0)
    k_total = sum(k_sizes)
    k_axis, n_axis = (1, 0) if w_transposed else (0, 1)
    assert k_total == w.shape[k_axis]
    if col_blocks is None:
        n_cols = w.shape[n_axis] - col0 if n_cols is None else n_cols
        assert n_cols % tn == 0 and col0 % tn == 0
        col_blocks = tuple(col0 // tn + j for j in range(n_cols // tn))
    n_j = len(col_blocks)
    n_cols = n_j * tn
    tm = min(tm, m)
    assert m % tm == 0
    has_res = res0 is not None
    if w_transposed:
        w_block, w_index = (tn, k_total), lambda j, i: (_pick(col_blocks, j), 0)
    else:
        w_block, w_index = (k_total, tn), lambda j, i: (0, _pick(col_blocks, j))
    fixed = pl.Buffered(1)
    w_spec = pl.BlockSpec(w_block, w_index, pipeline_mode=fixed) if n_j == 1 else pl.BlockSpec(w_block, w_index)
    cast_w = w.dtype != BF16
    vmem = (1 if n_j == 1 else 2) * _nbytes(w_block, w.dtype) + (_nbytes(w_block, BF16) if cast_w else 0)

    x_specs, x_args, res_specs, res_args, out_specs, out_shape = [], [], [], [], [], []
    for s, (xs, out_dtype, res) in enumerate(row_sets):
        rows = xs[0].shape[0]
        assert tuple(x.shape[1] for x in xs) == k_sizes and (res is not None) == has_res
        if s == 0:
            tile, row_index, bufs = tm, (lambda j, i: i), 2
            x_specs += [pl.BlockSpec((tile, k), lambda j, i: (i, 0)) for k in k_sizes]
        else:
            tile, row_index, bufs = rows, (lambda j, i: 0), 1
            x_specs += [pl.BlockSpec((tile, k), lambda j, i: (0, 0), pipeline_mode=fixed) for k in k_sizes]
        x_args += list(xs)
        vmem += sum(bufs * _nbytes((tile, k), BF16) for k in k_sizes)
        if has_res:
            res_specs.append(pl.BlockSpec((tile, tn), lambda j, i, r=row_index: (r(j, i), j)))
            res_args.append(res)
            vmem += 2 * _nbytes((tile, tn), F32)
        out_specs.append(pl.BlockSpec((tile, tn), lambda j, i, r=row_index: (r(j, i), j)))
        out_shape.append(jax.ShapeDtypeStruct((rows, n_cols), out_dtype))
        vmem += 2 * _nbytes((tile, tn), out_dtype) + _nbytes((tile, tn), F32)
        if heads_out:
            assert n_cols == tn == MIX_WIDTH
            out_specs.append(pl.BlockSpec((tile, N_HEADS, HEAD_DIM), lambda j, i, r=row_index: (r(j, i), 0, 0)))
            out_shape.append(jax.ShapeDtypeStruct((rows, N_HEADS, HEAD_DIM), F32))
            vmem += 2 * _nbytes((tile, tn), F32)
        if post_norm_also is not None:
            assert not heads_out and post_norm_g is not None
            out_specs.append(pl.BlockSpec((tile, tn), lambda j, i, r=row_index: (r(j, i), j)))
            out_shape.append(jax.ShapeDtypeStruct((rows, n_cols), post_norm_also))
            vmem += 2 * _nbytes((tile, tn), post_norm_also)
    norm_specs, norm_args = [], []
    if post_norm_g is not None:
        assert n_j == 1 and col_blocks == (0,) and n_cols == w.shape[n_axis]
        norm_specs.append(pl.BlockSpec((1, n_cols), lambda j, i: (0, 0)))
        norm_args.append(post_norm_g.reshape(1, n_cols))
        vmem += 2 * _nbytes((tm, n_cols), F32)
    post_norm = None if post_norm_g is None else ("replace" if post_norm_also is None else "also")
    outs = pl.pallas_call(
        functools.partial(_proj_kernel, k_sizes=k_sizes, has_res=has_res, w_transposed=w_transposed,
                          heads_out=heads_out, has_extra=len(row_sets) == 2, out_scale=out_scale,
                          post_norm=post_norm, out_silu=out_silu),
        grid=(n_j, m // tm),
        in_specs=x_specs + [w_spec] + norm_specs + res_specs,
        out_specs=out_specs,
        out_shape=out_shape,
        scratch_shapes=[pltpu.VMEM(w_block, BF16)] if cast_w else [],
        compiler_params=_params(2, vmem),
        name=name,
    )(*x_args, w, *norm_args, *res_args)
    per_set = 2 if heads_out or post_norm == "also" else 1
    return [tuple(outs[s * per_set:(s + 1) * per_set]) if per_set == 2 else outs[s] for s in range(len(row_sets))]


CUM_CHUNK = 256
HEAD_ROWS = 16


def _tri_incl(n):
    r = lax.broadcasted_iota(jnp.int32, (n, n), 0)
    c = lax.broadcasted_iota(jnp.int32, (n, n), 1)
    return jnp.where(r <= c, 1.0, 0.0).astype(BF16)


def _cumsum_lanes(x, chunk):
    n = x.shape[1]
    tri = _tri_incl(chunk)
    carry = jnp.zeros((x.shape[0], 1), F32)
    outs = []
    for c in range(n // chunk):
        hi, mid, lo = _split3(x[:, c * chunk:(c + 1) * chunk])
        loc = _dot(hi, tri) + _dot(mid, tri) + _dot(lo, tri)
        outs.append(loc + carry)
        carry = carry + loc[:, chunk - 1:chunk]
    return jnp.concatenate(outs, axis=1)


def _forget_kernel(x_ref, w_ref, b_ref, logf_ref, *cum_refs):
    logit = _dot(x_ref[...], w_ref[...]) + b_ref[...]
    logf_t = _log_sigmoid(logit).T[:HEAD_ROWS, :]
    logf_ref[0] = logf_t
    if cum_refs:
        cum_refs[0][0] = _cumsum_lanes(logf_t, CUM_CHUNK)


def forget_gate(xn, wf, bf, n_seq, with_cumsum):
    m, d = xn.shape
    t = m // n_seq
    out_shape = [jax.ShapeDtypeStruct((n_seq, HEAD_ROWS, t), F32)]
    out_specs = [pl.BlockSpec((1, HEAD_ROWS, t), lambda s: (s, 0, 0))]
    if with_cumsum:
        out_shape = out_shape * 2
        out_specs = out_specs * 2
    vmem = 2 * _nbytes((t, d), BF16) + 2 * _nbytes((d, LANES), BF16) + 8 * _nbytes((t, LANES), F32)
    return pl.pallas_call(
        _forget_kernel,
        grid=(n_seq,),
        in_specs=[pl.BlockSpec((t, d), lambda s: (s, 0)),
                  pl.BlockSpec((d, LANES), lambda s: (0, 0)),
                  pl.BlockSpec((1, LANES), lambda s: (0, 0))],
        out_specs=out_specs,
        out_shape=out_shape,
        compiler_params=_params(1, vmem),
        name="forget_gate",
    )(xn, wf, bf)


def _pipeline_slots(steps, stages):
    vals = [None] * len(steps)

    def slot(t):
        for j, stage in enumerate(stages):
            i = t - j
            if 0 <= i < len(steps):
                vals[i] = stage(steps[i]) if j == 0 else stage(steps[i], vals[i])

    return [functools.partial(slot, t) for t in range(len(steps) + len(stages) - 1)]


def _trace_interleaved(slot_lists):
    total = max(len(slots) for slots in slot_lists)
    done = [0] * len(slot_lists)
    for t in range(total):
        for n, slots in enumerate(slot_lists):
            while done[n] * total < (t + 1) * len(slots):
                slots[done[n]]()
                done[n] += 1


def _fox_super_block(qs, h, q_ref, k_ref, v_ref, f_ref, z_ref, o_ref, m_ref, l_ref, acc_ref, *, blk, sup):
    n_sub = sup // blk
    q0 = qs * sup
    m_ref[...] = jnp.full(m_ref.shape, -jnp.inf, F32)
    l_ref[...] = jnp.zeros(l_ref.shape, F32)
    acc_ref[...] = jnp.zeros(acc_ref.shape, F32)

    def logits(step):
        kj, row_lo, diagonal = step
        q = q_ref[0, q0 + row_lo:q0 + sup, :]
        s = _dot_nt(q, k_ref[0, kj * blk:(kj + 1) * blk, :])
        s = s - f_ref[0, pl.ds(h, 1), kj:kj + 1, :].reshape(1, blk) * LOG2E
        if diagonal:
            r = lax.broadcasted_iota(jnp.int32, s.shape, 0)
            c = lax.broadcasted_iota(jnp.int32, s.shape, 1)
            s = jnp.where(c <= r, s, -jnp.inf)
        return s

    def weights(step, s):
        sl = slice(step[1], sup)
        m_old = m_ref[sl, :]
        m_new = jnp.maximum(m_old, jnp.max(s, axis=-1, keepdims=True))
        alpha = jnp.exp2(m_old - m_new)
        p = jnp.exp2(s - _widen(m_new, blk))
        m_ref[sl, :] = m_new
        l_ref[sl, :] = alpha * l_ref[sl, :] + jnp.sum(p, axis=-1, keepdims=True)
        return alpha, p.astype(BF16)

    def accumulate(step, alpha_p):
        kj, row_lo, _ = step
        sl = slice(row_lo, sup)
        alpha, p = alpha_p
        acc_ref[sl, :] = alpha * acc_ref[sl, :] + _dot(p, v_ref[0, kj * blk:(kj + 1) * blk, :])

    def finish():
        gate = _silu(z_ref[0, q0:q0 + sup, :].astype(F32))
        o_ref[0, q0:q0 + sup, :] = (acc_ref[...] / l_ref[...] * gate).astype(o_ref.dtype)

    steps = [(kj, 0, False) for kj in range(qs * n_sub)]
    steps += [(qs * n_sub + c, c * blk, True) for c in range(n_sub)]
    return _pipeline_slots(steps, [logits, weights, accumulate]), finish


def _tri_after(n):
    r = lax.broadcasted_iota(jnp.int32, (n, n), 0)
    c = lax.broadcasted_iota(jnp.int32, (n, n), 1)
    return jnp.where(r > c, 1.0, 0.0).astype(BF16)


def _sb_block(w, carry, before, tri):
    log_stay, log_take = _sb_logs(w, before)
    a = _sb_weights(log_take, _sb_suffix(log_stay, tri), carry, before)
    return a, carry + jnp.sum(log_stay, axis=-1, keepdims=True)


def _sb_logs(w, before):
    nw = -w
    soft = jnp.log(1.0 + jnp.exp2(jnp.minimum(w, nw))) * LOG2E
    log_stay = jnp.minimum(nw, 0.0) - soft
    log_take = log_stay + w
    if before is not None:
        log_stay = jnp.where(before, log_stay, 0.0)
    return log_stay, log_take


def _sb_suffix(log_stay, tri):
    return _dot(log_stay.astype(BF16), tri)


def _sb_weights(log_take, suffix, carry, before):
    a = jnp.exp2(log_take + (suffix + _widen(carry, suffix.shape[1])))
    if before is not None:
        a = jnp.where(before, a, 0.0)
    return a


def _sb_super_block(qs, q_ref, k_ref, v_ref, z_ref, o_ref, carry_ref, acc_ref, tri, *, blk, sup):
    n_sub = sup // blk
    q0 = qs * sup
    carry_ref[...] = jnp.zeros(carry_ref.shape, F32)
    acc_ref[...] = jnp.zeros(acc_ref.shape, F32)

    def mask(step):
        _, row_lo, row_hi, diagonal = step
        if not diagonal:
            return None
        r = lax.broadcasted_iota(jnp.int32, (row_hi - row_lo, blk), 0)
        c = lax.broadcasted_iota(jnp.int32, (row_hi - row_lo, blk), 1)
        return c < r

    def logits(step):
        kj, row_lo, row_hi, _ = step
        q = q_ref[0, q0 + row_lo:q0 + row_hi, :]
        return _dot_nt(q, k_ref[0, kj * blk:(kj + 1) * blk, :])

    def suffix_sums(step, w):
        sl = slice(step[1], step[2])
        log_stay, log_take = _sb_logs(w, mask(step))
        carry = carry_ref[sl, :]
        carry_ref[sl, :] = carry + jnp.sum(log_stay, axis=-1, keepdims=True)
        return log_take, _sb_suffix(log_stay, tri), carry

    def accumulate(step, parts):
        kj, row_lo, row_hi, _ = step
        sl = slice(row_lo, row_hi)
        a = _sb_weights(*parts, mask(step)).astype(BF16)
        acc_ref[sl, :] = acc_ref[sl, :] + _dot(a, v_ref[0, kj * blk:(kj + 1) * blk, :])

    def finish():
        gate = _silu(z_ref[0, q0:q0 + sup, :].astype(F32))
        o_ref[0, q0:q0 + sup, :] = (acc_ref[...] * gate).astype(o_ref.dtype)

    steps = []
    for c in reversed(range(n_sub)):
        steps.append((qs * n_sub + c, c * blk, (c + 1) * blk, True))
        if c + 1 < n_sub:
            steps.append((qs * n_sub + c, (c + 1) * blk, sup, False))
    steps += [(kj, 0, sup, False) for kj in reversed(range(qs * n_sub))]
    return _pipeline_slots(steps, [logits, suffix_sums, accumulate]), finish


def _fox_prompt_kernel(q_ref, k_ref, v_ref, f_ref, z_ref, o_ref, m_ref, l_ref, acc_ref, *, seq, blk, sup):
    h = pl.program_id(1)
    for qs in range(seq // sup):
        slots, finish = _fox_super_block(qs, h, q_ref, k_ref, v_ref, f_ref, z_ref, o_ref, m_ref, l_ref, acc_ref,
                                         blk=blk, sup=sup)
        _trace_interleaved([slots])
        finish()


def _sb_prompt_kernel(q_ref, k_ref, v_ref, z_ref, o_ref, carry_ref, acc_ref, *, seq, blk, sup):
    tri = _tri_after(blk)
    for qs in range(seq // sup):
        slots, finish = _sb_super_block(qs, q_ref, k_ref, v_ref, z_ref, o_ref, carry_ref, acc_ref, tri,
                                        blk=blk, sup=sup)
        _trace_interleaved([slots])
        finish()


def prompt_attention(kind, q, k, v, z, f4=None):
    b, s, w = k.shape
    heads = w // HEAD_DIM
    blk = f4.shape[3] if kind == "fox" else ATTN_BLOCK
    sup = min(ATTN_SUPER[kind], s)
    head_spec = pl.BlockSpec((1, s, HEAD_DIM), lambda bi, hi: (bi, 0, hi))
    part_spec = lambda part: pl.BlockSpec((1, s, HEAD_DIM), lambda bi, hi: (bi, 0, part * heads + hi))
    in_specs = [part_spec(q[1]), head_spec, head_spec]
    args = [q[0], k, v]
    col_state = pltpu.VMEM((sup, LANES), F32)
    acc_state = pltpu.VMEM((sup, HEAD_DIM), F32)
    if kind == "fox":
        in_specs.append(pl.BlockSpec((1, HEAD_ROWS, s // blk, blk), lambda bi, hi: (bi, 0, 0, 0)))
        args.append(f4)
        body = functools.partial(_fox_prompt_kernel, seq=s, blk=blk, sup=sup)
        scratch = [col_state, col_state, acc_state]
    else:
        body = functools.partial(_sb_prompt_kernel, seq=s, blk=blk, sup=sup)
        scratch = [col_state, acc_state]
    in_specs.append(part_spec(z[1]))
    args.append(z[0])
    vmem = 2 * 5 * _nbytes((s, HEAD_DIM), BF16) + 2 * _nbytes((HEAD_ROWS, s), F32)
    vmem += 3 * _nbytes((sup, LANES), F32) + 12 * _nbytes((sup, blk), F32)
    return pl.pallas_call(
        body,
        grid=(b, heads),
        in_specs=in_specs,
        out_specs=head_spec,
        out_shape=jax.ShapeDtypeStruct((b, s, w), BF16),
        scratch_shapes=scratch,
        compiler_params=_params(2, vmem),
        name=kind + "_prompt_attention",
    )(*args)


def _sample_kernel(*refs, past, t_new, keys):
    (fq, fkn, fvn, fkc, fvc, flf, fz, sq, skn, svn, skc, svc, sz,
     fo, so, fka, fva, ska, sva) = refs
    _sample_one("fox", fq, fkn, fvn, fkc, fvc, flf, fz, fo, fka, fva, past=past, t_new=t_new, keys=keys)
    _sample_one("sb", sq, skn, svn, skc, svc, None, sz, so, ska, sva, past=past, t_new=t_new, keys=keys)


def _sample_one(kind, q_ref, kn_ref, vn_ref, kc_ref, vc_ref, lf_ref, z_ref, o_ref, ka_ref, va_ref, *,
                past, t_new, keys):
    rows = N_HEADS * t_new
    width = N_HEADS * HEAD_DIM
    pad = keys - past - t_new

    for src_c, src_n, dst in ((kc_ref, kn_ref, ka_ref), (vc_ref, vn_ref, va_ref)):
        dst[0:past, :] = src_c[0].reshape(past, width).astype(BF16)
        dst[past:past + t_new, :] = src_n[0]
        dst[past + t_new:keys, :] = jnp.zeros((pad, width), BF16)

    r_q = lax.broadcasted_iota(jnp.int32, (rows, width), 0)
    c_q = lax.broadcasted_iota(jnp.int32, (rows, width), 1)
    q_tiled = jnp.concatenate([q_ref[0]] * N_HEADS, axis=0)
    same_head = _shift_div(r_q, t_new) == _shift_div(c_q, HEAD_DIM)
    q_bd = jnp.where(same_head, q_tiled, jnp.zeros_like(q_tiled))

    qk = _dot_nt(q_bd, ka_ref[...])
    r_s = lax.broadcasted_iota(jnp.int32, (rows, keys), 0)
    c_s = lax.broadcasted_iota(jnp.int32, (rows, keys), 1)
    q_pos = past + (r_s & (t_new - 1))

    if kind == "fox":
        f_all = _cumsum_lanes(lf_ref[0], LANES)
        f_rows = jnp.concatenate(
            [jnp.broadcast_to(f_all[hh:hh + 1, :], (t_new, keys)) for hh in range(N_HEADS)], axis=0)
        s = jnp.where(c_s <= q_pos, qk - f_rows * LOG2E, -jnp.inf)
        m = jnp.max(s, axis=-1, keepdims=True)
        p = jnp.exp2(s - m)
        wts = p
        norm = jnp.sum(p, axis=-1, keepdims=True)
    else:
        w = qk
        before = c_s < q_pos
        tri = _tri_after(LANES)
        carry = jnp.zeros((rows, LANES), F32)
        chunks = [None] * (keys // LANES)
        for c in reversed(range(keys // LANES)):
            sl = slice(c * LANES, (c + 1) * LANES)
            chunks[c], carry = _sb_block(w[:, sl], carry, before[:, sl], tri)
        wts = jnp.concatenate(chunks, axis=1)
        norm = None

    o_full = _dot(wts.astype(BF16), va_ref[...])
    if norm is not None:
        o_full = o_full / norm
    o = jnp.concatenate(
        [o_full[hh * t_new:(hh + 1) * t_new, hh * HEAD_DIM:(hh + 1) * HEAD_DIM] for hh in range(N_HEADS)],
        axis=1)
    o_ref[0] = (o * _silu(z_ref[0].astype(F32))).astype(o_ref.dtype)


def sample_attention(fox, sb, logf_all):
    b, t_new, w = fox[1].shape
    past = fox[3].shape[1]
    assert t_new & (t_new - 1) == 0
    keys = -(-(past + t_new) // LANES) * LANES
    new_spec = pl.BlockSpec((1, t_new, w), lambda i: (i, 0, 0))
    part_spec = lambda part: pl.BlockSpec((1, t_new, w), lambda i: (i, 0, part))
    cache_spec = pl.BlockSpec((1, past, N_HEADS, HEAD_DIM), lambda i: (i, 0, 0, 0))
    logf_spec = pl.BlockSpec((1, HEAD_ROWS, keys), lambda i: (i, 0, 0))
    mixer_specs = lambda m: [part_spec(m[0][1]), new_spec, new_spec, cache_spec, cache_spec]
    vmem = 8 * _nbytes((past, w), F32) + 4 * _nbytes((keys, w), BF16)
    vmem += 24 * _nbytes((N_HEADS * t_new, keys), F32)
    return pl.pallas_call(
        functools.partial(_sample_kernel, past=past, t_new=t_new, keys=keys),
        grid=(b,),
        in_specs=(mixer_specs(fox) + [logf_spec, part_spec(fox[5][1])]
                  + mixer_specs(sb) + [part_spec(sb[5][1])]),
        out_specs=[new_spec, new_spec],
        out_shape=[jax.ShapeDtypeStruct((b, t_new, w), BF16)] * 2,
        scratch_shapes=[pltpu.VMEM((keys, w), BF16)] * 4,
        compiler_params=_params(1, vmem),
        name="sample_attention",
    )(fox[0][0], *fox[1:5], logf_all, fox[5][0], sb[0][0], *sb[1:5], sb[5][0])


def _causal_mix_weights(w_ref):
    n_groups, chunk, _ = w_ref.shape
    r = lax.broadcasted_iota(jnp.int32, (chunk, chunk), 0)
    c = lax.broadcasted_iota(jnp.int32, (chunk, chunk), 1)
    return [jnp.where(c <= r, w_ref[g], 0.0).astype(BF16) for g in range(n_groups)]


def _layer_norm(v, g_ref, b_ref):
    mu = jnp.mean(v, axis=-1, keepdims=True)
    vc = v - mu
    var = jnp.mean(vc * vc, axis=-1, keepdims=True)
    return vc * lax.rsqrt(var + LN_EPS) * g_ref[...] + b_ref[...]


def _mix_and_gate(w_causal, vnb, bias_ref, u_ref, z_ref, rows, store):
    group_dim = vnb.shape[1] // len(w_causal)
    for g, wg in enumerate(w_causal):
        sl = slice(g * group_dim, (g + 1) * group_dim)
        mixed = _dot(wg, vnb[:, sl]) + bias_ref[:, g:g + 1]
        store(sl, (u_ref[rows, sl].astype(F32) * mixed * z_ref[rows, sl].astype(F32)).astype(BF16))


def _sgu_kernel(u_ref, v_ref, z_ref, g_ref, b_ref, w_ref, bias_ref, o_ref, *vn_refs):
    chunk = w_ref.shape[1]
    w_causal = _causal_mix_weights(w_ref)
    for s in range(v_ref.shape[0] // chunk):
        rs = slice(s * chunk, (s + 1) * chunk)
        vn = _layer_norm(v_ref[rs, :].astype(F32), g_ref, b_ref)
        if vn_refs:
            vn_refs[0][rs, :] = vn

        def store(sl, val, rs=rs):
            o_ref[rs, sl] = val

        _mix_and_gate(w_causal, vn.astype(BF16), bias_ref, u_ref, z_ref, rs, store)


SGU_STEP_BYTES = 4 << 20


def spatial_gate(uv, gate, ln_g, ln_b, w_mix, bias_t, emit_vn):
    m, c = gate.shape
    n_groups, chunk, _ = w_mix.shape
    rows = min(m, chunk * max(1, SGU_STEP_BYTES // _nbytes((chunk, c), uv.dtype)))
    assert m % rows == 0
    blk = lambda j: pl.BlockSpec((rows, c), lambda i, j=j: (i, j))
    out_shape = [jax.ShapeDtypeStruct((m, c), BF16)]
    out_specs = [pl.BlockSpec((rows, c), lambda i: (i, 0))]
    if emit_vn:
        out_shape.append(jax.ShapeDtypeStruct((m, c), F32))
        out_specs.append(pl.BlockSpec((rows, c), lambda i: (i, 0)))
    vmem = 6 * _nbytes((rows, c), uv.dtype) + 2 * _nbytes((rows, c), BF16) + 8 * _nbytes((rows, c), F32)
    vmem += 2 * _nbytes(w_mix.shape, F32)
    return pl.pallas_call(
        _sgu_kernel,
        grid=(m // rows,),
        in_specs=[blk(0), blk(1), blk(0),
                  pl.BlockSpec((1, c), lambda i: (0, 0)),
                  pl.BlockSpec((1, c), lambda i: (0, 0)),
                  pl.BlockSpec((n_groups, chunk, chunk), lambda i: (0, 0, 0)),
                  pl.BlockSpec((chunk, n_groups), lambda i: (0, 0))],
        out_specs=out_specs,
        out_shape=out_shape,
        compiler_params=_params(1, vmem),
        name="spatial_gate",
    )(uv, uv, gate, ln_g.reshape(1, c), ln_b.reshape(1, c), w_mix, bias_t)


def _layer0_inputs(xn_prompt, xn_sample, w_t):
    row_sets = [([xn_prompt], BF16, None), ([xn_sample], BF16, None)]
    qz = project(row_sets, w_t, col_blocks=QZ_PARTS, w_transposed=True,
                 out_scale=(QK_SCALE, 1.0, QK_SCALE, 1.0), name="in_proj0_qz")
    kv, kv_out = ([], []), ([], [])
    for p in KV_PARTS:
        outs = project(row_sets, w_t, col_blocks=(p,), heads_out=True, w_transposed=True, name="in_proj0_kv")
        for g, out in enumerate(outs):
            kv[g].append(out[0])
            kv_out[g].append(out[1])
    return qz, kv, kv_out


def kernel(x_prompt, x_sample, cache_fox_k, cache_fox_v, cache_fox_logf, cache_sb_k, cache_sb_v,
           norm0_g, w_in0, b_forget, w_out0, norm1_g, w_in1, sgu_ln_g, sgu_ln_b, w_sp, b_sp, w_out1, final_g):
    bp, s, d = x_prompt.shape
    bs, t_new, _ = x_sample.shape
    past = cache_fox_k.shape[1]
    hp = x_prompt.reshape(bp * s, d)
    hs = x_sample.reshape(bs * t_new, d)

    n_main = 8 * MIX_WIDTH
    w_t = w_in0.T
    wf = jnp.pad(w_in0[:, n_main:], ((0, 0), (0, LANES - N_HEADS))).astype(BF16)
    bf = jnp.pad(b_forget, (0, LANES - N_HEADS)).reshape(1, LANES).astype(F32)

    xn_p = rmsnorm(hp, norm0_g, BF16)
    xn_s = rmsnorm(hs, norm0_g, BF16)
    (qz_p, qz_s), ((fk, fv, sk, sv), (gk, gv, tk, tv)), (kv_prompt, kv_sample) = _layer0_inputs(xn_p, xn_s, w_t)
    logf_t, cum_t = forget_gate(xn_p, wf, bf, bp, True)
    (logf_s,) = forget_gate(xn_s, wf, bf, 1, False)

    as_seq = lambda a: a.reshape(bp, s, -1)
    qz_p = as_seq(qz_p)
    f4 = cum_t.reshape(bp, HEAD_ROWS, s // FOX_BLOCK, FOX_BLOCK)
    mix_f = prompt_attention("fox", (qz_p, 0), as_seq(fk), as_seq(fv), (qz_p, 1), f4)
    mix_s = prompt_attention("sb", (qz_p, 2), as_seq(sk), as_seq(sv), (qz_p, 3))
    fox_logf_prompt = jnp.swapaxes(logf_t[:, :N_HEADS, :], 1, 2)

    as_new = lambda a: a.reshape(bs, t_new, -1)
    qz_s = as_new(qz_s)
    logf_new_t = jnp.swapaxes(logf_s[0, :N_HEADS, :].reshape(N_HEADS, bs, t_new), 0, 1)
    keys = -(-(past + t_new) // LANES) * LANES
    logf_all = jnp.concatenate([jnp.swapaxes(cache_fox_logf.astype(F32), 1, 2), logf_new_t], axis=2)
    logf_all = jnp.pad(logf_all, ((0, 0), (0, HEAD_ROWS - N_HEADS), (0, keys - past - t_new)))
    mix_g, mix_t = sample_attention(
        ((qz_s, 0), as_new(gk), as_new(gv), cache_fox_k, cache_fox_v, (qz_s, 1)),
        ((qz_s, 2), as_new(tk), as_new(tv), cache_sb_k, cache_sb_v, (qz_s, 3)), logf_all)
    fox_logf_sample = jnp.swapaxes(logf_new_t, 1, 2)
    flat = lambda a: a.reshape(-1, MIX_WIDTH)
    w_out0_b = w_out0.astype(BF16)
    ((hp, xn1_p),) = project([([flat(mix_f), flat(mix_s)], F32, hp)], w_out0_b, tn=d, tm=512,
                             post_norm_g=norm1_g, post_norm_also=BF16, name="out_proj0")
    ((hs, xn1_s),) = project([([flat(mix_g), flat(mix_t)], F32, hs)], w_out0_b, tn=d, tm=512,
                             post_norm_g=norm1_g, post_norm_also=BF16, name="out_proj0")

    c_width = w_in1.shape[1] // 3
    bias_t = b_sp.T.astype(F32)
    layer1_rows = [([xn1_p], BF16, None), ([xn1_s], F32, None)]
    uv_p, uv_s = project(layer1_rows, w_in1, n_cols=2 * c_width, name="in_proj1_uv")
    gate_p, gate_s = project(layer1_rows, w_in1, col0=2 * c_width, out_silu=True, name="in_proj1_gate")
    (act_p,) = spatial_gate(uv_p, gate_p, sgu_ln_g, sgu_ln_b, w_sp, bias_t, False)

    per_blk = C_CHUNK // t_new
    eye = jnp.eye(per_blk, dtype=F32)
    w_corner = w_sp[:, :t_new, :t_new]
    w_bd = jnp.einsum("ab,gts->gatbs", eye, w_corner).reshape(C_GROUPS, C_CHUNK, C_CHUNK)
    bias_bd = jnp.tile(b_sp[:, :t_new].T.astype(F32), (per_blk, 1))
    act_s, vn_s = spatial_gate(uv_s, gate_s, sgu_ln_g, sgu_ln_b, w_bd, bias_bd, True)
    w_out1_b = w_out1.astype(BF16)
    (y_prompt,) = project([([act_p], F32, hp)], w_out1_b, tn=d, tm=512, post_norm_g=final_g, name="out_proj1")
    (y_sample,) = project([([act_s], F32, hs)], w_out1_b, tn=d, tm=512, post_norm_g=final_g, name="out_proj1")
    y_prompt = y_prompt.reshape(bp, s, d)
    y_sample = y_sample.reshape(bs, t_new, d)
    fk4, fv4, sk4, sv4 = (a.reshape(bp, s, N_HEADS, HEAD_DIM) for a in kv_prompt)
    gk4, gv4, tk4, tv4 = (a.reshape(bs, t_new, N_HEADS, HEAD_DIM) for a in kv_sample)
    return (y_prompt, y_sample,
            fk4, fv4, fox_logf_prompt,
            gk4, gv4, fox_logf_sample,
            sk4, sv4,
            tk4, tv4,
            vn_s.reshape(bs, t_new, c_width))
```

```python
import functools

import jax
import jax.numpy as jnp
from jax import lax
from jax.experimental import pallas as pl
from jax.experimental.pallas import tpu as pltpu

F32 = jnp.float32
BF16 = jnp.bfloat16

HEAD_DIM = 128
N_HEADS = 8
MIX_WIDTH = N_HEADS * HEAD_DIM
C_GROUPS = 16
C_CHUNK = 128
RMS_EPS = 1e-6
LN_EPS = 1e-5
ATTN_SCALE = HEAD_DIM ** -0.5
LOG2E = 1.4426950408889634
QK_SCALE = ATTN_SCALE * LOG2E
QZ_PARTS = (0, 3, 4, 7)
KV_PARTS = (1, 2, 5, 6)

LANES = 128
ATTN_BLOCK = 256
FOX_BLOCK = 512
ATTN_SUPER = {"fox": 1024, "sb": 2048}
V7X_VMEM_BYTES = 64 * 1024 * 1024
VMEM_LIMIT_CAP = V7X_VMEM_BYTES - 8 * 1024 * 1024


def _params(n_grid_axes, vmem_bytes, flags=None):
    limit = min(int(vmem_bytes * 1.25) + (4 << 20), VMEM_LIMIT_CAP)
    return pltpu.CompilerParams(
        dimension_semantics=("arbitrary",) * n_grid_axes,
        vmem_limit_bytes=limit,
        flags=flags,
    )


def _nbytes(shape, dtype):
    n = 1
    for s in shape:
        n *= s
    return n * jnp.dtype(dtype).itemsize


def _dot(a, b):
    return jnp.dot(a, b, preferred_element_type=F32)


def _dot_nt(a, b):
    return lax.dot_general(a, b, (((1,), (1,)), ((), ())), preferred_element_type=F32)


def _log_sigmoid(x):
    return jnp.minimum(x, 0.0) - jnp.log(1.0 + jnp.exp(-jnp.abs(x)))


def _silu(x):
    return x / (1.0 + jnp.exp2(x * -LOG2E))


def _widen(x, n):
    return x if n == LANES else jnp.concatenate([x] * (n // LANES), axis=1)


def _shift_div(x, n):
    assert n & (n - 1) == 0
    return lax.shift_right_logical(x, jnp.full_like(x, n.bit_length() - 1))


def _split3(x):
    hi = x.astype(BF16)
    r1 = x - hi.astype(F32)
    mid = r1.astype(BF16)
    lo = (r1 - mid.astype(F32)).astype(BF16)
    return hi, mid, lo


def _rms_kernel(x_ref, g_ref, o_ref):
    x = x_ref[...]
    ms = jnp.mean(x * x, axis=-1, keepdims=True)
    o_ref[...] = (x * lax.rsqrt(ms + RMS_EPS) * g_ref[...]).astype(o_ref.dtype)


def rmsnorm(x, g, out_dtype, tm=512):
    m, d = x.shape
    tm = min(tm, m)
    vmem = 2 * _nbytes((tm, d), F32) + 2 * _nbytes((tm, d), out_dtype)
    return pl.pallas_call(
        _rms_kernel,
        grid=(m // tm,),
        in_specs=[pl.BlockSpec((tm, d), lambda i: (i, 0)),
                  pl.BlockSpec((1, d), lambda i: (0, 0))],
        out_specs=pl.BlockSpec((tm, d), lambda i: (i, 0)),
        out_shape=jax.ShapeDtypeStruct((m, d), out_dtype),
        compiler_params=_params(1, vmem),
        name="rmsnorm",
    )(x, g.reshape(1, d))


EPILOGUE_ROW_CHUNK = 256
def _proj_kernel(*refs, k_sizes, has_res, w_transposed, heads_out, has_extra, out_scale, post_norm, out_silu):
    it = iter(refs)
    n_sets = 2 if has_extra else 1
    x_sets = [[next(it) for _ in k_sizes] for _ in range(n_sets)]
    w_ref = next(it)
    g_ref = next(it) if post_norm else None
    res_refs = [next(it) if has_res else None for _ in range(n_sets)]
    second = heads_out or post_norm == "also"
    out_sets = [(next(it), next(it) if second else None) for _ in range(n_sets)]
    if w_ref.dtype == BF16:
        wb_ref = w_ref
    else:
        wb_ref = next(it)

        @pl.when(pl.program_id(1) == 0)
        def _():
            wb_ref[...] = w_ref[...].astype(BF16)

    def row_chunk(x_refs, res_ref, o_ref, o4_ref, rs):
        k0 = 0
        acc = None
        for x_ref, k in zip(x_refs, k_sizes):
            if w_transposed:
                part = _dot_nt(x_ref[rs, :], wb_ref[:, k0:k0 + k])
            else:
                part = _dot(x_ref[rs, :], wb_ref[k0:k0 + k, :])
            acc = part if acc is None else acc + part
            k0 += k
        if has_res:
            acc = acc + res_ref[rs, :]
        if post_norm:
            ms = jnp.mean(acc * acc, axis=-1, keepdims=True)
            normed = acc * lax.rsqrt(ms + RMS_EPS) * g_ref[...]
            if post_norm == "also":
                o4_ref[rs, :] = normed.astype(o4_ref.dtype)
            else:
                acc = normed
        scaled = acc if out_scale is None else acc * _pick(out_scale, pl.program_id(0))
        if out_silu:
            scaled = _silu(scaled)
        o_ref[rs, :] = scaled.astype(o_ref.dtype)
        if heads_out:
            o4_ref[rs, :, :] = acc.reshape(acc.shape[0], N_HEADS, HEAD_DIM)

    def row_set(x_refs, res_ref, o_ref, o4_ref):
        rows = o_ref.shape[0]
        chunk = EPILOGUE_ROW_CHUNK if out_silu and rows % EPILOGUE_ROW_CHUNK == 0 else rows
        for c in range(rows // chunk):
            row_chunk(x_refs, res_ref, o_ref, o4_ref, slice(c * chunk, (c + 1) * chunk))

    row_set(x_sets[0], res_refs[0], *out_sets[0])
    if has_extra:
        @pl.when(pl.program_id(1) == pl.num_programs(1) - 1)
        def _():
            row_set(x_sets[1], res_refs[1], *out_sets[1])


def _pick(values, j):
    out = values[0]
    for t in range(1, len(values)):
        out = jnp.where(j >= t, values[t], out)
    return out


def project(row_sets, w, *, col0=0, n_cols=None, col_blocks=None, tn=1024, tm=1024, heads_out=False,
            w_transposed=False, out_scale=None, out_silu=False, post_norm_g=None, post_norm_also=None,
            name="proj"):
    assert 1 <= len(row_sets) <= 2
    xs0, _, res0 = row_sets[0]
    m = xs0[0].shape[0]
    k_sizes = tuple(x.shape[1] for x in xs0)
    k_total = sum(k_sizes)
    k_axis, n_axis = (1, 0) if w_transposed else (0, 1)
    assert k_total == w.shape[k_axis]
    if col_blocks is None:
        n_cols = w.shape[n_axis] - col0 if n_cols is None else n_cols
        assert n_cols % tn == 0 and col0 % tn == 0
        col_blocks = tuple(col0 // tn + j for j in range(n_cols // tn))
    n_j = len(col_blocks)
    n_cols = n_j * tn
    tm = min(tm, m)
    assert m % tm == 0
    has_res = res0 is not None
    if w_transposed:
        w_block, w_index = (tn, k_total), lambda j, i: (_pick(col_blocks, j), 0)
    else:
        w_block, w_index = (k_total, tn), lambda j, i: (0, _pick(col_blocks, j))
    fixed = pl.Buffered(1)
    w_spec = pl.BlockSpec(w_block, w_index, pipeline_mode=fixed) if n_j == 1 else pl.BlockSpec(w_block, w_index)
    cast_w = w.dtype != BF16
    vmem = (1 if n_j == 1 else 2) * _nbytes(w_block, w.dtype) + (_nbytes(w_block, BF16) if cast_w else 0)

    x_specs, x_args, res_specs, res_args, out_specs, out_shape = [], [], [], [], [], []
    for s, (xs, out_dtype, res) in enumerate(row_sets):
        rows = xs[0].shape[0]
        assert tuple(x.shape[1] for x in xs) == k_sizes and (res is not None) == has_res
        if s == 0:
            tile, row_index, bufs = tm, (lambda j, i: i), 2
            x_specs += [pl.BlockSpec((tile, k), lambda j, i: (i, 0)) for k in k_sizes]
        else:
            tile, row_index, bufs = rows, (lambda j, i: 0), 1
            x_specs += [pl.BlockSpec((tile, k), lambda j, i: (0, 0), pipeline_mode=fixed) for k in k_sizes]
        x_args += list(xs)
        vmem += sum(bufs * _nbytes((tile, k), BF16) for k in k_sizes)
        if has_res:
            res_specs.append(pl.BlockSpec((tile, tn), lambda j, i, r=row_index: (r(j, i), j)))
            res_args.append(res)
            vmem += 2 * _nbytes((tile, tn), F32)
        out_specs.append(pl.BlockSpec((tile, tn), lambda j, i, r=row_index: (r(j, i), j)))
        out_shape.append(jax.ShapeDtypeStruct((rows, n_cols), out_dtype))
        vmem += 2 * _nbytes((tile, tn), out_dtype) + _nbytes((tile, tn), F32)
        if heads_out:
            assert n_cols == tn == MIX_WIDTH
            out_specs.append(pl.BlockSpec((tile, N_HEADS, HEAD_DIM), lambda j, i, r=row_index: (r(j, i), 0, 0)))
            out_shape.append(jax.ShapeDtypeStruct((rows, N_HEADS, HEAD_DIM), F32))
            vmem += 2 * _nbytes((tile, tn), F32)
        if post_norm_also is not None:
            assert not heads_out and post_norm_g is not None
            out_specs.append(pl.BlockSpec((tile, tn), lambda j, i, r=row_index: (r(j, i), j)))
            out_shape.append(jax.ShapeDtypeStruct((rows, n_cols), post_norm_also))
            vmem += 2 * _nbytes((tile, tn), post_norm_also)
    norm_specs, norm_args = [], []
    if post_norm_g is not None:
        assert n_j == 1 and col_blocks == (0,) and n_cols == w.shape[n_axis]
        norm_specs.append(pl.BlockSpec((1, n_cols), lambda j, i: (0, 0)))
        norm_args.append(post_norm_g.reshape(1, n_cols))
        vmem += 2 * _nbytes((tm, n_cols), F32)
    post_norm = None if post_norm_g is None else ("replace" if post_norm_also is None else "also")
    outs = pl.pallas_call(
        functools.partial(_proj_kernel, k_sizes=k_sizes, has_res=has_res, w_transposed=w_transposed,
                          heads_out=heads_out, has_extra=len(row_sets) == 2, out_scale=out_scale,
                          post_norm=post_norm, out_silu=out_silu),
        grid=(n_j, m // tm),
        in_specs=x_specs + [w_spec] + norm_specs + res_specs,
        out_specs=out_specs,
        out_shape=out_shape,
        scratch_shapes=[pltpu.VMEM(w_block, BF16)] if cast_w else [],
        compiler_params=_params(2, vmem),
        name=name,
    )(*x_args, w, *norm_args, *res_args)
    per_set = 2 if heads_out or post_norm == "also" else 1
    return [tuple(outs[s * per_set:(s + 1) * per_set]) if per_set == 2 else outs[s] for s in range(len(row_sets))]


CUM_CHUNK = 256
HEAD_ROWS = 16


def _tri_incl(n):
    r = lax.broadcasted_iota(jnp.int32, (n, n), 0)
    c = lax.broadcasted_iota(jnp.int32, (n, n), 1)
    return jnp.where(r <= c, 1.0, 0.0).astype(BF16)


def _cumsum_lanes(x, chunk):
    n = x.shape[1]
    tri = _tri_incl(chunk)
    carry = jnp.zeros((x.shape[0], 1), F32)
    outs = []
    for c in range(n // chunk):
        hi, mid, lo = _split3(x[:, c * chunk:(c + 1) * chunk])
        loc = _dot(hi, tri) + _dot(mid, tri) + _dot(lo, tri)
        outs.append(loc + carry)
        carry = carry + loc[:, chunk - 1:chunk]
    return jnp.concatenate(outs, axis=1)


def _forget_kernel(x_ref, w_ref, b_ref, logf_ref, *cum_refs):
    logit = _dot(x_ref[...], w_ref[...]) + b_ref[...]
    logf_t = _log_sigmoid(logit).T[:HEAD_ROWS, :]
    logf_ref[0] = logf_t
    if cum_refs:
        cum_refs[0][0] = _cumsum_lanes(logf_t, CUM_CHUNK)


def forget_gate(xn, wf, bf, n_seq, with_cumsum):
    m, d = xn.shape
    t = m // n_seq
    out_shape = [jax.ShapeDtypeStruct((n_seq, HEAD_ROWS, t), F32)]
    out_specs = [pl.BlockSpec((1, HEAD_ROWS, t), lambda s: (s, 0, 0))]
    if with_cumsum:
        out_shape = out_shape * 2
        out_specs = out_specs * 2
    vmem = 2 * _nbytes((t, d), BF16) + 2 * _nbytes((d, LANES), BF16) + 8 * _nbytes((t, LANES), F32)
    return pl.pallas_call(
        _forget_kernel,
        grid=(n_seq,),
        in_specs=[pl.BlockSpec((t, d), lambda s: (s, 0)),
                  pl.BlockSpec((d, LANES), lambda s: (0, 0)),
                  pl.BlockSpec((1, LANES), lambda s: (0, 0))],
        out_specs=out_specs,
        out_shape=out_shape,
        compiler_params=_params(1, vmem),
        name="forget_gate",
    )(xn, wf, bf)


def _pipeline_slots(steps, stages):
    vals = [None] * len(steps)

    def slot(t):
        for j, stage in enumerate(stages):
            i = t - j
            if 0 <= i < len(steps):
                vals[i] = stage(steps[i]) if j == 0 else stage(steps[i], vals[i])

    return [functools.partial(slot, t) for t in range(len(steps) + len(stages) - 1)]


def _trace_interleaved(slot_lists):
    total = max(len(slots) for slots in slot_lists)
    done = [0] * len(slot_lists)
    for t in range(total):
        for n, slots in enumerate(slot_lists):
            while done[n] * total < (t + 1) * len(slots):
                slots[done[n]]()
                done[n] += 1


def _fox_super_block(qs, h, q_ref, k_ref, v_ref, f_ref, z_ref, o_ref, m_ref, l_ref, acc_ref, *, blk, sup):
    n_sub = sup // blk
    q0 = qs * sup
    m_ref[...] = jnp.full(m_ref.shape, -jnp.inf, F32)
    l_ref[...] = jnp.zeros(l_ref.shape, F32)
    acc_ref[...] = jnp.zeros(acc_ref.shape, F32)

    def logits(step):
        kj, row_lo, diagonal = step
        q = q_ref[0, q0 + row_lo:q0 + sup, :]
        s = _dot_nt(q, k_ref[0, kj * blk:(kj + 1) * blk, :])
        s = s - f_ref[0, pl.ds(h, 1), kj:kj + 1, :].reshape(1, blk) * LOG2E
        if diagonal:
            r = lax.broadcasted_iota(jnp.int32, s.shape, 0)
            c = lax.broadcasted_iota(jnp.int32, s.shape, 1)
            s = jnp.where(c <= r, s, -jnp.inf)
        return s

    def weights(step, s):
        sl = slice(step[1], sup)
        m_old = m_ref[sl, :]
        m_new = jnp.maximum(m_old, jnp.max(s, axis=-1, keepdims=True))
        alpha = jnp.exp2(m_old - m_new)
        p = jnp.exp2(s - _widen(m_new, blk))
        m_ref[sl, :] = m_new
        l_ref[sl, :] = alpha * l_ref[sl, :] + jnp.sum(p, axis=-1, keepdims=True)
        return alpha, p.astype(BF16)

    def accumulate(step, alpha_p):
        kj, row_lo, _ = step
        sl = slice(row_lo, sup)
        alpha, p = alpha_p
        acc_ref[sl, :] = alpha * acc_ref[sl, :] + _dot(p, v_ref[0, kj * blk:(kj + 1) * blk, :])

    def finish():
        gate = _silu(z_ref[0, q0:q0 + sup, :].astype(F32))
        o_ref[0, q0:q0 + sup, :] = (acc_ref[...] / l_ref[...] * gate).astype(o_ref.dtype)

    steps = [(kj, 0, False) for kj in range(qs * n_sub)]
    steps += [(qs * n_sub + c, c * blk, True) for c in range(n_sub)]
    return _pipeline_slots(steps, [logits, weights, accumulate]), finish


def _tri_after(n):
    r = lax.broadcasted_iota(jnp.int32, (n, n), 0)
    c = lax.broadcasted_iota(jnp.int32, (n, n), 1)
    return jnp.where(r > c, 1.0, 0.0).astype(BF16)


def _sb_block(w, carry, before, tri):
    log_stay, log_take = _sb_logs(w, before)
    a = _sb_weights(log_take, _sb_suffix(log_stay, tri), carry, before)
    return a, carry + jnp.sum(log_stay, axis=-1, keepdims=True)


def _sb_logs(w, before):
    nw = -w
    soft = jnp.log(1.0 + jnp.exp2(jnp.minimum(w, nw))) * LOG2E
    log_stay = jnp.minimum(nw, 0.0) - soft
    log_take = log_stay + w
    if before is not None:
        log_stay = jnp.where(before, log_stay, 0.0)
    return log_stay, log_take


def _sb_suffix(log_stay, tri):
    return _dot(log_stay.astype(BF16), tri)


def _sb_weights(log_take, suffix, carry, before):
    a = jnp.exp2(log_take + (suffix + _widen(carry, suffix.shape[1])))
    if before is not None:
        a = jnp.where(before, a, 0.0)
    return a


def _sb_super_block(qs, q_ref, k_ref, v_ref, z_ref, o_ref, carry_ref, acc_ref, tri, *, blk, sup):
    n_sub = sup // blk
    q0 = qs * sup
    carry_ref[...] = jnp.zeros(carry_ref.shape, F32)
    acc_ref[...] = jnp.zeros(acc_ref.shape, F32)

    def mask(step):
        _, row_lo, row_hi, diagonal = step
        if not diagonal:
            return None
        r = lax.broadcasted_iota(jnp.int32, (row_hi - row_lo, blk), 0)
        c = lax.broadcasted_iota(jnp.int32, (row_hi - row_lo, blk), 1)
        return c < r

    def logits(step):
        kj, row_lo, row_hi, _ = step
        q = q_ref[0, q0 + row_lo:q0 + row_hi, :]
        return _dot_nt(q, k_ref[0, kj * blk:(kj + 1) * blk, :])

    def suffix_sums(step, w):
        sl = slice(step[1], step[2])
        log_stay, log_take = _sb_logs(w, mask(step))
        carry = carry_ref[sl, :]
        carry_ref[sl, :] = carry + jnp.sum(log_stay, axis=-1, keepdims=True)
        return log_take, _sb_suffix(log_stay, tri), carry

    def accumulate(step, parts):
        kj, row_lo, row_hi, _ = step
        sl = slice(row_lo, row_hi)
        a = _sb_weights(*parts, mask(step)).astype(BF16)
        acc_ref[sl, :] = acc_ref[sl, :] + _dot(a, v_ref[0, kj * blk:(kj + 1) * blk, :])

    def finish():
        gate = _silu(z_ref[0, q0:q0 + sup, :].astype(F32))
        o_ref[0, q0:q0 + sup, :] = (acc_ref[...] * gate).astype(o_ref.dtype)

    steps = []
    for c in reversed(range(n_sub)):
        steps.append((qs * n_sub + c, c * blk, (c + 1) * blk, True))
        if c + 1 < n_sub:
            steps.append((qs * n_sub + c, (c + 1) * blk, sup, False))
    steps += [(kj, 0, sup, False) for kj in reversed(range(qs * n_sub))]
    return _pipeline_slots(steps, [logits, suffix_sums, accumulate]), finish


def _fox_prompt_kernel(q_ref, k_ref, v_ref, f_ref, z_ref, o_ref, m_ref, l_ref, acc_ref, *, seq, blk, sup):
    h = pl.program_id(1)
    for qs in range(seq // sup):
        slots, finish = _fox_super_block(qs, h, q_ref, k_ref, v_ref, f_ref, z_ref, o_ref, m_ref, l_ref, acc_ref,
                                         blk=blk, sup=sup)
        _trace_interleaved([slots])
        finish()


def _sb_prompt_kernel(q_ref, k_ref, v_ref, z_ref, o_ref, carry_ref, acc_ref, *, seq, blk, sup):
    tri = _tri_after(blk)
    for qs in range(seq // sup):
        slots, finish = _sb_super_block(qs, q_ref, k_ref, v_ref, z_ref, o_ref, carry_ref, acc_ref, tri,
                                        blk=blk, sup=sup)
        _trace_interleaved([slots])
        finish()


def prompt_attention(kind, q, k, v, z, f4=None):
    b, s, w = k.shape
    heads = w // HEAD_DIM
    blk = f4.shape[3] if kind == "fox" else ATTN_BLOCK
    sup = min(ATTN_SUPER[kind], s)
    head_spec = pl.BlockSpec((1, s, HEAD_DIM), lambda bi, hi: (bi, 0, hi))
    part_spec = lambda part: pl.BlockSpec((1, s, HEAD_DIM), lambda bi, hi: (bi, 0, part * heads + hi))
    in_specs = [part_spec(q[1]), head_spec, head_spec]
    args = [q[0], k, v]
    col_state = pltpu.VMEM((sup, LANES), F32)
    acc_state = pltpu.VMEM((sup, HEAD_DIM), F32)
    if kind == "fox":
        in_specs.append(pl.BlockSpec((1, HEAD_ROWS, s // blk, blk), lambda bi, hi: (bi, 0, 0, 0)))
        args.append(f4)
        body = functools.partial(_fox_prompt_kernel, seq=s, blk=blk, sup=sup)
        scratch = [col_state, col_state, acc_state]
    else:
        body = functools.partial(_sb_prompt_kernel, seq=s, blk=blk, sup=sup)
        scratch = [col_state, acc_state]
    in_specs.append(part_spec(z[1]))
    args.append(z[0])
    vmem = 2 * 5 * _nbytes((s, HEAD_DIM), BF16) + 2 * _nbytes((HEAD_ROWS, s), F32)
    vmem += 3 * _nbytes((sup, LANES), F32) + 12 * _nbytes((sup, blk), F32)
    return pl.pallas_call(
        body,
        grid=(b, heads),
        in_specs=in_specs,
        out_specs=head_spec,
        out_shape=jax.ShapeDtypeStruct((b, s, w), BF16),
        scratch_shapes=scratch,
        compiler_params=_params(2, vmem),
        name=kind + "_prompt_attention",
    )(*args)


def _sample_kernel(*refs, past, t_new, keys):
    (fq, fkn, fvn, fkc, fvc, flf, fz, sq, skn, svn, skc, svc, sz,
     fo, so, fka, fva, ska, sva) = refs
    _sample_one("fox", fq, fkn, fvn, fkc, fvc, flf, fz, fo, fka, fva, past=past, t_new=t_new, keys=keys)
    _sample_one("sb", sq, skn, svn, skc, svc, None, sz, so, ska, sva, past=past, t_new=t_new, keys=keys)


def _sample_one(kind, q_ref, kn_ref, vn_ref, kc_ref, vc_ref, lf_ref, z_ref, o_ref, ka_ref, va_ref, *,
                past, t_new, keys):
    rows = N_HEADS * t_new
    width = N_HEADS * HEAD_DIM
    pad = keys - past - t_new

    for src_c, src_n, dst in ((kc_ref, kn_ref, ka_ref), (vc_ref, vn_ref, va_ref)):
        dst[0:past, :] = src_c[0].reshape(past, width).astype(BF16)
        dst[past:past + t_new, :] = src_n[0]
        dst[past + t_new:keys, :] = jnp.zeros((pad, width), BF16)

    r_q = lax.broadcasted_iota(jnp.int32, (rows, width), 0)
    c_q = lax.broadcasted_iota(jnp.int32, (rows, width), 1)
    q_tiled = jnp.concatenate([q_ref[0]] * N_HEADS, axis=0)
    same_head = _shift_div(r_q, t_new) == _shift_div(c_q, HEAD_DIM)
    q_bd = jnp.where(same_head, q_tiled, jnp.zeros_like(q_tiled))

    qk = _dot_nt(q_bd, ka_ref[...])
    r_s = lax.broadcasted_iota(jnp.int32, (rows, keys), 0)
    c_s = lax.broadcasted_iota(jnp.int32, (rows, keys), 1)
    q_pos = past + (r_s & (t_new - 1))

    if kind == "fox":
        f_all = _cumsum_lanes(lf_ref[0], LANES)
        f_rows = jnp.concatenate(
            [jnp.broadcast_to(f_all[hh:hh + 1, :], (t_new, keys)) for hh in range(N_HEADS)], axis=0)
        s = jnp.where(c_s <= q_pos, qk - f_rows * LOG2E, -jnp.inf)
        m = jnp.max(s, axis=-1, keepdims=True)
        p = jnp.exp2(s - m)
        wts = p
        norm = jnp.sum(p, axis=-1, keepdims=True)
    else:
        w = qk
        before = c_s < q_pos
        tri = _tri_after(LANES)
        carry = jnp.zeros((rows, LANES), F32)
        chunks = [None] * (keys // LANES)
        for c in reversed(range(keys // LANES)):
            sl = slice(c * LANES, (c + 1) * LANES)
            chunks[c], carry = _sb_block(w[:, sl], carry, before[:, sl], tri)
        wts = jnp.concatenate(chunks, axis=1)
        norm = None

    o_full = _dot(wts.astype(BF16), va_ref[...])
    if norm is not None:
        o_full = o_full / norm
    o = jnp.concatenate(
        [o_full[hh * t_new:(hh + 1) * t_new, hh * HEAD_DIM:(hh + 1) * HEAD_DIM] for hh in range(N_HEADS)],
        axis=1)
    o_ref[0] = (o * _silu(z_ref[0].astype(F32))).astype(o_ref.dtype)


def sample_attention(fox, sb, logf_all):
    b, t_new, w = fox[1].shape
    past = fox[3].shape[1]
    assert t_new & (t_new - 1) == 0
    keys = -(-(past + t_new) // LANES) * LANES
    new_spec = pl.BlockSpec((1, t_new, w), lambda i: (i, 0, 0))
    part_spec = lambda part: pl.BlockSpec((1, t_new, w), lambda i: (i, 0, part))
    cache_spec = pl.BlockSpec((1, past, N_HEADS, HEAD_DIM), lambda i: (i, 0, 0, 0))
    logf_spec = pl.BlockSpec((1, HEAD_ROWS, keys), lambda i: (i, 0, 0))
    mixer_specs = lambda m: [part_spec(m[0][1]), new_spec, new_spec, cache_spec, cache_spec]
    vmem = 8 * _nbytes((past, w), F32) + 4 * _nbytes((keys, w), BF16)
    vmem += 24 * _nbytes((N_HEADS * t_new, keys), F32)
    return pl.pallas_call(
        functools.partial(_sample_kernel, past=past, t_new=t_new, keys=keys),
        grid=(b,),
        in_specs=(mixer_specs(fox) + [logf_spec, part_spec(fox[5][1])]
                  + mixer_specs(sb) + [part_spec(sb[5][1])]),
        out_specs=[new_spec, new_spec],
        out_shape=[jax.ShapeDtypeStruct((b, t_new, w), BF16)] * 2,
        scratch_shapes=[pltpu.VMEM((keys, w), BF16)] * 4,
        compiler_params=_params(1, vmem),
        name="sample_attention",
    )(fox[0][0], *fox[1:5], logf_all, fox[5][0], sb[0][0], *sb[1:5], sb[5][0])


def _causal_mix_weights(w_ref):
    n_groups, chunk, _ = w_ref.shape
    r = lax.broadcasted_iota(jnp.int32, (chunk, chunk), 0)
    c = lax.broadcasted_iota(jnp.int32, (chunk, chunk), 1)
    return [jnp.where(c <= r, w_ref[g], 0.0).astype(BF16) for g in range(n_groups)]


def _layer_norm(v, g_ref, b_ref):
    mu = jnp.mean(v, axis=-1, keepdims=True)
    vc = v - mu
    var = jnp.mean(vc * vc, axis=-1, keepdims=True)
    return vc * lax.rsqrt(var + LN_EPS) * g_ref[...] + b_ref[...]


def _mix_and_gate(w_causal, vnb, bias_ref, u_ref, z_ref, rows, store):
    group_dim = vnb.shape[1] // len(w_causal)
    for g, wg in enumerate(w_causal):
        sl = slice(g * group_dim, (g + 1) * group_dim)
        mixed = _dot(wg, vnb[:, sl]) + bias_ref[:, g:g + 1]
        store(sl, (u_ref[rows, sl].astype(F32) * mixed * z_ref[rows, sl].astype(F32)).astype(BF16))


def _sgu_kernel(u_ref, v_ref, z_ref, g_ref, b_ref, w_ref, bias_ref, o_ref, *vn_refs):
    chunk = w_ref.shape[1]
    w_causal = _causal_mix_weights(w_ref)
    for s in range(v_ref.shape[0] // chunk):
        rs = slice(s * chunk, (s + 1) * chunk)
        vn = _layer_norm(v_ref[rs, :].astype(F32), g_ref, b_ref)
        if vn_refs:
            vn_refs[0][rs, :] = vn

        def store(sl, val, rs=rs):
            o_ref[rs, sl] = val

        _mix_and_gate(w_causal, vn.astype(BF16), bias_ref, u_ref, z_ref, rs, store)


SGU_STEP_BYTES = 4 << 20


def spatial_gate(uv, gate, ln_g, ln_b, w_mix, bias_t, emit_vn):
    m, c = gate.shape
    n_groups, chunk, _ = w_mix.shape
    rows = min(m, chunk * max(1, SGU_STEP_BYTES // _nbytes((chunk, c), uv.dtype)))
    assert m % rows == 0
    blk = lambda j: pl.BlockSpec((rows, c), lambda i, j=j: (i, j))
    out_shape = [jax.ShapeDtypeStruct((m, c), BF16)]
    out_specs = [pl.BlockSpec((rows, c), lambda i: (i, 0))]
    if emit_vn:
        out_shape.append(jax.ShapeDtypeStruct((m, c), F32))
        out_specs.append(pl.BlockSpec((rows, c), lambda i: (i, 0)))
    vmem = 6 * _nbytes((rows, c), uv.dtype) + 2 * _nbytes((rows, c), BF16) + 8 * _nbytes((rows, c), F32)
    vmem += 2 * _nbytes(w_mix.shape, F32)
    return pl.pallas_call(
        _sgu_kernel,
        grid=(m // rows,),
        in_specs=[blk(0), blk(1), blk(0),
                  pl.BlockSpec((1, c), lambda i: (0, 0)),
                  pl.BlockSpec((1, c), lambda i: (0, 0)),
                  pl.BlockSpec((n_groups, chunk, chunk), lambda i: (0, 0, 0)),
                  pl.BlockSpec((chunk, n_groups), lambda i: (0, 0))],
        out_specs=out_specs,
        out_shape=out_shape,
        compiler_params=_params(1, vmem),
        name="spatial_gate",
    )(uv, uv, gate, ln_g.reshape(1, c), ln_b.reshape(1, c), w_mix, bias_t)


def _layer0_inputs(xn_prompt, xn_sample, w_t):
    row_sets = [([xn_prompt], BF16, None), ([xn_sample], BF16, None)]
    qz = project(row_sets, w_t, col_blocks=QZ_PARTS, w_transposed=True,
                 out_scale=(QK_SCALE, 1.0, QK_SCALE, 1.0), name="in_proj0_qz")
    kv, kv_out = ([], []), ([], [])
    for p in KV_PARTS:
        outs = project(row_sets, w_t, col_blocks=(p,), heads_out=True, w_transposed=True, name="in_proj0_kv")
        for g, out in enumerate(outs):
            kv[g].append(out[0])
            kv_out[g].append(out[1])
    return qz, kv, kv_out


def kernel(x_prompt, x_sample, cache_fox_k, cache_fox_v, cache_fox_logf, cache_sb_k, cache_sb_v,
           norm0_g, w_in0, b_forget, w_out0, norm1_g, w_in1, sgu_ln_g, sgu_ln_b, w_sp, b_sp, w_out1, final_g):
    bp, s, d = x_prompt.shape
    bs, t_new, _ = x_sample.shape
    past = cache_fox_k.shape[1]
    hp = x_prompt.reshape(bp * s, d)
    hs = x_sample.reshape(bs * t_new, d)

    n_main = 8 * MIX_WIDTH
    w_t = w_in0.T
    wf = jnp.pad(w_in0[:, n_main:], ((0, 0), (0, LANES - N_HEADS))).astype(BF16)
    bf = jnp.pad(b_forget, (0, LANES - N_HEADS)).reshape(1, LANES).astype(F32)

    xn_p = rmsnorm(hp, norm0_g, BF16)
    xn_s = rmsnorm(hs, norm0_g, BF16)
    (qz_p, qz_s), ((fk, fv, sk, sv), (gk, gv, tk, tv)), (kv_prompt, kv_sample) = _layer0_inputs(xn_p, xn_s, w_t)
    logf_t, cum_t = forget_gate(xn_p, wf, bf, bp, True)
    (logf_s,) = forget_gate(xn_s, wf, bf, 1, False)

    as_seq = lambda a: a.reshape(bp, s, -1)
    qz_p = as_seq(qz_p)
    f4 = cum_t.reshape(bp, HEAD_ROWS, s // FOX_BLOCK, FOX_BLOCK)
    mix_f = prompt_attention("fox", (qz_p, 0), as_seq(fk), as_seq(fv), (qz_p, 1), f4)
    mix_s = prompt_attention("sb", (qz_p, 2), as_seq(sk), as_seq(sv), (qz_p, 3))
    fox_logf_prompt = jnp.swapaxes(logf_t[:, :N_HEADS, :], 1, 2)

    as_new = lambda a: a.reshape(bs, t_new, -1)
    qz_s = as_new(qz_s)
    logf_new_t = jnp.swapaxes(logf_s[0, :N_HEADS, :].reshape(N_HEADS, bs, t_new), 0, 1)
    keys = -(-(past + t_new) // LANES) * LANES
    logf_all = jnp.concatenate([jnp.swapaxes(cache_fox_logf.astype(F32), 1, 2), logf_new_t], axis=2)
    logf_all = jnp.pad(logf_all, ((0, 0), (0, HEAD_ROWS - N_HEADS), (0, keys - past - t_new)))
    mix_g, mix_t = sample_attention(
        ((qz_s, 0), as_new(gk), as_new(gv), cache_fox_k, cache_fox_v, (qz_s, 1)),
        ((qz_s, 2), as_new(tk), as_new(tv), cache_sb_k, cache_sb_v, (qz_s, 3)), logf_all)
    fox_logf_sample = jnp.swapaxes(logf_new_t, 1, 2)
    flat = lambda a: a.reshape(-1, MIX_WIDTH)
    w_out0_b = w_out0.astype(BF16)
    ((hp, xn1_p),) = project([([flat(mix_f), flat(mix_s)], F32, hp)], w_out0_b, tn=d, tm=512,
                             post_norm_g=norm1_g, post_norm_also=BF16, name="out_proj0")
    ((hs, xn1_s),) = project([([flat(mix_g), flat(mix_t)], F32, hs)], w_out0_b, tn=d, tm=512,
                             post_norm_g=norm1_g, post_norm_also=BF16, name="out_proj0")

    c_width = w_in1.shape[1] // 3
    bias_t = b_sp.T.astype(F32)
    layer1_rows = [([xn1_p], BF16, None), ([xn1_s], F32, None)]
    uv_p, uv_s = project(layer1_rows, w_in1, n_cols=2 * c_width, name="in_proj1_uv")
    gate_p, gate_s = project(layer1_rows, w_in1, col0=2 * c_width, out_silu=True, name="in_proj1_gate")
    (act_p,) = spatial_gate(uv_p, gate_p, sgu_ln_g, sgu_ln_b, w_sp, bias_t, False)

    per_blk = C_CHUNK // t_new
    eye = jnp.eye(per_blk, dtype=F32)
    w_corner = w_sp[:, :t_new, :t_new]
    w_bd = jnp.einsum("ab,gts->gatbs", eye, w_corner).reshape(C_GROUPS, C_CHUNK, C_CHUNK)
    bias_bd = jnp.tile(b_sp[:, :t_new].T.astype(F32), (per_blk, 1))
    act_s, vn_s = spatial_gate(uv_s, gate_s, sgu_ln_g, sgu_ln_b, w_bd, bias_bd, True)
    w_out1_b = w_out1.astype(BF16)
    (y_prompt,) = project([([act_p], F32, hp)], w_out1_b, tn=d, tm=512, post_norm_g=final_g, name="out_proj1")
    (y_sample,) = project([([act_s], F32, hs)], w_out1_b, tn=d, tm=512, post_norm_g=final_g, name="out_proj1")
    y_prompt = y_prompt.reshape(bp, s, d)
    y_sample = y_sample.reshape(bs, t_new, d)
    fk4, fv4, sk4, sv4 = (a.reshape(bp, s, N_HEADS, HEAD_DIM) for a in kv_prompt)
    gk4, gv4, tk4, tv4 = (a.reshape(bs, t_new, N_HEADS, HEAD_DIM) for a in kv_sample)
    return (y_prompt, y_sample,
            fk4, fv4, fox_logf_prompt,
            gk4, gv4, fox_logf_sample,
            sk4, sv4,
            tk4, tv4,
            vn_s.reshape(bs, t_new, c_width))
```

```python
import functools

import jax
import jax.numpy as jnp
from jax import lax
from jax.experimental import pallas as pl
from jax.experimental.pallas import tpu as pltpu

F32 = jnp.float32
BF16 = jnp.bfloat16

HEAD_DIM = 128
N_HEADS = 8
MIX_WIDTH = N_HEADS * HEAD_DIM
C_GROUPS = 16
C_CHUNK = 128
RMS_EPS = 1e-6
LN_EPS = 1e-5
ATTN_SCALE = HEAD_DIM ** -0.5
LOG2E = 1.4426950408889634
QK_SCALE = ATTN_SCALE * LOG2E
QZ_PARTS = (0, 3, 4, 7)
KV_PARTS = (1, 2, 5, 6)

LANES = 128
ATTN_BLOCK = 256
FOX_BLOCK = 512
ATTN_SUPER = {"fox": 1024, "sb": 2048}
V7X_VMEM_BYTES = 64 * 1024 * 1024
VMEM_LIMIT_CAP = V7X_VMEM_BYTES - 8 * 1024 * 1024


def _params(n_grid_axes, vmem_bytes, flags=None):
    limit = min(int(vmem_bytes * 1.25) + (4 << 20), VMEM_LIMIT_CAP)
    return pltpu.CompilerParams(
        dimension_semantics=("arbitrary",) * n_grid_axes,
        vmem_limit_bytes=limit,
        flags=flags,
    )


def _nbytes(shape, dtype):
    n = 1
    for s in shape:
        n *= s
    return n * jnp.dtype(dtype).itemsize


def _dot(a, b):
    return jnp.dot(a, b, preferred_element_type=F32)


def _dot_nt(a, b):
    return lax.dot_general(a, b, (((1,), (1,)), ((), ())), preferred_element_type=F32)


def _log_sigmoid(x):
    return jnp.minimum(x, 0.0) - jnp.log(1.0 + jnp.exp(-jnp.abs(x)))


def _silu(x):
    return x / (1.0 + jnp.exp2(x * -LOG2E))


def _widen(x, n):
    return x if n == LANES else jnp.concatenate([x] * (n // LANES), axis=1)


def _shift_div(x, n):
    assert n & (n - 1) == 0
    return lax.shift_right_logical(x, jnp.full_like(x, n.bit_length() - 1))


def _split3(x):
    hi = x.astype(BF16)
    r1 = x - hi.astype(F32)
    mid = r1.astype(BF16)
    lo = (r1 - mid.astype(F32)).astype(BF16)
    return hi, mid, lo


EPILOGUE_ROW_CHUNK = 256
def _proj_kernel(*refs, k_sizes, has_res, w_transposed, heads_out, has_extra, out_scale, post_norm, out_silu):
    it = iter(refs)
    n_sets = 2 if has_extra else 1
    x_sets = [[next(it) for _ in k_sizes] for _ in range(n_sets)]
    w_ref = next(it)
    g_ref = next(it) if post_norm else None
    res_refs = [next(it) if has_res else None for _ in range(n_sets)]
    second = heads_out or post_norm == "also"
    out_sets = [(next(it), next(it) if second else None) for _ in range(n_sets)]
    if w_ref.dtype == BF16:
        wb_ref = w_ref
    else:
        wb_ref = next(it)

        @pl.when(pl.program_id(1) == 0)
        def _():
            wb_ref[...] = w_ref[...].astype(BF16)

    def row_chunk(x_refs, res_ref, o_ref, o4_ref, rs):
        k0 = 0
        acc = None
        for x_ref, k in zip(x_refs, k_sizes):
            if w_transposed:
                part = _dot_nt(x_ref[rs, :], wb_ref[:, k0:k0 + k])
            else:
                part = _dot(x_ref[rs, :], wb_ref[k0:k0 + k, :])
            acc = part if acc is None else acc + part
            k0 += k
        if has_res:
            acc = acc + res_ref[rs, :]
        if post_norm:
            ms = jnp.mean(acc * acc, axis=-1, keepdims=True)
            normed = acc * lax.rsqrt(ms + RMS_EPS) * g_ref[...]
            if post_norm == "also":
                o4_ref[rs, :] = normed.astype(o4_ref.dtype)
            else:
                acc = normed
        scaled = acc if out_scale is None else acc * _pick(out_scale, pl.program_id(0))
        if out_silu:
            scaled = _silu(scaled)
        o_ref[rs, :] = scaled.astype(o_ref.dtype)
        if heads_out:
            o4_ref[rs, :, :] = acc.reshape(acc.shape[0], N_HEADS, HEAD_DIM)

    def row_set(x_refs, res_ref, o_ref, o4_ref):
        rows = o_ref.shape[0]
        chunk = EPILOGUE_ROW_CHUNK if out_silu and rows % EPILOGUE_ROW_CHUNK == 0 else rows
        for c in range(rows // chunk):
            row_chunk(x_refs, res_ref, o_ref, o4_ref, slice(c * chunk, (c + 1) * chunk))

    row_set(x_sets[0], res_refs[0], *out_sets[0])
    if has_extra:
        @pl.when(pl.program_id(1) == pl.num_programs(1) - 1)
        def _():
            row_set(x_sets[1], res_refs[1], *out_sets[1])


def _pick(values, j):
    out = values[0]
    for t in range(1, len(values)):
        out = jnp.where(j >= t, values[t], out)
    return out


def project(row_sets, w, *, col0=0, n_cols=None, col_blocks=None, tn=1024, tm=1024, heads_out=False,
            w_transposed=False, out_scale=None, out_silu=False, post_norm_g=None, post_norm_also=None,
            name="proj"):
    assert 1 <= len(row_sets) <= 2
    xs0, _, res0 = row_sets[0]
    m = xs0[0].shape[0]
    k_sizes = tuple(x.shape[1] for x in xs0)
    k_total = sum(k_sizes)
    k_axis, n_axis = (1, 0) if w_transposed else (0, 1)
    assert k_total == w.shape[k_axis]
    if col_blocks is None:
        n_cols = w.shape[n_axis] - col0 if n_cols is None else n_cols
        assert n_cols % tn == 0 and col0 % tn == 0
        col_blocks = tuple(col0 // tn + j for j in range(n_cols // tn))
    n_j = len(col_blocks)
    n_cols = n_j * tn
    tm = min(tm, m)
    assert m % tm == 0
    has_res = res0 is not None
    if w_transposed:
        w_block, w_index = (tn, k_total), lambda j, i: (_pick(col_blocks, j), 0)
    else:
        w_block, w_index = (k_total, tn), lambda j, i: (0, _pick(col_blocks, j))
    fixed = pl.Buffered(1)
    w_spec = pl.BlockSpec(w_block, w_index, pipeline_mode=fixed) if n_j == 1 else pl.BlockSpec(w_block, w_index)
    cast_w = w.dtype != BF16
    vmem = (1 if n_j == 1 else 2) * _nbytes(w_block, w.dtype) + (_nbytes(w_block, BF16) if cast_w else 0)

    x_specs, x_args, res_specs, res_args, out_specs, out_shape = [], [], [], [], [], []
    for s, (xs, out_dtype, res) in enumerate(row_sets):
        rows = xs[0].shape[0]
        assert tuple(x.shape[1] for x in xs) == k_sizes and (res is not None) == has_res
        if s == 0:
            tile, row_index, bufs = tm, (lambda j, i: i), 2
            x_specs += [pl.BlockSpec((tile, k), lambda j, i: (i, 0)) for k in k_sizes]
        else:
            tile, row_index, bufs = rows, (lambda j, i: 0), 1
            x_specs += [pl.BlockSpec((tile, k), lambda j, i: (0, 0), pipeline_mode=fixed) for k in k_sizes]
        x_args += list(xs)
        vmem += sum(bufs * _nbytes((tile, k), BF16) for k in k_sizes)
        if has_res:
            res_specs.append(pl.BlockSpec((tile, tn), lambda j, i, r=row_index: (r(j, i), j)))
            res_args.append(res)
            vmem += 2 * _nbytes((tile, tn), F32)
        out_specs.append(pl.BlockSpec((tile, tn), lambda j, i, r=row_index: (r(j, i), j)))
        out_shape.append(jax.ShapeDtypeStruct((rows, n_cols), out_dtype))
        vmem += 2 * _nbytes((tile, tn), out_dtype) + _nbytes((tile, tn), F32)
        if heads_out:
            assert n_cols == tn == MIX_WIDTH
            out_specs.append(pl.BlockSpec((tile, N_HEADS, HEAD_DIM), lambda j, i, r=row_index: (r(j, i), 0, 0)))
            out_shape.append(jax.ShapeDtypeStruct((rows, N_HEADS, HEAD_DIM), F32))
            vmem += 2 * _nbytes((tile, tn), F32)
        if post_norm_also is not None:
            assert not heads_out and post_norm_g is not None
            out_specs.append(pl.BlockSpec((tile, tn), lambda j, i, r=row_index: (r(j, i), j)))
            out_shape.append(jax.ShapeDtypeStruct((rows, n_cols), post_norm_also))
            vmem += 2 * _nbytes((tile, tn), post_norm_also)
    norm_specs, norm_args = [], []
    if post_norm_g is not None:
        assert n_j == 1 and col_blocks == (0,) and n_cols == w.shape[n_axis]
        norm_specs.append(pl.BlockSpec((1, n_cols), lambda j, i: (0, 0)))
        norm_args.append(post_norm_g.reshape(1, n_cols))
        vmem += 2 * _nbytes((tm, n_cols), F32)
    post_norm = None if post_norm_g is None else ("replace" if post_norm_also is None else "also")
    outs = pl.pallas_call(
        functools.partial(_proj_kernel, k_sizes=k_sizes, has_res=has_res, w_transposed=w_transposed,
                          heads_out=heads_out, has_extra=len(row_sets) == 2, out_scale=out_scale,
                          post_norm=post_norm, out_silu=out_silu),
        grid=(n_j, m // tm),
        in_specs=x_specs + [w_spec] + norm_specs + res_specs,
        out_specs=out_specs,
        out_shape=out_shape,
        scratch_shapes=[pltpu.VMEM(w_block, BF16)] if cast_w else [],
        compiler_params=_params(2, vmem),
        name=name,
    )(*x_args, w, *norm_args, *res_args)
    per_set = 2 if heads_out or post_norm == "also" else 1
    return [tuple(outs[s * per_set:(s + 1) * per_set]) if per_set == 2 else outs[s] for s in range(len(row_sets))]


CUM_CHUNK = 256
HEAD_ROWS = 16


def _tri_incl(n):
    r = lax.broadcasted_iota(jnp.int32, (n, n), 0)
    c = lax.broadcasted_iota(jnp.int32, (n, n), 1)
    return jnp.where(r <= c, 1.0, 0.0).astype(BF16)


def _cumsum_lanes(x, chunk):
    n = x.shape[1]
    tri = _tri_incl(chunk)
    carry = jnp.zeros((x.shape[0], 1), F32)
    outs = []
    for c in range(n // chunk):
        hi, mid, lo = _split3(x[:, c * chunk:(c + 1) * chunk])
        loc = _dot(hi, tri) + _dot(mid, tri) + _dot(lo, tri)
        outs.append(loc + carry)
        carry = carry + loc[:, chunk - 1:chunk]
    return jnp.concatenate(outs, axis=1)


FORGET_ROW_TILE = 512


def _norm_forget_kernel(x_ref, g_ref, w_ref, b_ref, xn_ref, logf_ref, *cum_refs):
    x = x_ref[...]
    ms = jnp.mean(x * x, axis=-1, keepdims=True)
    xn = (x * lax.rsqrt(ms + RMS_EPS) * g_ref[...]).astype(BF16)
    xn_ref[...] = xn
    logit = _dot(xn, w_ref[...]) + b_ref[...]
    logf_t = _log_sigmoid(logit).T[:HEAD_ROWS, :]
    logf_ref[0] = logf_t
    if cum_refs:
        cum_ref, carry_ref = cum_refs
        t = logf_t.shape[1]

        @pl.when(pl.program_id(1) == 0)
        def _():
            carry_ref[...] = jnp.zeros(carry_ref.shape, F32)

        cum = _cumsum_lanes(logf_t, CUM_CHUNK) + _widen(carry_ref[...], t)
        cum_ref[0] = cum
        carry_ref[...] = jnp.broadcast_to(cum[:, t - 1:t], carry_ref.shape)


def norm_forget(x, norm_g, wf, bf, n_seq, with_cumsum):
    m, d = x.shape
    t = m // n_seq
    tt = min(FORGET_ROW_TILE, t)
    steps = t // tt
    assert t % tt == 0
    head_spec = pl.BlockSpec((1, HEAD_ROWS, tt), lambda s, i: (s, 0, i))
    out_shape = [jax.ShapeDtypeStruct((m, d), BF16), jax.ShapeDtypeStruct((n_seq, HEAD_ROWS, t), F32)]
    out_specs = [pl.BlockSpec((tt, d), lambda s, i: (s * steps + i, 0)), head_spec]
    scratch = []
    if with_cumsum:
        out_shape.append(out_shape[1])
        out_specs.append(head_spec)
        scratch.append(pltpu.VMEM((HEAD_ROWS, LANES), F32))
    vmem = 5 * _nbytes((tt, d), F32) + 2 * _nbytes((tt, d), BF16) + 2 * _nbytes((d, LANES), BF16)
    vmem += 8 * _nbytes((tt, LANES), F32)
    return pl.pallas_call(
        _norm_forget_kernel,
        grid=(n_seq, steps),
        in_specs=[pl.BlockSpec((tt, d), lambda s, i: (s * steps + i, 0)),
                  pl.BlockSpec((1, d), lambda s, i: (0, 0)),
                  pl.BlockSpec((d, LANES), lambda s, i: (0, 0)),
                  pl.BlockSpec((1, LANES), lambda s, i: (0, 0))],
        out_specs=out_specs,
        out_shape=out_shape,
        scratch_shapes=scratch,
        compiler_params=_params(2, vmem),
        name="norm_forget",
    )(x, norm_g.reshape(1, d), wf, bf)


def _pipeline_slots(steps, stages):
    vals = [None] * len(steps)

    def slot(t):
        for j, stage in enumerate(stages):
            i = t - j
            if 0 <= i < len(steps):
                vals[i] = stage(steps[i]) if j == 0 else stage(steps[i], vals[i])

    return [functools.partial(slot, t) for t in range(len(steps) + len(stages) - 1)]


def _trace_interleaved(slot_lists):
    total = max(len(slots) for slots in slot_lists)
    done = [0] * len(slot_lists)
    for t in range(total):
        for n, slots in enumerate(slot_lists):
            while done[n] * total < (t + 1) * len(slots):
                slots[done[n]]()
                done[n] += 1


def _fox_super_block(qs, h, q_ref, k_ref, v_ref, f_ref, z_ref, o_ref, m_ref, l_ref, acc_ref, *, blk, sup):
    n_sub = sup // blk
    q0 = qs * sup
    m_ref[...] = jnp.full(m_ref.shape, -jnp.inf, F32)
    l_ref[...] = jnp.zeros(l_ref.shape, F32)
    acc_ref[...] = jnp.zeros(acc_ref.shape, F32)

    def logits(step):
        kj, row_lo, diagonal = step
        q = q_ref[0, q0 + row_lo:q0 + sup, :]
        s = _dot_nt(q, k_ref[0, kj * blk:(kj + 1) * blk, :])
        s = s - f_ref[0, pl.ds(h, 1), kj:kj + 1, :].reshape(1, blk) * LOG2E
        if diagonal:
            r = lax.broadcasted_iota(jnp.int32, s.shape, 0)
            c = lax.broadcasted_iota(jnp.int32, s.shape, 1)
            s = jnp.where(c <= r, s, -jnp.inf)
        return s

    def weights(step, s):
        sl = slice(step[1], sup)
        m_old = m_ref[sl, :]
        m_new = jnp.maximum(m_old, jnp.max(s, axis=-1, keepdims=True))
        alpha = jnp.exp2(m_old - m_new)
        p = jnp.exp2(s - _widen(m_new, blk))
        m_ref[sl, :] = m_new
        l_ref[sl, :] = alpha * l_ref[sl, :] + jnp.sum(p, axis=-1, keepdims=True)
        return alpha, p.astype(BF16)

    def accumulate(step, alpha_p):
        kj, row_lo, _ = step
        sl = slice(row_lo, sup)
        alpha, p = alpha_p
        acc_ref[sl, :] = alpha * acc_ref[sl, :] + _dot(p, v_ref[0, kj * blk:(kj + 1) * blk, :])

    def finish():
        gate = _silu(z_ref[0, q0:q0 + sup, :].astype(F32))
        o_ref[0, q0:q0 + sup, :] = (acc_ref[...] / l_ref[...] * gate).astype(o_ref.dtype)

    steps = [(kj, 0, False) for kj in range(qs * n_sub)]
    steps += [(qs * n_sub + c, c * blk, True) for c in range(n_sub)]
    return _pipeline_slots(steps, [logits, weights, accumulate]), finish


def _tri_after(n):
    r = lax.broadcasted_iota(jnp.int32, (n, n), 0)
    c = lax.broadcasted_iota(jnp.int32, (n, n), 1)
    return jnp.where(r > c, 1.0, 0.0).astype(BF16)


def _sb_block(w, carry, before, tri):
    log_stay, log_take = _sb_logs(w, before)
    a = _sb_weights(log_take, _sb_suffix(log_stay, tri), carry, before)
    return a, carry + jnp.sum(log_stay, axis=-1, keepdims=True)


def _sb_logs(w, before):
    nw = -w
    soft = jnp.log(1.0 + jnp.exp2(jnp.minimum(w, nw))) * LOG2E
    log_stay = jnp.minimum(nw, 0.0) - soft
    log_take = log_stay + w
    if before is not None:
        log_stay = jnp.where(before, log_stay, 0.0)
    return log_stay, log_take


def _sb_suffix(log_stay, tri):
    return _dot(log_stay.astype(BF16), tri)


def _sb_weights(log_take, suffix, carry, before):
    a = jnp.exp2(log_take + (suffix + _widen(carry, suffix.shape[1])))
    if before is not None:
        a = jnp.where(before, a, 0.0)
    return a


def _sb_super_block(qs, q_ref, k_ref, v_ref, z_ref, o_ref, carry_ref, acc_ref, tri, *, blk, sup):
    n_sub = sup // blk
    q0 = qs * sup
    carry_ref[...] = jnp.zeros(carry_ref.shape, F32)
    acc_ref[...] = jnp.zeros(acc_ref.shape, F32)

    def mask(step):
        _, row_lo, row_hi, diagonal = step
        if not diagonal:
            return None
        r = lax.broadcasted_iota(jnp.int32, (row_hi - row_lo, blk), 0)
        c = lax.broadcasted_iota(jnp.int32, (row_hi - row_lo, blk), 1)
        return c < r

    def logits(step):
        kj, row_lo, row_hi, _ = step
        q = q_ref[0, q0 + row_lo:q0 + row_hi, :]
        return _dot_nt(q, k_ref[0, kj * blk:(kj + 1) * blk, :])

    def suffix_sums(step, w):
        sl = slice(step[1], step[2])
        log_stay, log_take = _sb_logs(w, mask(step))
        carry = carry_ref[sl, :]
        carry_ref[sl, :] = carry + jnp.sum(log_stay, axis=-1, keepdims=True)
        return log_take, _sb_suffix(log_stay, tri), carry

    def accumulate(step, parts):
        kj, row_lo, row_hi, _ = step
        sl = slice(row_lo, row_hi)
        a = _sb_weights(*parts, mask(step)).astype(BF16)
        acc_ref[sl, :] = acc_ref[sl, :] + _dot(a, v_ref[0, kj * blk:(kj + 1) * blk, :])

    def finish():
        gate = _silu(z_ref[0, q0:q0 + sup, :].astype(F32))
        o_ref[0, q0:q0 + sup, :] = (acc_ref[...] * gate).astype(o_ref.dtype)

    steps = []
    for c in reversed(range(n_sub)):
        steps.append((qs * n_sub + c, c * blk, (c + 1) * blk, True))
        if c + 1 < n_sub:
            steps.append((qs * n_sub + c, (c + 1) * blk, sup, False))
    steps += [(kj, 0, sup, False) for kj in reversed(range(qs * n_sub))]
    return _pipeline_slots(steps, [logits, suffix_sums, accumulate]), finish


def _fox_prompt_kernel(q_ref, k_ref, v_ref, f_ref, z_ref, o_ref, m_ref, l_ref, acc_ref, *, seq, blk, sup):
    h = pl.program_id(1)
    for qs in range(seq // sup):
        slots, finish = _fox_super_block(qs, h, q_ref, k_ref, v_ref, f_ref, z_ref, o_ref, m_ref, l_ref, acc_ref,
                                         blk=blk, sup=sup)
        _trace_interleaved([slots])
        finish()


def _sb_prompt_kernel(q_ref, k_ref, v_ref, z_ref, o_ref, carry_ref, acc_ref, *, seq, blk, sup):
    tri = _tri_after(blk)
    for qs in range(seq // sup):
        slots, finish = _sb_super_block(qs, q_ref, k_ref, v_ref, z_ref, o_ref, carry_ref, acc_ref, tri,
                                        blk=blk, sup=sup)
        _trace_interleaved([slots])
        finish()


def prompt_attention(kind, q, k, v, z, f4=None):
    b, s, w = k.shape
    heads = w // HEAD_DIM
    blk = f4.shape[3] if kind == "fox" else ATTN_BLOCK
    sup = min(ATTN_SUPER[kind], s)
    head_spec = pl.BlockSpec((1, s, HEAD_DIM), lambda bi, hi: (bi, 0, hi))
    part_spec = lambda part: pl.BlockSpec((1, s, HEAD_DIM), lambda bi, hi: (bi, 0, part * heads + hi))
    in_specs = [part_spec(q[1]), head_spec, head_spec]
    args = [q[0], k, v]
    col_state = pltpu.VMEM((sup, LANES), F32)
    acc_state = pltpu.VMEM((sup, HEAD_DIM), F32)
    if kind == "fox":
        in_specs.append(pl.BlockSpec((1, HEAD_ROWS, s // blk, blk), lambda bi, hi: (bi, 0, 0, 0)))
        args.append(f4)
        body = functools.partial(_fox_prompt_kernel, seq=s, blk=blk, sup=sup)
        scratch = [col_state, col_state, acc_state]
    else:
        body = functools.partial(_sb_prompt_kernel, seq=s, blk=blk, sup=sup)
        scratch = [col_state, acc_state]
    in_specs.append(part_spec(z[1]))
    args.append(z[0])
    vmem = 2 * 5 * _nbytes((s, HEAD_DIM), BF16) + 2 * _nbytes((HEAD_ROWS, s), F32)
    vmem += 3 * _nbytes((sup, LANES), F32) + 12 * _nbytes((sup, blk), F32)
    return pl.pallas_call(
        body,
        grid=(b, heads),
        in_specs=in_specs,
        out_specs=head_spec,
        out_shape=jax.ShapeDtypeStruct((b, s, w), BF16),
        scratch_shapes=scratch,
        compiler_params=_params(2, vmem),
        name=kind + "_prompt_attention",
    )(*args)


def _sample_kernel(*refs, past, t_new, keys):
    (fq, fkn, fvn, fkc, fvc, flf, fz, sq, skn, svn, skc, svc, sz,
     fo, so, fka, fva, ska, sva) = refs
    _sample_one("fox", fq, fkn, fvn, fkc, fvc, flf, fz, fo, fka, fva, past=past, t_new=t_new, keys=keys)
    _sample_one("sb", sq, skn, svn, skc, svc, None, sz, so, ska, sva, past=past, t_new=t_new, keys=keys)


def _sample_one(kind, q_ref, kn_ref, vn_ref, kc_ref, vc_ref, lf_ref, z_ref, o_ref, ka_ref, va_ref, *,
                past, t_new, keys):
    rows = N_HEADS * t_new
    width = N_HEADS * HEAD_DIM
    pad = keys - past - t_new

    for src_c, src_n, dst in ((kc_ref, kn_ref, ka_ref), (vc_ref, vn_ref, va_ref)):
        dst[0:past, :] = src_c[0].reshape(past, width).astype(BF16)
        dst[past:past + t_new, :] = src_n[0]
        dst[past + t_new:keys, :] = jnp.zeros((pad, width), BF16)

    r_q = lax.broadcasted_iota(jnp.int32, (rows, width), 0)
    c_q = lax.broadcasted_iota(jnp.int32, (rows, width), 1)
    q_tiled = jnp.concatenate([q_ref[0]] * N_HEADS, axis=0)
    same_head = _shift_div(r_q, t_new) == _shift_div(c_q, HEAD_DIM)
    q_bd = jnp.where(same_head, q_tiled, jnp.zeros_like(q_tiled))

    qk = _dot_nt(q_bd, ka_ref[...])
    r_s = lax.broadcasted_iota(jnp.int32, (rows, keys), 0)
    c_s = lax.broadcasted_iota(jnp.int32, (rows, keys), 1)
    q_pos = past + (r_s & (t_new - 1))

    if kind == "fox":
        f_all = _cumsum_lanes(lf_ref[0], LANES)
        f_rows = jnp.concatenate(
            [jnp.broadcast_to(f_all[hh:hh + 1, :], (t_new, keys)) for hh in range(N_HEADS)], axis=0)
        s = jnp.where(c_s <= q_pos, qk - f_rows * LOG2E, -jnp.inf)
        m = jnp.max(s, axis=-1, keepdims=True)
        p = jnp.exp2(s - m)
        wts = p
        norm = jnp.sum(p, axis=-1, keepdims=True)
    else:
        w = qk
        before = c_s < q_pos
        tri = _tri_after(LANES)
        carry = jnp.zeros((rows, LANES), F32)
        chunks = [None] * (keys // LANES)
        for c in reversed(range(keys // LANES)):
            sl = slice(c * LANES, (c + 1) * LANES)
            chunks[c], carry = _sb_block(w[:, sl], carry, before[:, sl], tri)
        wts = jnp.concatenate(chunks, axis=1)
        norm = None

    o_full = _dot(wts.astype(BF16), va_ref[...])
    if norm is not None:
        o_full = o_full / norm
    o = jnp.concatenate(
        [o_full[hh * t_new:(hh + 1) * t_new, hh * HEAD_DIM:(hh + 1) * HEAD_DIM] for hh in range(N_HEADS)],
        axis=1)
    o_ref[0] = (o * _silu(z_ref[0].astype(F32))).astype(o_ref.dtype)


def sample_attention(fox, sb, logf_all):
    b, t_new, w = fox[1].shape
    past = fox[3].shape[1]
    assert t_new & (t_new - 1) == 0
    keys = -(-(past + t_new) // LANES) * LANES
    new_spec = pl.BlockSpec((1, t_new, w), lambda i: (i, 0, 0))
    part_spec = lambda part: pl.BlockSpec((1, t_new, w), lambda i: (i, 0, part))
    cache_spec = pl.BlockSpec((1, past, N_HEADS, HEAD_DIM), lambda i: (i, 0, 0, 0))
    logf_spec = pl.BlockSpec((1, HEAD_ROWS, keys), lambda i: (i, 0, 0))
    mixer_specs = lambda m: [part_spec(m[0][1]), new_spec, new_spec, cache_spec, cache_spec]
    vmem = 8 * _nbytes((past, w), F32) + 4 * _nbytes((keys, w), BF16)
    vmem += 24 * _nbytes((N_HEADS * t_new, keys), F32)
    return pl.pallas_call(
        functools.partial(_sample_kernel, past=past, t_new=t_new, keys=keys),
        grid=(b,),
        in_specs=(mixer_specs(fox) + [logf_spec, part_spec(fox[5][1])]
                  + mixer_specs(sb) + [part_spec(sb[5][1])]),
        out_specs=[new_spec, new_spec],
        out_shape=[jax.ShapeDtypeStruct((b, t_new, w), BF16)] * 2,
        scratch_shapes=[pltpu.VMEM((keys, w), BF16)] * 4,
        compiler_params=_params(1, vmem),
        name="sample_attention",
    )(fox[0][0], *fox[1:5], logf_all, fox[5][0], sb[0][0], *sb[1:5], sb[5][0])


def _causal_mix_weights(w_ref):
    n_groups, chunk, _ = w_ref.shape
    r = lax.broadcasted_iota(jnp.int32, (chunk, chunk), 0)
    c = lax.broadcasted_iota(jnp.int32, (chunk, chunk), 1)
    return [jnp.where(c <= r, w_ref[g], 0.0).astype(BF16) for g in range(n_groups)]


def _layer_norm(v, g_ref, b_ref):
    mu = jnp.mean(v, axis=-1, keepdims=True)
    vc = v - mu
    var = jnp.mean(vc * vc, axis=-1, keepdims=True)
    return vc * lax.rsqrt(var + LN_EPS) * g_ref[...] + b_ref[...]


def _mix_and_gate(w_causal, vnb, bias_ref, u_ref, z_ref, rows, store):
    group_dim = vnb.shape[1] // len(w_causal)
    for g, wg in enumerate(w_causal):
        sl = slice(g * group_dim, (g + 1) * group_dim)
        mixed = _dot(wg, vnb[:, sl]) + bias_ref[:, g:g + 1]
        store(sl, (u_ref[rows, sl].astype(F32) * mixed * z_ref[rows, sl].astype(F32)).astype(BF16))


def _sgu_kernel(u_ref, v_ref, z_ref, g_ref, b_ref, w_ref, bias_ref, o_ref, *vn_refs):
    chunk = w_ref.shape[1]
    w_causal = _causal_mix_weights(w_ref)
    for s in range(v_ref.shape[0] // chunk):
        rs = slice(s * chunk, (s + 1) * chunk)
        vn = _layer_norm(v_ref[rs, :].astype(F32), g_ref, b_ref)
        if vn_refs:
            vn_refs[0][rs, :] = vn

        def store(sl, val, rs=rs):
            o_ref[rs, sl] = val

        _mix_and_gate(w_causal, vn.astype(BF16), bias_ref, u_ref, z_ref, rs, store)


SGU_STEP_BYTES = 4 << 20


def spatial_gate(uv, gate, ln_g, ln_b, w_mix, bias_t, emit_vn):
    m, c = gate.shape
    n_groups, chunk, _ = w_mix.shape
    rows = min(m, chunk * max(1, SGU_STEP_BYTES // _nbytes((chunk, c), uv.dtype)))
    assert m % rows == 0
    blk = lambda j: pl.BlockSpec((rows, c), lambda i, j=j: (i, j))
    out_shape = [jax.ShapeDtypeStruct((m, c), BF16)]
    out_specs = [pl.BlockSpec((rows, c), lambda i: (i, 0))]
    if emit_vn:
        out_shape.append(jax.ShapeDtypeStruct((m, c), F32))
        out_specs.append(pl.BlockSpec((rows, c), lambda i: (i, 0)))
    vmem = 6 * _nbytes((rows, c), uv.dtype) + 2 * _nbytes((rows, c), BF16) + 8 * _nbytes((rows, c), F32)
    vmem += 2 * _nbytes(w_mix.shape, F32)
    return pl.pallas_call(
        _sgu_kernel,
        grid=(m // rows,),
        in_specs=[blk(0), blk(1), blk(0),
                  pl.BlockSpec((1, c), lambda i: (0, 0)),
                  pl.BlockSpec((1, c), lambda i: (0, 0)),
                  pl.BlockSpec((n_groups, chunk, chunk), lambda i: (0, 0, 0)),
                  pl.BlockSpec((chunk, n_groups), lambda i: (0, 0))],
        out_specs=out_specs,
        out_shape=out_shape,
        compiler_params=_params(1, vmem),
        name="spatial_gate",
    )(uv, uv, gate, ln_g.reshape(1, c), ln_b.reshape(1, c), w_mix, bias_t)


def _layer0_inputs(xn_prompt, xn_sample, w_t):
    row_sets = [([xn_prompt], BF16, None), ([xn_sample], BF16, None)]
    qz = project(row_sets, w_t, col_blocks=QZ_PARTS, w_transposed=True,
                 out_scale=(QK_SCALE, 1.0, QK_SCALE, 1.0), name="in_proj0_qz")
    kv, kv_out = ([], []), ([], [])
    for p in KV_PARTS:
        outs = project(row_sets, w_t, col_blocks=(p,), heads_out=True, w_transposed=True, name="in_proj0_kv")
        for g, out in enumerate(outs):
            kv[g].append(out[0])
            kv_out[g].append(out[1])
    return qz, kv, kv_out


def kernel(x_prompt, x_sample, cache_fox_k, cache_fox_v, cache_fox_logf, cache_sb_k, cache_sb_v,
           norm0_g, w_in0, b_forget, w_out0, norm1_g, w_in1, sgu_ln_g, sgu_ln_b, w_sp, b_sp, w_out1, final_g):
    bp, s, d = x_prompt.shape
    bs, t_new, _ = x_sample.shape
    past = cache_fox_k.shape[1]
    hp = x_prompt.reshape(bp * s, d)
    hs = x_sample.reshape(bs * t_new, d)

    n_main = 8 * MIX_WIDTH
    w_t = w_in0.T
    wf = jnp.pad(w_in0[:, n_main:], ((0, 0), (0, LANES - N_HEADS))).astype(BF16)
    bf = jnp.pad(b_forget, (0, LANES - N_HEADS)).reshape(1, LANES).astype(F32)

    xn_p, logf_t, cum_t = norm_forget(hp, norm0_g, wf, bf, bp, True)
    xn_s, logf_s = norm_forget(hs, norm0_g, wf, bf, 1, False)
    (qz_p, qz_s), ((fk, fv, sk, sv), (gk, gv, tk, tv)), (kv_prompt, kv_sample) = _layer0_inputs(xn_p, xn_s, w_t)

    as_seq = lambda a: a.reshape(bp, s, -1)
    qz_p = as_seq(qz_p)
    f4 = cum_t.reshape(bp, HEAD_ROWS, s // FOX_BLOCK, FOX_BLOCK)
    mix_f = prompt_attention("fox", (qz_p, 0), as_seq(fk), as_seq(fv), (qz_p, 1), f4)
    mix_s = prompt_attention("sb", (qz_p, 2), as_seq(sk), as_seq(sv), (qz_p, 3))
    fox_logf_prompt = jnp.swapaxes(logf_t[:, :N_HEADS, :], 1, 2)

    as_new = lambda a: a.reshape(bs, t_new, -1)
    qz_s = as_new(qz_s)
    logf_new_t = jnp.swapaxes(logf_s[0, :N_HEADS, :].reshape(N_HEADS, bs, t_new), 0, 1)
    keys = -(-(past + t_new) // LANES) * LANES
    logf_all = jnp.concatenate([jnp.swapaxes(cache_fox_logf.astype(F32), 1, 2), logf_new_t], axis=2)
    logf_all = jnp.pad(logf_all, ((0, 0), (0, HEAD_ROWS - N_HEADS), (0, keys - past - t_new)))
    mix_g, mix_t = sample_attention(
        ((qz_s, 0), as_new(gk), as_new(gv), cache_fox_k, cache_fox_v, (qz_s, 1)),
        ((qz_s, 2), as_new(tk), as_new(tv), cache_sb_k, cache_sb_v, (qz_s, 3)), logf_all)
    fox_logf_sample = jnp.swapaxes(logf_new_t, 1, 2)
    flat = lambda a: a.reshape(-1, MIX_WIDTH)
    w_out0_b = w_out0.astype(BF16)
    ((hp, xn1_p),) = project([([flat(mix_f), flat(mix_s)], F32, hp)], w_out0_b, tn=d, tm=512,
                             post_norm_g=norm1_g, post_norm_also=BF16, name="out_proj0")
    ((hs, xn1_s),) = project([([flat(mix_g), flat(mix_t)], F32, hs)], w_out0_b, tn=d, tm=512,
                             post_norm_g=norm1_g, post_norm_also=BF16, name="out_proj0")

    c_width = w_in1.shape[1] // 3
    bias_t = b_sp.T.astype(F32)
    layer1_rows = [([xn1_p], BF16, None), ([xn1_s], F32, None)]
    uv_p, uv_s = project(layer1_rows, w_in1, n_cols=2 * c_width, name="in_proj1_uv")
    gate_p, gate_s = project(layer1_rows, w_in1, col0=2 * c_width, out_silu=True, name="in_proj1_gate")
    (act_p,) = spatial_gate(uv_p, gate_p, sgu_ln_g, sgu_ln_b, w_sp, bias_t, False)

    per_blk = C_CHUNK // t_new
    eye = jnp.eye(per_blk, dtype=F32)
    w_corner = w_sp[:, :t_new, :t_new]
    w_bd = jnp.einsum("ab,gts->gatbs", eye, w_corner).reshape(C_GROUPS, C_CHUNK, C_CHUNK)
    bias_bd = jnp.tile(b_sp[:, :t_new].T.astype(F32), (per_blk, 1))
    act_s, vn_s = spatial_gate(uv_s, gate_s, sgu_ln_g, sgu_ln_b, w_bd, bias_bd, True)
    w_out1_b = w_out1.astype(BF16)
    (y_prompt,) = project([([act_p], F32, hp)], w_out1_b, tn=d, tm=512, post_norm_g=final_g, name="out_proj1")
    (y_sample,) = project([([act_s], F32, hs)], w_out1_b, tn=d, tm=512, post_norm_g=final_g, name="out_proj1")
    y_prompt = y_prompt.reshape(bp, s, d)
    y_sample = y_sample.reshape(bs, t_new, d)
    fk4, fv4, sk4, sv4 = (a.reshape(bp, s, N_HEADS, HEAD_DIM) for a in kv_prompt)
    gk4, gv4, tk4, tv4 = (a.reshape(bs, t_new, N_HEADS, HEAD_DIM) for a in kv_sample)
    return (y_prompt, y_sample,
            fk4, fv4, fox_logf_prompt,
            gk4, gv4, fox_logf_sample,
            sk4, sv4,
            tk4, tv4,
            vn_s.reshape(bs, t_new, c_width))
```

```python
import functools

import jax
import jax.numpy as jnp
from jax import lax
from jax.experimental import pallas as pl
from jax.experimental.pallas import tpu as pltpu

F32 = jnp.float32
BF16 = jnp.bfloat16

HEAD_DIM = 128
N_HEADS = 8
MIX_WIDTH = N_HEADS * HEAD_DIM
C_GROUPS = 16
C_CHUNK = 128
RMS_EPS = 1e-6
LN_EPS = 1e-5
ATTN_SCALE = HEAD_DIM ** -0.5
LOG2E = 1.4426950408889634
QK_SCALE = ATTN_SCALE * LOG2E
QZ_PARTS = (0, 3, 4, 7)
KV_PARTS = (1, 2, 5, 6)

LANES = 128
ATTN_BLOCK = 256
FOX_BLOCK = 512
ATTN_SUPER = {"fox": 1024, "sb": 2048}
V7X_VMEM_BYTES = 64 * 1024 * 1024
VMEM_LIMIT_CAP = V7X_VMEM_BYTES - 8 * 1024 * 1024


def _params(n_grid_axes, vmem_bytes, flags=None):
    limit = min(int(vmem_bytes * 1.25) + (4 << 20), VMEM_LIMIT_CAP)
    return pltpu.CompilerParams(
        dimension_semantics=("arbitrary",) * n_grid_axes,
        vmem_limit_bytes=limit,
        flags=flags,
    )


def _nbytes(shape, dtype):
    n = 1
    for s in shape:
        n *= s
    return n * jnp.dtype(dtype).itemsize


def _dot(a, b):
    return jnp.dot(a, b, preferred_element_type=F32)


def _dot_nt(a, b):
    return lax.dot_general(a, b, (((1,), (1,)), ((), ())), preferred_element_type=F32)


def _log_sigmoid(x):
    return jnp.minimum(x, 0.0) - jnp.log(1.0 + jnp.exp(-jnp.abs(x)))


def _silu(x):
    return x / (1.0 + jnp.exp2(x * -LOG2E))


def _widen(x, n):
    return x if n == LANES else jnp.concatenate([x] * (n // LANES), axis=1)


def _shift_div(x, n):
    assert n & (n - 1) == 0
    return lax.shift_right_logical(x, jnp.full_like(x, n.bit_length() - 1))


def _split3(x):
    hi = x.astype(BF16)
    r1 = x - hi.astype(F32)
    mid = r1.astype(BF16)
    lo = (r1 - mid.astype(F32)).astype(BF16)
    return hi, mid, lo


EPILOGUE_ROW_CHUNK = 256
def _proj_kernel(*refs, k_sizes, has_res, w_transposed, heads_out, has_extra, out_scale, post_norm, out_silu):
    it = iter(refs)
    n_sets = 2 if has_extra else 1
    x_sets = [[next(it) for _ in k_sizes] for _ in range(n_sets)]
    w_ref = next(it)
    g_ref = next(it) if post_norm else None
    res_refs = [next(it) if has_res else None for _ in range(n_sets)]
    second = heads_out or post_norm == "also"
    out_sets = [(next(it), next(it) if second else None) for _ in range(n_sets)]
    if w_ref.dtype == BF16:
        wb_ref = w_ref
    else:
        wb_ref = next(it)

        @pl.when(pl.program_id(1) == 0)
        def _():
            wb_ref[...] = w_ref[...].astype(BF16)

    def row_chunk(x_refs, res_ref, o_ref, o4_ref, rs):
        k0 = 0
        acc = None
        for x_ref, k in zip(x_refs, k_sizes):
            if w_transposed:
                part = _dot_nt(x_ref[rs, :], wb_ref[:, k0:k0 + k])
            else:
                part = _dot(x_ref[rs, :], wb_ref[k0:k0 + k, :])
            acc = part if acc is None else acc + part
            k0 += k
        if has_res:
            acc = acc + res_ref[rs, :]
        if post_norm:
            ms = jnp.mean(acc * acc, axis=-1, keepdims=True)
            normed = acc * lax.rsqrt(ms + RMS_EPS) * g_ref[...]
            if post_norm == "also":
                o4_ref[rs, :] = normed.astype(o4_ref.dtype)
            else:
                acc = normed
        scaled = acc if out_scale is None else acc * _pick(out_scale, pl.program_id(0))
        if out_silu:
            scaled = _silu(scaled)
        o_ref[rs, :] = scaled.astype(o_ref.dtype)
        if heads_out:
            o4_ref[rs, :, :] = acc.reshape(acc.shape[0], N_HEADS, HEAD_DIM)

    def row_set(x_refs, res_ref, o_ref, o4_ref):
        rows = o_ref.shape[0]
        chunk = EPILOGUE_ROW_CHUNK if out_silu and rows % EPILOGUE_ROW_CHUNK == 0 else rows
        for c in range(rows // chunk):
            row_chunk(x_refs, res_ref, o_ref, o4_ref, slice(c * chunk, (c + 1) * chunk))

    row_set(x_sets[0], res_refs[0], *out_sets[0])
    if has_extra:
        @pl.when(pl.program_id(1) == pl.num_programs(1) - 1)
        def _():
            row_set(x_sets[1], res_refs[1], *out_sets[1])


def _pick(values, j):
    out = values[0]
    for t in range(1, len(values)):
        out = jnp.where(j >= t, values[t], out)
    return out


def project(row_sets, w, *, col0=0, n_cols=None, col_blocks=None, tn=1024, tm=1024, heads_out=False,
            w_transposed=False, out_scale=None, out_silu=False, post_norm_g=None, post_norm_also=None,
            name="proj"):
    assert 1 <= len(row_sets) <= 2
    xs0, _, res0 = row_sets[0]
    m = xs0[0].shape[0]
    k_sizes = tuple(x.shape[1] for x in xs0)
    k_total = sum(k_sizes)
    k_axis, n_axis = (1, 0) if w_transposed else (0, 1)
    assert k_total == w.shape[k_axis]
    if col_blocks is None:
        n_cols = w.shape[n_axis] - col0 if n_cols is None else n_cols
        assert n_cols % tn == 0 and col0 % tn == 0
        col_blocks = tuple(col0 // tn + j for j in range(n_cols // tn))
    n_j = len(col_blocks)
    n_cols = n_j * tn
    tm = min(tm, m)
    assert m % tm == 0
    has_res = res0 is not None
    if w_transposed:
        w_block, w_index = (tn, k_total), lambda j, i: (_pick(col_blocks, j), 0)
    else:
        w_block, w_index = (k_total, tn), lambda j, i: (0, _pick(col_blocks, j))
    fixed = pl.Buffered(1)
    w_spec = pl.BlockSpec(w_block, w_index, pipeline_mode=fixed) if n_j == 1 else pl.BlockSpec(w_block, w_index)
    cast_w = w.dtype != BF16
    vmem = (1 if n_j == 1 else 2) * _nbytes(w_block, w.dtype) + (_nbytes(w_block, BF16) if cast_w else 0)

    x_specs, x_args, res_specs, res_args, out_specs, out_shape = [], [], [], [], [], []
    for s, (xs, out_dtype, res) in enumerate(row_sets):
        rows = xs[0].shape[0]
        assert tuple(x.shape[1] for x in xs) == k_sizes and (res is not None) == has_res
        if s == 0:
            tile, row_index, bufs = tm, (lambda j, i: i), 2
            x_specs += [pl.BlockSpec((tile, k), lambda j, i: (i, 0)) for k in k_sizes]
        else:
            tile, row_index, bufs = rows, (lambda j, i: 0), 1
            x_specs += [pl.BlockSpec((tile, k), lambda j, i: (0, 0), pipeline_mode=fixed) for k in k_sizes]
        x_args += list(xs)
        vmem += sum(bufs * _nbytes((tile, k), BF16) for k in k_sizes)
        if has_res:
            res_specs.append(pl.BlockSpec((tile, tn), lambda j, i, r=row_index: (r(j, i), j)))
            res_args.append(res)
            vmem += 2 * _nbytes((tile, tn), F32)
        out_specs.append(pl.BlockSpec((tile, tn), lambda j, i, r=row_index: (r(j, i), j)))
        out_shape.append(jax.ShapeDtypeStruct((rows, n_cols), out_dtype))
        vmem += 2 * _nbytes((tile, tn), out_dtype) + _nbytes((tile, tn), F32)
        if heads_out:
            assert n_cols == tn == MIX_WIDTH
            out_specs.append(pl.BlockSpec((tile, N_HEADS, HEAD_DIM), lambda j, i, r=row_index: (r(j, i), 0, 0)))
            out_shape.append(jax.ShapeDtypeStruct((rows, N_HEADS, HEAD_DIM), F32))
            vmem += 2 * _nbytes((tile, tn), F32)
        if post_norm_also is not None:
            assert not heads_out and post_norm_g is not None
            out_specs.append(pl.BlockSpec((tile, tn), lambda j, i, r=row_index: (r(j, i), j)))
            out_shape.append(jax.ShapeDtypeStruct((rows, n_cols), post_norm_also))
            vmem += 2 * _nbytes((tile, tn), post_norm_also)
    norm_specs, norm_args = [], []
    if post_norm_g is not None:
        assert n_j == 1 and col_blocks == (0,) and n_cols == w.shape[n_axis]
        norm_specs.append(pl.BlockSpec((1, n_cols), lambda j, i: (0, 0)))
        norm_args.append(post_norm_g.reshape(1, n_cols))
        vmem += 2 * _nbytes((tm, n_cols), F32)
    post_norm = None if post_norm_g is None else ("replace" if post_norm_also is None else "also")
    outs = pl.pallas_call(
        functools.partial(_proj_kernel, k_sizes=k_sizes, has_res=has_res, w_transposed=w_transposed,
                          heads_out=heads_out, has_extra=len(row_sets) == 2, out_scale=out_scale,
                          post_norm=post_norm, out_silu=out_silu),
        grid=(n_j, m // tm),
        in_specs=x_specs + [w_spec] + norm_specs + res_specs,
        out_specs=out_specs,
        out_shape=out_shape,
        scratch_shapes=[pltpu.VMEM(w_block, BF16)] if cast_w else [],
        compiler_params=_params(2, vmem),
        name=name,
    )(*x_args, w, *norm_args, *res_args)
    per_set = 2 if heads_out or post_norm == "also" else 1
    return [tuple(outs[s * per_set:(s + 1) * per_set]) if per_set == 2 else outs[s] for s in range(len(row_sets))]


CUM_CHUNK = 256
HEAD_ROWS = 16


def _tri_incl(n):
    r = lax.broadcasted_iota(jnp.int32, (n, n), 0)
    c = lax.broadcasted_iota(jnp.int32, (n, n), 1)
    return jnp.where(r <= c, 1.0, 0.0).astype(BF16)


def _cumsum_lanes(x, chunk):
    n = x.shape[1]
    tri = _tri_incl(chunk)
    carry = jnp.zeros((x.shape[0], 1), F32)
    outs = []
    for c in range(n // chunk):
        hi, mid, lo = _split3(x[:, c * chunk:(c + 1) * chunk])
        loc = _dot(hi, tri) + _dot(mid, tri) + _dot(lo, tri)
        outs.append(loc + carry)
        carry = carry + loc[:, chunk - 1:chunk]
    return jnp.concatenate(outs, axis=1)


FORGET_ROW_TILE = 1024


def _norm_forget_kernel(x_ref, g_ref, w_ref, b_ref, xn_ref, logf_ref, *cum_refs):
    x = x_ref[...]
    ms = jnp.mean(x * x, axis=-1, keepdims=True)
    xn = (x * lax.rsqrt(ms + RMS_EPS) * g_ref[...]).astype(BF16)
    xn_ref[...] = xn
    logit = _dot(xn, w_ref[...]) + b_ref[...]
    logf_t = _log_sigmoid(logit).T[:HEAD_ROWS, :]
    logf_ref[0] = logf_t
    if cum_refs:
        cum_ref, carry_ref = cum_refs
        t = logf_t.shape[1]

        @pl.when(pl.program_id(1) == 0)
        def _():
            carry_ref[...] = jnp.zeros(carry_ref.shape, F32)

        cum = _cumsum_lanes(logf_t, CUM_CHUNK) + _widen(carry_ref[...], t)
        cum_ref[0] = cum
        carry_ref[...] = jnp.broadcast_to(cum[:, t - 1:t], carry_ref.shape)


def norm_forget(x, norm_g, wf, bf, n_seq, with_cumsum):
    m, d = x.shape
    t = m // n_seq
    tt = min(FORGET_ROW_TILE, t)
    steps = t // tt
    assert t % tt == 0
    head_spec = pl.BlockSpec((1, HEAD_ROWS, tt), lambda s, i: (s, 0, i))
    out_shape = [jax.ShapeDtypeStruct((m, d), BF16), jax.ShapeDtypeStruct((n_seq, HEAD_ROWS, t), F32)]
    out_specs = [pl.BlockSpec((tt, d), lambda s, i: (s * steps + i, 0)), head_spec]
    scratch = []
    if with_cumsum:
        out_shape.append(out_shape[1])
        out_specs.append(head_spec)
        scratch.append(pltpu.VMEM((HEAD_ROWS, LANES), F32))
    vmem = 5 * _nbytes((tt, d), F32) + 2 * _nbytes((tt, d), BF16) + 2 * _nbytes((d, LANES), BF16)
    vmem += 8 * _nbytes((tt, LANES), F32)
    return pl.pallas_call(
        _norm_forget_kernel,
        grid=(n_seq, steps),
        in_specs=[pl.BlockSpec((tt, d), lambda s, i: (s * steps + i, 0)),
                  pl.BlockSpec((1, d), lambda s, i: (0, 0)),
                  pl.BlockSpec((d, LANES), lambda s, i: (0, 0)),
                  pl.BlockSpec((1, LANES), lambda s, i: (0, 0))],
        out_specs=out_specs,
        out_shape=out_shape,
        scratch_shapes=scratch,
        compiler_params=_params(2, vmem),
        name="norm_forget",
    )(x, norm_g.reshape(1, d), wf, bf)


def _pipeline_slots(steps, stages):
    vals = [None] * len(steps)

    def slot(t):
        for j, stage in enumerate(stages):
            i = t - j
            if 0 <= i < len(steps):
                vals[i] = stage(steps[i]) if j == 0 else stage(steps[i], vals[i])

    return [functools.partial(slot, t) for t in range(len(steps) + len(stages) - 1)]


def _trace_interleaved(slot_lists):
    total = max(len(slots) for slots in slot_lists)
    done = [0] * len(slot_lists)
    for t in range(total):
        for n, slots in enumerate(slot_lists):
            while done[n] * total < (t + 1) * len(slots):
                slots[done[n]]()
                done[n] += 1


def _fox_super_block(qs, h, q_ref, k_ref, v_ref, f_ref, z_ref, o_ref, m_ref, l_ref, acc_ref, *, blk, sup):
    n_sub = sup // blk
    q0 = qs * sup
    m_ref[...] = jnp.full(m_ref.shape, -jnp.inf, F32)
    l_ref[...] = jnp.zeros(l_ref.shape, F32)
    acc_ref[...] = jnp.zeros(acc_ref.shape, F32)

    def logits(step):
        kj, row_lo, diagonal = step
        q = q_ref[0, q0 + row_lo:q0 + sup, :]
        s = _dot_nt(q, k_ref[0, kj * blk:(kj + 1) * blk, :])
        s = s - f_ref[0, pl.ds(h, 1), kj:kj + 1, :].reshape(1, blk) * LOG2E
        if diagonal:
            r = lax.broadcasted_iota(jnp.int32, s.shape, 0)
            c = lax.broadcasted_iota(jnp.int32, s.shape, 1)
            s = jnp.where(c <= r, s, -jnp.inf)
        return s

    def weights(step, s):
        sl = slice(step[1], sup)
        m_old = m_ref[sl, :]
        m_new = jnp.maximum(m_old, jnp.max(s, axis=-1, keepdims=True))
        alpha = jnp.exp2(m_old - m_new)
        p = jnp.exp2(s - _widen(m_new, blk))
        m_ref[sl, :] = m_new
        l_ref[sl, :] = alpha * l_ref[sl, :] + jnp.sum(p, axis=-1, keepdims=True)
        return alpha, p.astype(BF16)

    def accumulate(step, alpha_p):
        kj, row_lo, _ = step
        sl = slice(row_lo, sup)
        alpha, p = alpha_p
        acc_ref[sl, :] = alpha * acc_ref[sl, :] + _dot(p, v_ref[0, kj * blk:(kj + 1) * blk, :])

    def finish():
        gate = _silu(z_ref[0, q0:q0 + sup, :].astype(F32))
        o_ref[0, q0:q0 + sup, :] = (acc_ref[...] / l_ref[...] * gate).astype(o_ref.dtype)

    steps = [(kj, 0, False) for kj in range(qs * n_sub)]
    steps += [(qs * n_sub + c, c * blk, True) for c in range(n_sub)]
    return _pipeline_slots(steps, [logits, weights, accumulate]), finish


def _tri_after(n):
    r = lax.broadcasted_iota(jnp.int32, (n, n), 0)
    c = lax.broadcasted_iota(jnp.int32, (n, n), 1)
    return jnp.where(r > c, 1.0, 0.0).astype(BF16)


def _sb_block(w, carry, before, tri):
    log_stay, log_take = _sb_logs(w, before)
    a = _sb_weights(log_take, _sb_suffix(log_stay, tri), carry, before)
    return a, carry + jnp.sum(log_stay, axis=-1, keepdims=True)


def _sb_logs(w, before):
    nw = -w
    soft = jnp.log(1.0 + jnp.exp2(jnp.minimum(w, nw))) * LOG2E
    log_stay = jnp.minimum(nw, 0.0) - soft
    log_take = log_stay + w
    if before is not None:
        log_stay = jnp.where(before, log_stay, 0.0)
    return log_stay, log_take


def _sb_suffix(log_stay, tri):
    return _dot(log_stay.astype(BF16), tri)


def _sb_weights(log_take, suffix, carry, before):
    a = jnp.exp2(log_take + (suffix + _widen(carry, suffix.shape[1])))
    if before is not None:
        a = jnp.where(before, a, 0.0)
    return a


def _sb_super_block(qs, q_ref, k_ref, v_ref, z_ref, o_ref, carry_ref, acc_ref, tri, *, blk, sup):
    n_sub = sup // blk
    q0 = qs * sup
    carry_ref[...] = jnp.zeros(carry_ref.shape, F32)
    acc_ref[...] = jnp.zeros(acc_ref.shape, F32)

    def mask(step):
        _, row_lo, row_hi, diagonal = step
        if not diagonal:
            return None
        r = lax.broadcasted_iota(jnp.int32, (row_hi - row_lo, blk), 0)
        c = lax.broadcasted_iota(jnp.int32, (row_hi - row_lo, blk), 1)
        return c < r

    def logits(step):
        kj, row_lo, row_hi, _ = step
        q = q_ref[0, q0 + row_lo:q0 + row_hi, :]
        return _dot_nt(q, k_ref[0, kj * blk:(kj + 1) * blk, :])

    def suffix_sums(step, w):
        sl = slice(step[1], step[2])
        log_stay, log_take = _sb_logs(w, mask(step))
        carry = carry_ref[sl, :]
        carry_ref[sl, :] = carry + jnp.sum(log_stay, axis=-1, keepdims=True)
        return log_take, _sb_suffix(log_stay, tri), carry

    def accumulate(step, parts):
        kj, row_lo, row_hi, _ = step
        sl = slice(row_lo, row_hi)
        a = _sb_weights(*parts, mask(step)).astype(BF16)
        acc_ref[sl, :] = acc_ref[sl, :] + _dot(a, v_ref[0, kj * blk:(kj + 1) * blk, :])

    def finish():
        gate = _silu(z_ref[0, q0:q0 + sup, :].astype(F32))
        o_ref[0, q0:q0 + sup, :] = (acc_ref[...] * gate).astype(o_ref.dtype)

    steps = []
    for c in reversed(range(n_sub)):
        steps.append((qs * n_sub + c, c * blk, (c + 1) * blk, True))
        if c + 1 < n_sub:
            steps.append((qs * n_sub + c, (c + 1) * blk, sup, False))
    steps += [(kj, 0, sup, False) for kj in reversed(range(qs * n_sub))]
    return _pipeline_slots(steps, [logits, suffix_sums, accumulate]), finish


def _fox_prompt_kernel(q_ref, k_ref, v_ref, f_ref, z_ref, o_ref, m_ref, l_ref, acc_ref, *, seq, blk, sup):
    h = pl.program_id(1)
    for qs in range(seq // sup):
        slots, finish = _fox_super_block(qs, h, q_ref, k_ref, v_ref, f_ref, z_ref, o_ref, m_ref, l_ref, acc_ref,
                                         blk=blk, sup=sup)
        _trace_interleaved([slots])
        finish()


def _sb_prompt_kernel(q_ref, k_ref, v_ref, z_ref, o_ref, carry_ref, acc_ref, *, seq, blk, sup):
    tri = _tri_after(blk)
    for qs in range(seq // sup):
        slots, finish = _sb_super_block(qs, q_ref, k_ref, v_ref, z_ref, o_ref, carry_ref, acc_ref, tri,
                                        blk=blk, sup=sup)
        _trace_interleaved([slots])
        finish()


def prompt_attention(kind, q, k, v, z, f4=None):
    b, s, w = k.shape
    heads = w // HEAD_DIM
    blk = f4.shape[3] if kind == "fox" else ATTN_BLOCK
    sup = min(ATTN_SUPER[kind], s)
    head_spec = pl.BlockSpec((1, s, HEAD_DIM), lambda bi, hi: (bi, 0, hi))
    part_spec = lambda part: pl.BlockSpec((1, s, HEAD_DIM), lambda bi, hi: (bi, 0, part * heads + hi))
    in_specs = [part_spec(q[1]), head_spec, head_spec]
    args = [q[0], k, v]
    col_state = pltpu.VMEM((sup, LANES), F32)
    acc_state = pltpu.VMEM((sup, HEAD_DIM), F32)
    if kind == "fox":
        in_specs.append(pl.BlockSpec((1, HEAD_ROWS, s // blk, blk), lambda bi, hi: (bi, 0, 0, 0)))
        args.append(f4)
        body = functools.partial(_fox_prompt_kernel, seq=s, blk=blk, sup=sup)
        scratch = [col_state, col_state, acc_state]
    else:
        body = functools.partial(_sb_prompt_kernel, seq=s, blk=blk, sup=sup)
        scratch = [col_state, acc_state]
    in_specs.append(part_spec(z[1]))
    args.append(z[0])
    vmem = 2 * 5 * _nbytes((s, HEAD_DIM), BF16) + 2 * _nbytes((HEAD_ROWS, s), F32)
    vmem += 3 * _nbytes((sup, LANES), F32) + 12 * _nbytes((sup, blk), F32)
    return pl.pallas_call(
        body,
        grid=(b, heads),
        in_specs=in_specs,
        out_specs=head_spec,
        out_shape=jax.ShapeDtypeStruct((b, s, w), BF16),
        scratch_shapes=scratch,
        compiler_params=_params(2, vmem),
        name=kind + "_prompt_attention",
    )(*args)


def _sample_kernel(*refs, past, t_new, keys):
    (fq, fkn, fvn, fkc, fvc, flf, fz, sq, skn, svn, skc, svc, sz,
     fo, so, fka, fva, ska, sva) = refs
    _sample_one("fox", fq, fkn, fvn, fkc, fvc, flf, fz, fo, fka, fva, past=past, t_new=t_new, keys=keys)
    _sample_one("sb", sq, skn, svn, skc, svc, None, sz, so, ska, sva, past=past, t_new=t_new, keys=keys)


def _sample_one(kind, q_ref, kn_ref, vn_ref, kc_ref, vc_ref, lf_ref, z_ref, o_ref, ka_ref, va_ref, *,
                past, t_new, keys):
    rows = N_HEADS * t_new
    width = N_HEADS * HEAD_DIM
    pad = keys - past - t_new

    for src_c, src_n, dst in ((kc_ref, kn_ref, ka_ref), (vc_ref, vn_ref, va_ref)):
        dst[0:past, :] = src_c[0].reshape(past, width).astype(BF16)
        dst[past:past + t_new, :] = src_n[0]
        dst[past + t_new:keys, :] = jnp.zeros((pad, width), BF16)

    r_q = lax.broadcasted_iota(jnp.int32, (rows, width), 0)
    c_q = lax.broadcasted_iota(jnp.int32, (rows, width), 1)
    q_tiled = jnp.concatenate([q_ref[0]] * N_HEADS, axis=0)
    same_head = _shift_div(r_q, t_new) == _shift_div(c_q, HEAD_DIM)
    q_bd = jnp.where(same_head, q_tiled, jnp.zeros_like(q_tiled))

    qk = _dot_nt(q_bd, ka_ref[...])
    r_s = lax.broadcasted_iota(jnp.int32, (rows, keys), 0)
    c_s = lax.broadcasted_iota(jnp.int32, (rows, keys), 1)
    q_pos = past + (r_s & (t_new - 1))

    if kind == "fox":
        f_all = _cumsum_lanes(lf_ref[0], LANES)
        f_rows = jnp.concatenate(
            [jnp.broadcast_to(f_all[hh:hh + 1, :], (t_new, keys)) for hh in range(N_HEADS)], axis=0)
        s = jnp.where(c_s <= q_pos, qk - f_rows * LOG2E, -jnp.inf)
        m = jnp.max(s, axis=-1, keepdims=True)
        p = jnp.exp2(s - m)
        wts = p
        norm = jnp.sum(p, axis=-1, keepdims=True)
    else:
        w = qk
        before = c_s < q_pos
        tri = _tri_after(LANES)
        carry = jnp.zeros((rows, LANES), F32)
        chunks = [None] * (keys // LANES)
        for c in reversed(range(keys // LANES)):
            sl = slice(c * LANES, (c + 1) * LANES)
            chunks[c], carry = _sb_block(w[:, sl], carry, before[:, sl], tri)
        wts = jnp.concatenate(chunks, axis=1)
        norm = None

    o_full = _dot(wts.astype(BF16), va_ref[...])
    if norm is not None:
        o_full = o_full / norm
    o = jnp.concatenate(
        [o_full[hh * t_new:(hh + 1) * t_new, hh * HEAD_DIM:(hh + 1) * HEAD_DIM] for hh in range(N_HEADS)],
        axis=1)
    o_ref[0] = (o * _silu(z_ref[0].astype(F32))).astype(o_ref.dtype)


def sample_attention(fox, sb, logf_all):
    b, t_new, w = fox[1].shape
    past = fox[3].shape[1]
    assert t_new & (t_new - 1) == 0
    keys = -(-(past + t_new) // LANES) * LANES
    new_spec = pl.BlockSpec((1, t_new, w), lambda i: (i, 0, 0))
    part_spec = lambda part: pl.BlockSpec((1, t_new, w), lambda i: (i, 0, part))
    cache_spec = pl.BlockSpec((1, past, N_HEADS, HEAD_DIM), lambda i: (i, 0, 0, 0))
    logf_spec = pl.BlockSpec((1, HEAD_ROWS, keys), lambda i: (i, 0, 0))
    mixer_specs = lambda m: [part_spec(m[0][1]), new_spec, new_spec, cache_spec, cache_spec]
    vmem = 8 * _nbytes((past, w), F32) + 4 * _nbytes((keys, w), BF16)
    vmem += 24 * _nbytes((N_HEADS * t_new, keys), F32)
    return pl.pallas_call(
        functools.partial(_sample_kernel, past=past, t_new=t_new, keys=keys),
        grid=(b,),
        in_specs=(mixer_specs(fox) + [logf_spec, part_spec(fox[5][1])]
                  + mixer_specs(sb) + [part_spec(sb[5][1])]),
        out_specs=[new_spec, new_spec],
        out_shape=[jax.ShapeDtypeStruct((b, t_new, w), BF16)] * 2,
        scratch_shapes=[pltpu.VMEM((keys, w), BF16)] * 4,
        compiler_params=_params(1, vmem),
        name="sample_attention",
    )(fox[0][0], *fox[1:5], logf_all, fox[5][0], sb[0][0], *sb[1:5], sb[5][0])


def _causal_mix_weights(w_ref):
    n_groups, chunk, _ = w_ref.shape
    r = lax.broadcasted_iota(jnp.int32, (chunk, chunk), 0)
    c = lax.broadcasted_iota(jnp.int32, (chunk, chunk), 1)
    return [jnp.where(c <= r, w_ref[g], 0.0).astype(BF16) for g in range(n_groups)]


def _layer_norm(v, g_ref, b_ref):
    mu = jnp.mean(v, axis=-1, keepdims=True)
    vc = v - mu
    var = jnp.mean(vc * vc, axis=-1, keepdims=True)
    return vc * lax.rsqrt(var + LN_EPS) * g_ref[...] + b_ref[...]


def _mix_and_gate(w_causal, vnb, bias_ref, u_ref, z_ref, rows, store):
    group_dim = vnb.shape[1] // len(w_causal)
    for g, wg in enumerate(w_causal):
        sl = slice(g * group_dim, (g + 1) * group_dim)
        mixed = _dot(wg, vnb[:, sl]) + bias_ref[:, g:g + 1]
        store(sl, (u_ref[rows, sl].astype(F32) * mixed * z_ref[rows, sl].astype(F32)).astype(BF16))


def _sgu_kernel(u_ref, v_ref, z_ref, g_ref, b_ref, w_ref, bias_ref, o_ref, *vn_refs):
    chunk = w_ref.shape[1]
    w_causal = _causal_mix_weights(w_ref)
    for s in range(v_ref.shape[0] // chunk):
        rs = slice(s * chunk, (s + 1) * chunk)
        vn = _layer_norm(v_ref[rs, :].astype(F32), g_ref, b_ref)
        if vn_refs:
            vn_refs[0][rs, :] = vn

        def store(sl, val, rs=rs):
            o_ref[rs, sl] = val

        _mix_and_gate(w_causal, vn.astype(BF16), bias_ref, u_ref, z_ref, rs, store)


SGU_STEP_BYTES = 4 << 20


def spatial_gate(uv, gate, ln_g, ln_b, w_mix, bias_t, emit_vn):
    m, c = gate.shape
    n_groups, chunk, _ = w_mix.shape
    rows = min(m, chunk * max(1, SGU_STEP_BYTES // _nbytes((chunk, c), uv.dtype)))
    assert m % rows == 0
    blk = lambda j: pl.BlockSpec((rows, c), lambda i, j=j: (i, j))
    out_shape = [jax.ShapeDtypeStruct((m, c), BF16)]
    out_specs = [pl.BlockSpec((rows, c), lambda i: (i, 0))]
    if emit_vn:
        out_shape.append(jax.ShapeDtypeStruct((m, c), F32))
        out_specs.append(pl.BlockSpec((rows, c), lambda i: (i, 0)))
    vmem = 6 * _nbytes((rows, c), uv.dtype) + 2 * _nbytes((rows, c), BF16) + 8 * _nbytes((rows, c), F32)
    vmem += 2 * _nbytes(w_mix.shape, F32)
    return pl.pallas_call(
        _sgu_kernel,
        grid=(m // rows,),
        in_specs=[blk(0), blk(1), blk(0),
                  pl.BlockSpec((1, c), lambda i: (0, 0)),
                  pl.BlockSpec((1, c), lambda i: (0, 0)),
                  pl.BlockSpec((n_groups, chunk, chunk), lambda i: (0, 0, 0)),
                  pl.BlockSpec((chunk, n_groups), lambda i: (0, 0))],
        out_specs=out_specs,
        out_shape=out_shape,
        compiler_params=_params(1, vmem),
        name="spatial_gate",
    )(uv, uv, gate, ln_g.reshape(1, c), ln_b.reshape(1, c), w_mix, bias_t)


def _layer0_inputs(xn_prompt, xn_sample, w_t):
    row_sets = [([xn_prompt], BF16, None), ([xn_sample], BF16, None)]
    qz = project(row_sets, w_t, col_blocks=QZ_PARTS, w_transposed=True,
                 out_scale=(QK_SCALE, 1.0, QK_SCALE, 1.0), name="in_proj0_qz")
    kv, kv_out = ([], []), ([], [])
    for p in KV_PARTS:
        outs = project(row_sets, w_t, col_blocks=(p,), heads_out=True, w_transposed=True, name="in_proj0_kv")
        for g, out in enumerate(outs):
            kv[g].append(out[0])
            kv_out[g].append(out[1])
    return qz, kv, kv_out


def kernel(x_prompt, x_sample, cache_fox_k, cache_fox_v, cache_fox_logf, cache_sb_k, cache_sb_v,
           norm0_g, w_in0, b_forget, w_out0, norm1_g, w_in1, sgu_ln_g, sgu_ln_b, w_sp, b_sp, w_out1, final_g):
    bp, s, d = x_prompt.shape
    bs, t_new, _ = x_sample.shape
    past = cache_fox_k.shape[1]
    hp = x_prompt.reshape(bp * s, d)
    hs = x_sample.reshape(bs * t_new, d)

    n_main = 8 * MIX_WIDTH
    w_t = w_in0.T
    wf = jnp.pad(w_in0[:, n_main:], ((0, 0), (0, LANES - N_HEADS))).astype(BF16)
    bf = jnp.pad(b_forget, (0, LANES - N_HEADS)).reshape(1, LANES).astype(F32)

    xn_p, logf_t, cum_t = norm_forget(hp, norm0_g, wf, bf, bp, True)
    xn_s, logf_s = norm_forget(hs, norm0_g, wf, bf, 1, False)
    (qz_p, qz_s), ((fk, fv, sk, sv), (gk, gv, tk, tv)), (kv_prompt, kv_sample) = _layer0_inputs(xn_p, xn_s, w_t)

    as_seq = lambda a: a.reshape(bp, s, -1)
    qz_p = as_seq(qz_p)
    f4 = cum_t.reshape(bp, HEAD_ROWS, s // FOX_BLOCK, FOX_BLOCK)
    mix_f = prompt_attention("fox", (qz_p, 0), as_seq(fk), as_seq(fv), (qz_p, 1), f4)
    mix_s = prompt_attention("sb", (qz_p, 2), as_seq(sk), as_seq(sv), (qz_p, 3))
    fox_logf_prompt = jnp.swapaxes(logf_t[:, :N_HEADS, :], 1, 2)

    as_new = lambda a: a.reshape(bs, t_new, -1)
    qz_s = as_new(qz_s)
    logf_new_t = jnp.swapaxes(logf_s[0, :N_HEADS, :].reshape(N_HEADS, bs, t_new), 0, 1)
    keys = -(-(past + t_new) // LANES) * LANES
    logf_all = jnp.concatenate([jnp.swapaxes(cache_fox_logf.astype(F32), 1, 2), logf_new_t], axis=2)
    logf_all = jnp.pad(logf_all, ((0, 0), (0, HEAD_ROWS - N_HEADS), (0, keys - past - t_new)))
    mix_g, mix_t = sample_attention(
        ((qz_s, 0), as_new(gk), as_new(gv), cache_fox_k, cache_fox_v, (qz_s, 1)),
        ((qz_s, 2), as_new(tk), as_new(tv), cache_sb_k, cache_sb_v, (qz_s, 3)), logf_all)
    fox_logf_sample = jnp.swapaxes(logf_new_t, 1, 2)
    flat = lambda a: a.reshape(-1, MIX_WIDTH)
    w_out0_b = w_out0.astype(BF16)
    ((hp, xn1_p),) = project([([flat(mix_f), flat(mix_s)], F32, hp)], w_out0_b, tn=d, tm=512,
                             post_norm_g=norm1_g, post_norm_also=BF16, name="out_proj0")
    ((hs, xn1_s),) = project([([flat(mix_g), flat(mix_t)], F32, hs)], w_out0_b, tn=d, tm=512,
                             post_norm_g=norm1_g, post_norm_also=BF16, name="out_proj0")

    c_width = w_in1.shape[1] // 3
    bias_t = b_sp.T.astype(F32)
    layer1_rows = [([xn1_p], BF16, None), ([xn1_s], F32, None)]
    uv_p, uv_s = project(layer1_rows, w_in1, n_cols=2 * c_width, name="in_proj1_uv")
    gate_p, gate_s = project(layer1_rows, w_in1, col0=2 * c_width, out_silu=True, name="in_proj1_gate")
    (act_p,) = spatial_gate(uv_p, gate_p, sgu_ln_g, sgu_ln_b, w_sp, bias_t, False)

    per_blk = C_CHUNK // t_new
    w_corner = w_sp[:, :t_new, :t_new]
    blk_id = jnp.arange(C_CHUNK) // t_new
    w_bd = jnp.where(blk_id[:, None] == blk_id[None, :], jnp.tile(w_corner, (1, per_blk, per_blk)), 0.0)
    bias_bd = jnp.tile(b_sp[:, :t_new].T.astype(F32), (per_blk, 1))
    act_s, vn_s = spatial_gate(uv_s, gate_s, sgu_ln_g, sgu_ln_b, w_bd, bias_bd, True)
    w_out1_b = w_out1.astype(BF16)
    (y_prompt,) = project([([act_p], F32, hp)], w_out1_b, tn=d, tm=512, post_norm_g=final_g, name="out_proj1")
    (y_sample,) = project([([act_s], F32, hs)], w_out1_b, tn=d, tm=512, post_norm_g=final_g, name="out_proj1")
    y_prompt = y_prompt.reshape(bp, s, d)
    y_sample = y_sample.reshape(bs, t_new, d)
    fk4, fv4, sk4, sv4 = (a.reshape(bp, s, N_HEADS, HEAD_DIM) for a in kv_prompt)
    gk4, gv4, tk4, tv4 = (a.reshape(bs, t_new, N_HEADS, HEAD_DIM) for a in kv_sample)
    return (y_prompt, y_sample,
            fk4, fv4, fox_logf_prompt,
            gk4, gv4, fox_logf_sample,
            sk4, sv4,
            tk4, tv4,
            vn_s.reshape(bs, t_new, c_width))
```
